```python
import math
import jax
import jax.numpy as jnp
from jax import lax
import numpy as np

D_MODEL = 1024
BATCH = 32
SEQ = 256
DEPTH = 4
DEC_BATCH = 8
DEC_SEQ = 4096
PAST_LEN = 256

GRID_W = 64
W_A = D_MODEL // 2
W_B = D_MODEL // 2
W_C = D_MODEL // 2
W_D = D_MODEL // 2
W_MIX = W_A + W_B
SC_WIDTH = 3
LRU_CONV = 4
LRU_HEADS = 8
LRU_BLK = W_B // LRU_HEADS
RG_C = 8.0
HY_CONV = 3
HY_ORDER = 2
HY_BANDS = 16
HY_POS_DIM = 1 + 2 * HY_BANDS
HY_FFN = 64
HY_TARGET = 1e-2
HY_SHORT_PCT = 0.3
HY_LONG_PCT = 1.5
RW_HEAD = 64
RW_HEADS = W_D // RW_HEAD
RW_LORA_W = 64
RW_LORA_A = 64
D_FF = 4 * D_MODEL
N_AB = (DEPTH + 1) // 2
N_CD = DEPTH // 2
DN_ALPHA = (2 * DEPTH) ** 0.25
DN_BETA = (8 * DEPTH) ** -0.25
LN_EPS = 1e-5
GN_EPS = 64e-5

kernel_name = 'hybrid_diffusion_conv_lru_hyena_rwkv7_step'


def _layer_norm(x, g, b):
    xf = x.astype(jnp.float32)
    xc = xf - jnp.mean(xf, -1, keepdims=True)
    var = jnp.mean(xc * xc, -1, keepdims=True)
    return (xc * lax.rsqrt(var + LN_EPS) * g.astype(jnp.float32) + b.astype(jnp.float32)).astype(x.dtype)


def _dwconv(x, w, pad_left, line):
    K, L = w.shape[0], x.shape[1]
    xp = jnp.pad(x, ((0, 0), (pad_left, K - 1 - pad_left), (0, 0)))
    pos = None if line is None else jnp.arange(L) % line
    out = 0.0
    for k in range(K):
        off = k - pad_left
        term = xp[:, k:k + L] * w[k]
        if pos is not None and off != 0:
            valid = ((pos + off >= 0) & (pos + off < line))[None, :, None]
            term = jnp.where(valid, term, jnp.zeros_like(term))
        out = out + term
    return out


def _tshift(x, line):
    w = jnp.array([0.5, 0.0, 0.5], x.dtype)[:, None]
    return _dwconv(x, w, 1, line)


def _to_colmajor(t, rows):
    b, L, ch = t.shape
    return t.reshape(b, rows, GRID_W, ch).transpose(0, 2, 1, 3).reshape(b, L, ch)


def _from_colmajor(t, rows):
    b, L, ch = t.shape
    return t.reshape(b, GRID_W, rows, ch).transpose(0, 2, 1, 3).reshape(b, L, ch)


def _lin_combine(e1, e2):
    a1, b1 = e1
    a2, b2 = e2
    return a1 * a2, a2 * b1 + b2


def _rglru_dir(xc, wa, ba, wi, bi, lam, h0):
    b, L, ch = xc.shape
    xf = xc.astype(jnp.float32)
    xb = xf.reshape(b, L, LRU_HEADS, LRU_BLK)
    r = jax.nn.sigmoid(jnp.einsum('blhi,hij->blhj', xb, wa).reshape(b, L, ch) + ba)
    i = jax.nn.sigmoid(jnp.einsum('blhi,hij->blhj', xb, wi).reshape(b, L, ch) + bi)
    log_a = -RG_C * r * jax.nn.softplus(-lam.astype(jnp.float32))
    a = jnp.exp(log_a)
    u = jnp.sqrt(-jnp.expm1(2.0 * log_a)) * (i * xf)
    u = u.at[:, 0].add(a[:, 0] * h0.astype(jnp.float32))
    _, h = lax.associative_scan(_lin_combine, (a, u), axis=1)
    return h, h[:, -1]


def _rwkv_dir(r, w, k, v, kk, a, s0):
    xs = tuple(jnp.moveaxis(t, 1, 0) for t in (r, w, k, v, kk, a))

    def step(S, inp):
        r_t, w_t, k_t, v_t, kk_t, a_t = inp
        sa = jnp.einsum('bhvk,bhk->bhv', S, kk_t)
        S = (S * w_t[:, :, None, :] - sa[..., None] * (kk_t * a_t)[:, :, None, :]
             + v_t[..., None] * k_t[:, :, None, :])
        return S, jnp.einsum('bhvk,bhk->bhv', S, r_t)

    s_fin, o = lax.scan(step, s0, xs)
    return jnp.moveaxis(o, 0, 1), s_fin


def _hyena_filters(L, w1, b1, w2, b2, w3, freq):
    f32 = jnp.float32
    tn = jnp.linspace(0.0, 1.0, L, dtype=f32)
    tr = jnp.arange(L, dtype=f32)
    bands = jnp.linspace(1e-4, HY_BANDS - 1, HY_BANDS, dtype=f32)
    ang = (2.0 * math.pi / L) * tr[:, None] * bands[None, :]
    feats = jnp.concatenate([tn[:, None], jnp.cos(ang), -jnp.sin(ang)], -1)
    fr = freq.astype(f32)
    hid = jnp.sin(fr * (jnp.dot(feats, w1.astype(f32)) + b1))
    hid = jnp.sin(fr * (jnp.dot(hid, w2.astype(f32)) + b2))
    raw = jnp.dot(hid, w3.astype(f32)).reshape(L, HY_ORDER, 2, W_C)
    deltas = jnp.abs(jnp.linspace(math.log(HY_TARGET) / HY_LONG_PCT, math.log(HY_TARGET) / HY_SHORT_PCT, W_C, dtype=f32))
    decay = jnp.exp(-tn[:, None] * deltas[None, :])
    filt = raw * decay[:, None, None, :]
    circ = jnp.concatenate([filt[:, :, 0], jnp.zeros((1, HY_ORDER, W_C), f32), filt[:0:-1, :, 1]], 0)
    return circ * lax.rsqrt(jnp.sum(circ * circ, 0, keepdims=True) + 1e-6)


def _fft_conv(z, f):
    L = z.shape[1]
    n = 2 * L
    zf = jnp.fft.rfft(z.astype(jnp.float32), n=n, axis=1)
    ff = jnp.fft.rfft(f, n=n, axis=0)
    return jnp.fft.irfft(zf * ff[None], n=n, axis=1)[:, :L]


def _mixer_ab(h, j, h0, line, p):
    proj = jnp.dot(h, p['ab_w_in'][j])
    s_b, s_c, s_v, g_lru, x_lru = jnp.split(proj, [W_A, 2 * W_A, 3 * W_A, 3 * W_A + W_B], axis=-1)
    y_a = s_b * _dwconv(s_c * s_v, p['sc_conv'][j], 1, line)
    xc = _dwconv(x_lru, p['lru_conv'][j], 2, line) + p['lru_conv_b'][j]
    hf, sf = _rglru_dir(xc, p['lru_wa'][j, 0], p['lru_ba'][j, 0], p['lru_wi'][j, 0], p['lru_bi'][j, 0], p['lru_lambda'][j, 0], h0[:, 0])
    hb, sb = _rglru_dir(jnp.flip(xc, 1), p['lru_wa'][j, 1], p['lru_ba'][j, 1], p['lru_wi'][j, 1], p['lru_bi'][j, 1], p['lru_lambda'][j, 1], h0[:, 1])
    y_b = jax.nn.gelu(g_lru) * (hf + jnp.flip(hb, 1)).astype(h.dtype)
    return jnp.concatenate([y_a, y_b], -1), jnp.stack([sf, sb], 1)


def _mixer_cd(h, j, s0, line, p):
    f32 = jnp.float32
    bsz, L, _ = h.shape
    proj = jnp.dot(h, p['cd_w_in'][j])
    u = _dwconv(proj[..., :3 * W_C], p['hy_conv'][j], 1, line)
    hv, hx1, hx2 = jnp.split(u.astype(f32), 3, axis=-1)
    filt = _hyena_filters(L, p['hy_w1'][j], p['hy_b1'][j], p['hy_w2'][j], p['hy_b2'][j], p['hy_w3'][j], p['hy_freq'][j])
    bias = p['hy_bias'][j].astype(f32)
    z = hx1 * (_fft_conv(hv, filt[:, 0]) + bias[0] * hv)
    y_c = hx2 * (_fft_conv(z, filt[:, 1]) + bias[1] * z)

    mu = p['rw_mu'][j]
    r, k, v, g = [t + (_tshift(t, line) - t) * mu[n] for n, t in enumerate(jnp.split(proj[..., 3 * W_C:], 4, axis=-1))]
    dh = _tshift(h, line) - h
    xw = h + dh * p['rw_mu_x'][j, 0]
    xa = h + dh * p['rw_mu_x'][j, 1]

    def heads(t):
        return t.astype(f32).reshape(bsz, L, RW_HEADS, RW_HEAD)

    rh, vh = heads(r), heads(v)
    kk = heads(k * p['rw_kk'][j])
    kk = kk * lax.rsqrt(jnp.sum(kk * kk, -1, keepdims=True) + 1e-12)
    rk = p['rw_rk'][j].astype(f32)
    o_sum = 0.0
    bonus = 0.0
    finals = []
    for d in range(2):
        w_raw = -jax.nn.softplus(-(p['rw_w0'][j, d] + jnp.dot(jnp.tanh(jnp.dot(xw, p['rw_w1'][j, d])), p['rw_w2'][j, d]))) - 0.5
        decay = heads(jnp.exp(-jnp.exp(w_raw.astype(f32))))
        a = jax.nn.sigmoid(p['rw_a0'][j, d] + jnp.dot(jnp.dot(xa, p['rw_a1'][j, d]), p['rw_a2'][j, d]))
        kd = k * (1.0 + (a - 1.0) * p['rw_ka'][j])
        ah, kdh = heads(a), heads(kd)
        seq = (rh, decay, kdh, vh, kk, ah)
        if d == 1:
            seq = tuple(jnp.flip(t, 1) for t in seq)
        o, s_fin = _rwkv_dir(seq[0], seq[1], seq[2], seq[3], seq[4], seq[5], s0[:, d].astype(f32))
        if d == 1:
            o = jnp.flip(o, 1)
        o_sum = o_sum + o
        bonus = bonus + jnp.sum(rh * kdh * rk, -1, keepdims=True) * vh
        finals.append(s_fin)
    oc = o_sum - jnp.mean(o_sum, -1, keepdims=True)
    on = oc * lax.rsqrt(jnp.mean(oc * oc, -1, keepdims=True) + GN_EPS)
    on = on.reshape(bsz, L, W_D) * p['rw_gn_g'][j].astype(f32) + p['rw_gn_b'][j].astype(f32)
    y_d = (on + bonus.reshape(bsz, L, W_D)) * jax.nn.sigmoid(g.astype(f32))
    return jnp.concatenate([y_c, y_d], -1).astype(h.dtype), jnp.stack(finals, 1)


def _trunk(x, cvec, init_lru, init_rwkv, rows, p):
    bsz = x.shape[0]
    if init_lru is None:
        init_lru = jnp.zeros((bsz, N_AB, 2, W_B), jnp.float32)
        init_rwkv = jnp.zeros((bsz, N_CD, 2, RW_HEADS, RW_HEAD, RW_HEAD), jnp.float32)
    new_lru = []
    new_rwkv = []
    for l in range(DEPTH):
        j = l // 2
        mods = jnp.dot(jax.nn.silu(cvec), p['w_mod'][l]) + p['b_mod'][l]
        sh1, sc1, g1, sh2, sc2, g2 = [m[:, None, :] for m in jnp.split(mods, 6, axis=-1)]
        h = x * (1.0 + sc1) + sh1
        if l % 2 == 0:
            line = None if rows is None else GRID_W
            mix, st = _mixer_ab(h, j, init_lru[:, j], line, p)
            y = jnp.dot(mix, p['w_out'][l])
            new_lru.append(st)
        else:
            if rows is None:
                mix, st = _mixer_cd(h, j, init_rwkv[:, j], None, p)
                y = jnp.dot(mix, p['w_out'][l])
            else:
                mix, st = _mixer_cd(_to_colmajor(h, rows), j, init_rwkv[:, j], rows, p)
                y = _from_colmajor(jnp.dot(mix, p['w_out'][l]), rows)
            new_rwkv.append(st)
        x = _layer_norm(DN_ALPHA * x + g1 * y, p['ln1_g'][l], p['ln1_b'][l])
        h = x * (1.0 + sc2) + sh2
        y = jnp.dot(jnp.square(jax.nn.relu(jnp.dot(h, p['mlp_w1'][l]))), p['mlp_w2'][l])
        x = _layer_norm(DN_ALPHA * x + g2 * y, p['ln2_g'][l], p['ln2_b'][l])
    return x, jnp.stack(new_lru, 1), jnp.stack(new_rwkv, 1)


def setup_inputs(seed: int = 0) -> dict:
    key = jax.random.key(seed)
    ks = iter(jax.random.split(key, 64))
    f32 = jnp.float32

    def nrm(shape, s):
        return jax.random.normal(next(ks), shape, f32) * s

    def uni(shape, lo, hi):
        return jax.random.uniform(next(ks), shape, f32, lo, hi)

    lam_a = uni((N_AB, 2, W_B), 0.9, 0.999) ** (1.0 / RG_C)
    return {
        'x_prompt': nrm((BATCH, SEQ, D_MODEL), 1.0),
        'x_sample': nrm((DEC_BATCH, DEC_SEQ, D_MODEL), 1.0),
        'state_lru': nrm((DEC_BATCH, N_AB, 2, W_B), 0.5),
        'state_rwkv': nrm((DEC_BATCH, N_CD, 2, RW_HEADS, RW_HEAD, RW_HEAD), 0.5),
        'c': nrm((DEC_BATCH, D_MODEL), 1.0),
        'c_ctx': nrm((D_MODEL,), 1.0),
        'w_mod': nrm((DEPTH, D_MODEL, 6 * D_MODEL), 0.5 * D_MODEL ** -0.5),
        'b_mod': nrm((DEPTH, 6 * D_MODEL), 0.02),
        'ln1_g': 1.0 + nrm((DEPTH, D_MODEL), 0.02),
        'ln1_b': nrm((DEPTH, D_MODEL), 0.01),
        'ln2_g': 1.0 + nrm((DEPTH, D_MODEL), 0.02),
        'ln2_b': nrm((DEPTH, D_MODEL), 0.01),
        'mlp_w1': nrm((DEPTH, D_MODEL, D_FF), D_MODEL ** -0.5),
        'mlp_w2': nrm((DEPTH, D_FF, D_MODEL), DN_BETA * D_FF ** -0.5),
        'w_out': nrm((DEPTH, W_MIX, D_MODEL), DN_BETA * W_MIX ** -0.5),
        'ab_w_in': nrm((N_AB, D_MODEL, 3 * W_A + 2 * W_B), D_MODEL ** -0.5),
        'sc_conv': nrm((N_AB, SC_WIDTH, W_A), SC_WIDTH ** -0.5),
        'lru_conv': nrm((N_AB, LRU_CONV, W_B), LRU_CONV ** -0.5),
        'lru_conv_b': nrm((N_AB, W_B), 0.01),
        'lru_wa': nrm((N_AB, 2, LRU_HEADS, LRU_BLK, LRU_BLK), LRU_BLK ** -0.5),
        'lru_ba': nrm((N_AB, 2, W_B), 0.01),
        'lru_wi': nrm((N_AB, 2, LRU_HEADS, LRU_BLK, LRU_BLK), LRU_BLK ** -0.5),
        'lru_bi': nrm((N_AB, 2, W_B), 0.01),
        'lru_lambda': jnp.log(lam_a) - jnp.log1p(-lam_a),
        'cd_w_in': nrm((N_CD, D_MODEL, 3 * W_C + 4 * W_D), D_MODEL ** -0.5),
        'hy_conv': nrm((N_CD, HY_CONV, 3 * W_C), HY_CONV ** -0.5),
        'hy_w1': nrm((N_CD, HY_POS_DIM, HY_FFN), HY_POS_DIM ** -0.5),
        'hy_b1': nrm((N_CD, HY_FFN), 0.01),
        'hy_w2': nrm((N_CD, HY_FFN, HY_FFN), HY_FFN ** -0.5),
        'hy_b2': nrm((N_CD, HY_FFN), 0.01),
        'hy_w3': nrm((N_CD, HY_FFN, HY_ORDER * 2 * W_C), HY_FFN ** -0.5),
        'hy_freq': 1.0 + nrm((N_CD, HY_FFN), 0.1),
        'hy_bias': nrm((N_CD, HY_ORDER, W_C), 0.1),
        'rw_mu': uni((N_CD, 4, W_D), 0.0, 1.0),
        'rw_mu_x': uni((N_CD, 2, D_MODEL), 0.0, 1.0),
        'rw_w0': uni((N_CD, 2, W_D), -6.0, -1.0),
        'rw_w1': nrm((N_CD, 2, D_MODEL, RW_LORA_W), D_MODEL ** -0.5),
        'rw_w2': nrm((N_CD, 2, RW_LORA_W, W_D), 0.1 * RW_LORA_W ** -0.5),
        'rw_a0': nrm((N_CD, 2, W_D), 0.1),
        'rw_a1': nrm((N_CD, 2, D_MODEL, RW_LORA_A), D_MODEL ** -0.5),
        'rw_a2': nrm((N_CD, 2, RW_LORA_A, W_D), 0.1 * RW_LORA_A ** -0.5),
        'rw_kk': 0.85 + nrm((N_CD, W_D), 0.05),
        'rw_ka': 1.0 + nrm((N_CD, W_D), 0.05),
        'rw_rk': nrm((N_CD, RW_HEADS, RW_HEAD), 0.1),
        'rw_gn_g': 1.0 + nrm((N_CD, W_D), 0.02),
        'rw_gn_b': nrm((N_CD, W_D), 0.01),
    }


def reference(x_prompt, x_sample, state_lru, state_rwkv, c, c_ctx, w_mod, b_mod, ln1_g, ln1_b, ln2_g, ln2_b,
              mlp_w1, mlp_w2, w_out, ab_w_in, sc_conv, lru_conv, lru_conv_b, lru_wa, lru_ba, lru_wi, lru_bi,
              lru_lambda, cd_w_in, hy_conv, hy_w1, hy_b1, hy_w2, hy_b2, hy_w3, hy_freq, hy_bias, rw_mu, rw_mu_x,
              rw_w0, rw_w1, rw_w2, rw_a0, rw_a1, rw_a2, rw_kk, rw_ka, rw_rk, rw_gn_g, rw_gn_b):
    p = dict(w_mod=w_mod, b_mod=b_mod, ln1_g=ln1_g, ln1_b=ln1_b, ln2_g=ln2_g, ln2_b=ln2_b,
             mlp_w1=mlp_w1, mlp_w2=mlp_w2, w_out=w_out, ab_w_in=ab_w_in, sc_conv=sc_conv,
             lru_conv=lru_conv, lru_conv_b=lru_conv_b, lru_wa=lru_wa, lru_ba=lru_ba, lru_wi=lru_wi,
             lru_bi=lru_bi, lru_lambda=lru_lambda, cd_w_in=cd_w_in, hy_conv=hy_conv, hy_w1=hy_w1,
             hy_b1=hy_b1, hy_w2=hy_w2, hy_b2=hy_b2, hy_w3=hy_w3, hy_freq=hy_freq, hy_bias=hy_bias,
             rw_mu=rw_mu, rw_mu_x=rw_mu_x, rw_w0=rw_w0, rw_w1=rw_w1, rw_w2=rw_w2, rw_a0=rw_a0,
             rw_a1=rw_a1, rw_a2=rw_a2, rw_kk=rw_kk, rw_ka=rw_ka, rw_rk=rw_rk, rw_gn_g=rw_gn_g,
             rw_gn_b=rw_gn_b)
    y_prompt, new_state_lru, new_state_rwkv = _trunk(x_prompt, c_ctx[None, :], None, None, None, p)
    rows = x_sample.shape[1] // GRID_W
    y_sample, _, _ = _trunk(x_sample, c, state_lru, state_rwkv, rows, p)
    return (y_prompt, y_sample, new_state_lru, new_state_rwkv)
```

```python
import functools
import math

import jax
import jax.numpy as jnp
import numpy as np
from jax import lax
from jax.experimental import pallas as pl
from jax.experimental.pallas import tpu as pltpu

F32 = jnp.float32
BF16 = jnp.bfloat16
HIGHEST = lax.Precision.HIGHEST

D_MODEL = 1024
DEPTH = 4
GRID_W = 64
W_HALF = D_MODEL // 2
LRU_HEADS = 8
RG_C = 8.0
HY_BANDS = 16
HY_TARGET = 1e-2
HY_SHORT_PCT = 0.3
HY_LONG_PCT = 1.5
RW_HEAD = 64
RW_HEADS = W_HALF // RW_HEAD
D_FF = 4 * D_MODEL
DN_ALPHA = (2 * DEPTH) ** 0.25
LN_EPS = 1e-5
GN_EPS = 64e-5

TM = 256
VMEM_LIMIT = 56 * 1024 * 1024
DFT_P = 128
FF_CHUNK = 1024


def _cparams(sem):
    return pltpu.CompilerParams(dimension_semantics=sem, vmem_limit_bytes=VMEM_LIMIT)


def _shift_rows(x, off, line):
    if off == 0:
        return x
    n = x.shape[0]
    rolled = pltpu.roll(x, (-off) % n, 0)
    pos = lax.broadcasted_iota(jnp.int32, x.shape, 0) & (line - 1)
    valid = (pos + off >= 0) if off < 0 else (pos + off < line)
    return jnp.where(valid, rolled, 0.0)


def _dwconv_rows(x, w, pad_left, line):
    out = None
    for k in range(w.shape[0]):
        term = _shift_rows(x, k - pad_left, line) * w[k:k + 1]
        out = term if out is None else out + term
    return out


def _tshift_rows(x, line):
    return 0.5 * (_shift_rows(x, -1, line) + _shift_rows(x, 1, line))


def _layer_norm(v, g, b):
    mu = jnp.mean(v, -1, keepdims=True)
    vc = v - mu
    var = jnp.mean(vc * vc, -1, keepdims=True)
    return vc * lax.rsqrt(var + LN_EPS) * g + b


def _dot_bf16(a, b):
    return jnp.dot(a.astype(BF16), b, preferred_element_type=F32)


def _group_sum(x, ones):
    hi = x.astype(BF16)
    lo = (x - hi.astype(F32)).astype(BF16)
    return (jnp.dot(hi, ones, preferred_element_type=F32)
            + jnp.dot(lo, ones, preferred_element_type=F32))


def _softplus(x):
    return jnp.maximum(x, 0.0) + jnp.log(1.0 + jnp.exp(-jnp.abs(x)))


def _sigmoid(x):
    return 1.0 / (1.0 + jnp.exp(-x))


def _mods_kernel(c_ref, w_ref, b_ref, o_ref):
    c = c_ref[...]
    s = c * _sigmoid(c)
    o_ref[0, 0] = jnp.dot(s, w_ref[0], precision=HIGHEST, preferred_element_type=F32) + b_ref[0, 0]


def _mods(cvec, w_mod, b_mod):
    r = cvec.shape[0]
    return pl.pallas_call(
        _mods_kernel,
        grid=(DEPTH, 6),
        in_specs=[pl.BlockSpec((r, D_MODEL), lambda l, n: (0, 0)),
                  pl.BlockSpec((1, D_MODEL, D_MODEL), lambda l, n: (l, 0, n)),
                  pl.BlockSpec((1, 1, 1, D_MODEL), lambda l, n: (l, n, 0, 0))],
        out_specs=pl.BlockSpec((1, 1, r, D_MODEL), lambda l, n: (l, n, 0, 0)),
        out_shape=jax.ShapeDtypeStruct((DEPTH, 6, r, D_MODEL), F32),
        compiler_params=_cparams(("parallel", "parallel")),
        name="mods",
    )(cvec, w_mod, b_mod.reshape(DEPTH, 6, 1, D_MODEL))


def _ab_in_kernel(x_ref, mod_ref, w_ref, scw_ref, lcw_ref, lcb_ref, wg_ref, bg_ref, nsp_ref,
                  ya_ref, gg_ref, a_ref, u_ref, *, line):
    x = x_ref[...]
    m = mod_ref[0]
    h = x * (1.0 + m[1:2]) + m[0:1]
    proj = _dot_bf16(h, w_ref[...])
    w = W_HALF
    s_b, s_c, s_v = proj[:, 0:w], proj[:, w:2 * w], proj[:, 2 * w:3 * w]
    g_lru, x_lru = proj[:, 3 * w:4 * w], proj[:, 4 * w:5 * w]
    ya_ref[...] = s_b * _dwconv_rows(s_c * s_v, scw_ref[...], 1, line)
    gg_ref[...] = 0.5 * g_lru * (1.0 + jnp.tanh(
        math.sqrt(2.0 / math.pi) * (g_lru + 0.044715 * (g_lru * g_lru * g_lru))))
    xc = _dwconv_rows(x_lru, lcw_ref[...], 2, line) + lcb_ref[...]
    gates = _dot_bf16(xc, wg_ref[...]) + bg_ref[...]
    for d in range(2):
        r = _sigmoid(gates[:, (2 * d) * w:(2 * d + 1) * w])
        i = _sigmoid(gates[:, (2 * d + 1) * w:(2 * d + 2) * w])
        a = jnp.exp(-RG_C * r * nsp_ref[d:d + 1])
        a_ref[d] = a
        u_ref[d] = jnp.sqrt(1.0 - a * a) * (i * xc)


def _ab_in(x, mods, w_in, scw, lcw, lcb, wg, bg, nsp, line):
    n = x.shape[0]
    tpb = n // mods.shape[0] // TM
    tok = lambda i: (i, 0)
    const2 = lambda i: (0, 0)
    half = jax.ShapeDtypeStruct((n, W_HALF), F32)
    both = jax.ShapeDtypeStruct((2, n, W_HALF), F32)
    return pl.pallas_call(
        functools.partial(_ab_in_kernel, line=line),
        grid=(n // TM,),
        in_specs=[pl.BlockSpec((TM, D_MODEL), tok),
                  pl.BlockSpec((1, 6, D_MODEL), lambda i: (i // tpb, 0, 0)),
                  pl.BlockSpec(w_in.shape, const2),
                  pl.BlockSpec(scw.shape, const2),
                  pl.BlockSpec(lcw.shape, const2),
                  pl.BlockSpec(lcb.shape, const2),
                  pl.BlockSpec(wg.shape, const2),
                  pl.BlockSpec(bg.shape, const2),
                  pl.BlockSpec(nsp.shape, const2)],
        out_specs=[pl.BlockSpec((TM, W_HALF), tok), pl.BlockSpec((TM, W_HALF), tok),
                   pl.BlockSpec((2, TM, W_HALF), lambda i: (0, i, 0)),
                   pl.BlockSpec((2, TM, W_HALF), lambda i: (0, i, 0))],
        out_shape=[half, half, both, both],
        compiler_params=_cparams(("parallel",)),
        name="ab_in",
    )(x, mods, w_in, scw, lcw, lcb, wg, bg, nsp)


def _lru_scan_kernel(af_ref, uf_ref, ab_ref, ub_ref, h0_ref, hf_ref, hb_ref, fin_ref, carry, *, tc, nc):
    c = pl.program_id(1)

    @pl.when(c == 0)
    def _():
        carry[...] = h0_ref[0]

    def step(s, hs):
        hf, hb = hs
        tb = tc - 1 - s
        hf = af_ref[0, 0, pl.ds(s, 1), :] * hf + uf_ref[0, 0, pl.ds(s, 1), :]
        hb = ab_ref[0, 0, pl.ds(tb, 1), :] * hb + ub_ref[0, 0, pl.ds(tb, 1), :]
        hf_ref[0, pl.ds(s, 1), :] = hf
        hb_ref[0, pl.ds(tb, 1), :] = hb
        return hf, hb

    hf, hb = lax.fori_loop(0, tc, step, (carry[0:1], carry[1:2]), unroll=8)
    carry[0:1] = hf
    carry[1:2] = hb

    @pl.when(c == nc - 1)
    def _():
        fin_ref[0, 0:1] = hf
        fin_ref[0, 1:2] = hb


def _lru_scan(a, u, h0, bsz, seq):
    tc = min(seq, 512)
    nc = seq // tc
    a = a.reshape(2, bsz, seq, W_HALF)
    u = u.reshape(2, bsz, seq, W_HALF)
    fwd = lambda b, c: (0, b, c, 0)
    bwd = lambda b, c: (1, b, nc - 1 - c, 0)
    blk = (1, 1, tc, W_HALF)
    seq_shape = jax.ShapeDtypeStruct((bsz, seq, W_HALF), F32)
    return pl.pallas_call(
        functools.partial(_lru_scan_kernel, tc=tc, nc=nc),
        grid=(bsz, nc),
        in_specs=[pl.BlockSpec(blk, fwd), pl.BlockSpec(blk, fwd),
                  pl.BlockSpec(blk, bwd), pl.BlockSpec(blk, bwd),
                  pl.BlockSpec((1, 2, W_HALF), lambda b, c: (b, 0, 0))],
        out_specs=[pl.BlockSpec((1, tc, W_HALF), lambda b, c: (b, c, 0)),
                   pl.BlockSpec((1, tc, W_HALF), lambda b, c: (b, nc - 1 - c, 0)),
                   pl.BlockSpec((1, 2, W_HALF), lambda b, c: (b, 0, 0))],
        out_shape=[seq_shape, seq_shape, jax.ShapeDtypeStruct((bsz, 2, W_HALF), F32)],
        scratch_shapes=[pltpu.VMEM((2, W_HALF), F32)],
        compiler_params=_cparams(("parallel", "arbitrary")),
        name="lru_scan",
    )(a, u, a, u, h0)


def _ab_out_kernel(x_ref, mod_ref, ya_ref, gg_ref, hf_ref, hb_ref, w_ref, g_ref, b_ref, o_ref):
    x = x_ref[...]
    m = mod_ref[0]
    yb = gg_ref[...] * (hf_ref[...] + hb_ref[...])
    y = _dot_bf16(ya_ref[...], w_ref[0:W_HALF, :]) + _dot_bf16(yb, w_ref[W_HALF:, :])
    o_ref[...] = _layer_norm(DN_ALPHA * x + m[2:3] * y, g_ref[...], b_ref[...])


def _ab_out(x, mods, ya, gg, hf, hb, w_out, ln_g, ln_b):
    n = x.shape[0]
    tpb = n // mods.shape[0] // TM
    tok = lambda i: (i, 0)
    const2 = lambda i: (0, 0)
    half = pl.BlockSpec((TM, W_HALF), tok)
    return pl.pallas_call(
        _ab_out_kernel,
        grid=(n // TM,),
        in_specs=[pl.BlockSpec((TM, D_MODEL), tok),
                  pl.BlockSpec((1, 6, D_MODEL), lambda i: (i // tpb, 0, 0)),
                  half, half, half, half,
                  pl.BlockSpec(w_out.shape, const2),
                  pl.BlockSpec((1, D_MODEL), const2), pl.BlockSpec((1, D_MODEL), const2)],
        out_specs=pl.BlockSpec((TM, D_MODEL), tok),
        out_shape=jax.ShapeDtypeStruct((n, D_MODEL), F32),
        compiler_params=_cparams(("parallel",)),
        name="ab_out",
    )(x, mods, ya, gg, hf, hb, w_out, ln_g, ln_b)


def _mlp_kernel(x_ref, mod_ref, w1_ref, w2_ref, g_ref, b_ref, o_ref):
    x = x_ref[...]
    m = mod_ref[0]
    h = (x * (1.0 + m[4:5]) + m[3:4]).astype(BF16)
    y = None
    for c in range(D_FF // FF_CHUNK):
        t = jnp.dot(h, w1_ref[:, c * FF_CHUNK:(c + 1) * FF_CHUNK], preferred_element_type=F32)
        t = jnp.maximum(t, 0.0)
        part = _dot_bf16(t * t, w2_ref[c * FF_CHUNK:(c + 1) * FF_CHUNK, :])
        y = part if y is None else y + part
    o_ref[...] = _layer_norm(DN_ALPHA * x + m[5:6] * y, g_ref[...], b_ref[...])


def _mlp(x, mods, w1, w2, ln_g, ln_b):
    n = x.shape[0]
    tpb = n // mods.shape[0] // TM
    tok = lambda i: (i, 0)
    const2 = lambda i: (0, 0)
    return pl.pallas_call(
        _mlp_kernel,
        grid=(n // TM,),
        in_specs=[pl.BlockSpec((TM, D_MODEL), tok),
                  pl.BlockSpec((1, 6, D_MODEL), lambda i: (i // tpb, 0, 0)),
                  pl.BlockSpec(w1.shape, const2, pipeline_mode=pl.Buffered(1)),
                  pl.BlockSpec(w2.shape, const2, pipeline_mode=pl.Buffered(1)),
                  pl.BlockSpec((1, D_MODEL), const2), pl.BlockSpec((1, D_MODEL), const2)],
        out_specs=pl.BlockSpec((TM, D_MODEL), tok),
        out_shape=jax.ShapeDtypeStruct((n, D_MODEL), F32),
        compiler_params=_cparams(("parallel",)),
        name="mlp",
    )(x, mods, w1, w2, ln_g, ln_b)


def _cd_in_kernel(x_ref, mod_ref, w_ref, hyc_ref, mu_ref, mux_ref, l1_ref, l2_ref, l0_ref,
                  kkp_ref, ka_ref, rk_ref, ones_ref,
                  hv_ref, hx1_ref, hx2_ref, r_ref, v_ref, kk_ref, w_out_ref, kd_ref, b_ref,
                  bonus_ref, sg_ref, *, line):
    x = x_ref[...]
    m = mod_ref[0]
    h = x * (1.0 + m[1:2]) + m[0:1]
    proj = _dot_bf16(h, w_ref[...])
    w = W_HALF
    u = _dwconv_rows(proj[:, 0:3 * w], hyc_ref[...], 1, line)
    hv_ref[...] = u[:, 0:w]
    hx1_ref[...] = u[:, w:2 * w]
    hx2_ref[...] = u[:, 2 * w:3 * w]

    mixed = []
    for n in range(4):
        t = proj[:, (3 + n) * w:(4 + n) * w]
        mixed.append(t + (_tshift_rows(t, line) - t) * mu_ref[n:n + 1])
    r, k, v, g = mixed
    dh = _tshift_rows(h, line) - h
    xw = h + dh * mux_ref[0:1]
    xa = h + dh * mux_ref[1:2]
    tw = jnp.tanh(_dot_bf16(xw, l1_ref[:, 0:128]))
    ta = _dot_bf16(xa, l1_ref[:, 128:256])
    zw = _dot_bf16(tw, l2_ref[0:128, :]) + l0_ref[0:1]
    za = _dot_bf16(ta, l2_ref[128:256, :]) + l0_ref[1:2]

    ones = ones_ref[...]
    kk = k * kkp_ref[...]
    kk = kk * lax.rsqrt(_group_sum(kk * kk, ones) + 1e-12)
    r_ref[...] = r
    v_ref[...] = v
    kk_ref[...] = kk
    kd_sum = None
    for d in range(2):
        w_raw = -_softplus(-zw[:, d * w:(d + 1) * w]) - 0.5
        w_out_ref[d] = jnp.exp(-jnp.exp(w_raw))
        a = _sigmoid(za[:, d * w:(d + 1) * w])
        kd = k * (1.0 + (a - 1.0) * ka_ref[...])
        kd_ref[d] = kd
        b_ref[d] = kk * a
        kd_sum = kd if kd_sum is None else kd_sum + kd
    bonus_ref[...] = _group_sum(r * kd_sum * rk_ref[...], ones) * v
    sg_ref[...] = _sigmoid(g)


def _cd_in(x, mods, w_in, hyc, mu, mux, l1, l2, l0, kkp, ka, rk, ones, line):
    n = x.shape[0]
    tpb = n // mods.shape[0] // TM
    tok = lambda i: (i, 0)
    const2 = lambda i: (0, 0)
    half = jax.ShapeDtypeStruct((n, W_HALF), F32)
    both = jax.ShapeDtypeStruct((2, n, W_HALF), F32)
    half_spec = pl.BlockSpec((TM, W_HALF), tok)
    both_spec = pl.BlockSpec((2, TM, W_HALF), lambda i: (0, i, 0))
    consts = [w_in, hyc, mu, mux, l1, l2, l0, kkp, ka, rk, ones]
    return pl.pallas_call(
        functools.partial(_cd_in_kernel, line=line),
        grid=(n // TM,),
        in_specs=[pl.BlockSpec((TM, D_MODEL), tok),
                  pl.BlockSpec((1, 6, D_MODEL), lambda i: (i // tpb, 0, 0))]
                 + [pl.BlockSpec(c.shape, const2) for c in consts],
        out_specs=[half_spec] * 6 + [both_spec] * 3 + [half_spec] * 2,
        out_shape=[half] * 6 + [both] * 3 + [half] * 2,
        compiler_params=_cparams(("parallel",)),
        name="cd_in",
    )(x, mods, *consts)


def _rwkv_scan_kernel(rf_ref, vf_ref, kkf_ref, wf_ref, kdf_ref, bf_ref,
                      rb_ref, vb_ref, kkb_ref, wb_ref, kdb_ref, bb_ref,
                      s0_ref, ones_ref, eye_ref,
                      of_ref, ob_ref, fin_ref, s_scr, o_scr, *, tc, nc, gb):
    c = pl.program_id(1)
    nsub = tc // RW_HEAD
    grp = 8
    per_sub = RW_HEAD // grp

    @pl.when(c == 0)
    def _():
        s_scr[...] = s0_ref[...]
        o_scr[...] = jnp.zeros_like(o_scr)

    ones = ones_ref[...]
    lane_pos = lax.broadcasted_iota(jnp.int32, (RW_HEAD, 128), 1) & (RW_HEAD - 1)
    chains = []
    for g in range(gb):
        chains.append((g, 0, rf_ref, vf_ref, kkf_ref, wf_ref, kdf_ref, bf_ref))
        chains.append((g, 1, rb_ref, vb_ref, kkb_ref, wb_ref, kdb_ref, bb_ref))

    def group(i, carry):
        sub = i // per_sub
        base = (i % per_sub) * grp
        for (g, d, r_ref, v_ref, kk_ref, w_ref, kd_ref, b_ref) in chains:
            t0 = pl.multiple_of(i * grp if d == 0 else tc - grp - i * grp, grp)
            rows = pl.ds(t0, grp)
            for j in range(W_HALF // 128):
                ls = pl.ds(j * 128, 128)
                kap8 = kk_ref[g, rows, ls]
                rr8 = r_ref[g, rows, ls]
                vv8 = v_ref[g, rows, ls]
                ww8 = w_ref[0, g, rows, ls]
                kd8 = kd_ref[0, g, rows, ls]
                bb8 = b_ref[0, g, rows, ls]
                for s in range(grp):
                    q = s if d == 0 else grp - 1 - s
                    pos = base + s if d == 0 else RW_HEAD - 1 - (base + s)
                    st = s_scr[g, d, :, ls]
                    sa = _dot_bf16(st * kap8[q:q + 1], ones)
                    vcol = _dot_bf16(eye_ref[:, ls] * vv8[q:q + 1], ones)
                    st = st * ww8[q:q + 1] - sa * bb8[q:q + 1] + vcol * kd8[q:q + 1]
                    s_scr[g, d, :, ls] = st
                    ob = _dot_bf16(st * rr8[q:q + 1], ones)
                    o_scr[g, d, :, ls] = jnp.where(lane_pos == pos, ob, o_scr[g, d, :, ls])

        @pl.when(i % per_sub == per_sub - 1)
        def _():
            for g in range(gb):
                of_ref[g, sub] = o_scr[g, 0]
                ob_ref[g, nsub - 1 - sub] = o_scr[g, 1]

        return carry

    lax.fori_loop(0, tc // grp, group, 0)

    @pl.when(c == nc - 1)
    def _():
        fin_ref[...] = s_scr[...]


def _rwkv_scan(r, v, kk, w, kd, b, s0, ones2, eye, bsz, seq):
    tc = 128
    nc = seq // tc
    gb = 2
    r, v, kk = (t.reshape(bsz, seq, W_HALF) for t in (r, v, kk))
    w, kd, b = (t.reshape(2, bsz, seq, W_HALF) for t in (w, kd, b))
    shared = (gb, tc, W_HALF)
    perdir = (1, gb, tc, W_HALF)
    sf = lambda i, c: (i, c, 0)
    sb = lambda i, c: (i, nc - 1 - c, 0)
    pf = lambda i, c: (0, i, c, 0)
    pb = lambda i, c: (1, i, nc - 1 - c, 0)
    nsub = tc // RW_HEAD
    o_shape = jax.ShapeDtypeStruct((bsz, seq // RW_HEAD, RW_HEAD, W_HALF), F32)
    o_blk = (gb, nsub, RW_HEAD, W_HALF)
    st_blk = (gb, 2, RW_HEAD, W_HALF)
    return pl.pallas_call(
        functools.partial(_rwkv_scan_kernel, tc=tc, nc=nc, gb=gb),
        grid=(bsz // gb, nc),
        in_specs=[pl.BlockSpec(shared, sf)] * 3 + [pl.BlockSpec(perdir, pf)] * 3
                 + [pl.BlockSpec(shared, sb)] * 3 + [pl.BlockSpec(perdir, pb)] * 3
                 + [pl.BlockSpec(st_blk, lambda i, c: (i, 0, 0, 0)),
                    pl.BlockSpec(ones2.shape, lambda i, c: (0, 0)),
                    pl.BlockSpec(eye.shape, lambda i, c: (0, 0))],
        out_specs=[pl.BlockSpec(o_blk, lambda i, c: (i, c, 0, 0)),
                   pl.BlockSpec(o_blk, lambda i, c: (i, nc - 1 - c, 0, 0)),
                   pl.BlockSpec(st_blk, lambda i, c: (i, 0, 0, 0))],
        out_shape=[o_shape, o_shape, jax.ShapeDtypeStruct((bsz, 2, RW_HEAD, W_HALF), F32)],
        scratch_shapes=[pltpu.VMEM(st_blk, F32), pltpu.VMEM(st_blk, F32)],
        compiler_params=_cparams(("parallel", "arbitrary")),
        name="rwkv_scan",
    )(r, v, kk, w, kd, b, r, v, kk, w, kd, b, s0, ones2, eye)


def _cd_out_kernel(x_ref, mod_ref, yc_ref, of_ref, ob_ref, bonus_ref, sg_ref, gng_ref, gnb_ref,
                   ones_ref, w_ref, g_ref, b_ref, o_ref):
    x = x_ref[...]
    m = mod_ref[0]
    ones = ones_ref[...]
    o = of_ref[...] + ob_ref[...]
    oc = o - _group_sum(o, ones) * (1.0 / RW_HEAD)
    on = oc * lax.rsqrt(_group_sum(oc * oc, ones) * (1.0 / RW_HEAD) + GN_EPS)
    yd = (on * gng_ref[...] + gnb_ref[...] + bonus_ref[...]) * sg_ref[...]
    y = _dot_bf16(yc_ref[...], w_ref[0:W_HALF, :]) + _dot_bf16(yd, w_ref[W_HALF:, :])
    o_ref[...] = _layer_norm(DN_ALPHA * x + m[2:3] * y, g_ref[...], b_ref[...])


def _cd_out(x, mods, yc, of, ob, bonus, sg, gng, gnb, ones, w_out, ln_g, ln_b):
    n = x.shape[0]
    tpb = n // mods.shape[0] // TM
    tok = lambda i: (i, 0)
    const2 = lambda i: (0, 0)
    half = pl.BlockSpec((TM, W_HALF), tok)
    vec = pl.BlockSpec((1, W_HALF), const2)
    return pl.pallas_call(
        _cd_out_kernel,
        grid=(n // TM,),
        in_specs=[pl.BlockSpec((TM, D_MODEL), tok),
                  pl.BlockSpec((1, 6, D_MODEL), lambda i: (i // tpb, 0, 0)),
                  half, half, half, half, half, vec, vec,
                  pl.BlockSpec(ones.shape, const2),
                  pl.BlockSpec(w_out.shape, const2),
                  pl.BlockSpec((1, D_MODEL), const2), pl.BlockSpec((1, D_MODEL), const2)],
        out_specs=pl.BlockSpec((TM, D_MODEL), tok),
        out_shape=jax.ShapeDtypeStruct((n, D_MODEL), F32),
        compiler_params=_cparams(("parallel",)),
        name="cd_out",
    )(x, mods, yc, of, ob, bonus, sg, gng, gnb, ones, w_out, ln_g, ln_b)


def _hyena_consts(seq):
    tn = np.linspace(0.0, 1.0, seq, dtype=np.float32)
    tr = np.arange(seq, dtype=np.float32)
    bands = np.linspace(1e-4, HY_BANDS - 1, HY_BANDS, dtype=np.float32)
    ang = np.float32(2.0 * math.pi / seq) * tr[:, None] * bands[None, :]
    feats = np.concatenate([tn[:, None], np.cos(ang), -np.sin(ang)], -1).astype(np.float32)
    feats = np.pad(feats, ((0, 0), (0, 40 - feats.shape[1])))
    deltas = np.abs(np.linspace(math.log(HY_TARGET) / HY_LONG_PCT, math.log(HY_TARGET) / HY_SHORT_PCT,
                                W_HALF, dtype=np.float32))
    decay = np.exp(-tn[:, None] * deltas[None, :]).astype(np.float32)
    return feats, decay


def _filt_kernel(f_ref, dec_ref, w1_ref, b1_ref, w2_ref, b2_ref, w3_ref, fr_ref, o_ref, ssq_ref):
    i = pl.program_id(0)
    fr = fr_ref[...]
    hid = jnp.sin(fr * (jnp.dot(f_ref[...], w1_ref[...], precision=HIGHEST,
                                preferred_element_type=F32) + b1_ref[...]))
    hid = jnp.sin(fr * (jnp.dot(hid, w2_ref[...], precision=HIGHEST,
                                preferred_element_type=F32) + b2_ref[...]))
    raw = jnp.dot(hid, w3_ref[...], precision=HIGHEST, preferred_element_type=F32)
    dec = dec_ref[...]
    rows = raw.shape[0]
    first = (lax.broadcasted_iota(jnp.int32, (rows, W_HALF), 0) == 0) & (i == 0)
    parts = []
    for n in range(4):
        f = raw[:, n * W_HALF:(n + 1) * W_HALF] * dec
        o_ref[:, n * W_HALF:(n + 1) * W_HALF] = f
        if n % 2 == 1:
            f = jnp.where(first, 0.0, f)
        parts.append(jnp.sum(f * f, 0, keepdims=True))
    ssq = jnp.concatenate(parts, -1)

    @pl.when(i == 0)
    def _():
        ssq_ref[...] = ssq

    @pl.when(i > 0)
    def _():
        ssq_ref[...] += ssq


def _hyena_filters(seq, w1, b1, w2, b2, w3, freq):
    feats, decay = _hyena_consts(seq)
    w1 = jnp.pad(w1, ((0, 40 - w1.shape[0]), (0, 0)))
    rows = 256
    const2 = lambda i: (0, 0)
    tok = lambda i: (i, 0)
    vec = lambda a: a.reshape(1, -1)
    args = [jnp.asarray(feats), jnp.asarray(decay), w1, vec(b1), w2, vec(b2), w3, vec(freq)]
    return pl.pallas_call(
        _filt_kernel,
        grid=(seq // rows,),
        in_specs=[pl.BlockSpec((rows, 40), tok), pl.BlockSpec((rows, W_HALF), tok)]
                 + [pl.BlockSpec(a.shape, const2) for a in args[2:]],
        out_specs=[pl.BlockSpec((rows, 4 * W_HALF), tok), pl.BlockSpec((1, 4 * W_HALF), const2)],
        out_shape=[jax.ShapeDtypeStruct((seq, 4 * W_HALF), F32),
                   jax.ShapeDtypeStruct((1, 4 * W_HALF), F32)],
        compiler_params=_cparams(("arbitrary",)),
        name="hyena_filters",
    )(*args)


def _circular(filt, seq):
    f = filt.reshape(seq, 2, 2, W_HALF)
    fwd = jnp.moveaxis(f[:, :, 0], 1, 0)
    bwd = jnp.moveaxis(f[:, :, 1], 1, 0)
    return jnp.concatenate([fwd, jnp.zeros((2, 1, W_HALF), F32), bwd[:, :0:-1]], 1)


def _dft_consts_two_stage(seq):
    n = 2 * seq
    p = DFT_P
    q = n // p
    a = np.arange(q)[None, :]
    k1 = np.arange(q)[:, None]
    ang1 = 2.0 * np.pi * a * k1 / q
    f1 = np.empty((2 * q, q))
    f1[0::2] = np.cos(ang1)
    f1[1::2] = -np.sin(ang1)
    f3 = np.empty((q, 2 * q))
    f3[:, 0::2] = np.cos(ang1).T / n
    f3[:, 1::2] = -np.sin(ang1).T / n
    qq = np.arange(p)[None, None, :]
    k2 = np.arange(p)[None, :, None]
    kk1 = np.arange(q)[:, None, None]
    ang = 2.0 * np.pi * (qq * k2 / p + qq * kk1 / n)
    gr, gi = np.cos(ang), -np.sin(ang)
    g = np.concatenate([np.concatenate([gr, -gi], 2), np.concatenate([gi, gr], 2)], 1)
    grt, git = np.swapaxes(gr, 1, 2), np.swapaxes(gi, 1, 2)
    ginv = np.concatenate([np.concatenate([grt, git], 2), np.concatenate([-git, grt], 2)], 1)
    half = q // 2
    return (f1.astype(np.float32), f1[:, :half].astype(np.float32), f3[:half].astype(np.float32),
            g.astype(np.float32), ginv.astype(np.float32))


def _dft_consts_one_stage(seq):
    n = 2 * seq
    t = np.arange(n)[None, :]
    f = np.arange(n)[:, None]
    ang = 2.0 * np.pi * t * f / n
    fwd = np.concatenate([np.cos(ang), -np.sin(ang)], 0)
    inv = np.concatenate([np.cos(ang), -np.sin(ang)], 1)[:seq] / n
    return fwd.astype(np.float32), fwd[:, :seq].astype(np.float32), inv.astype(np.float32)


def _stage_kernel(f_ref, x_ref, o_ref):
    o_ref[0] = jnp.dot(f_ref[...], x_ref[0], precision=HIGHEST, preferred_element_type=F32)


def _stage(f, x):
    bsz, k, mm = x.shape
    tn = min(mm, 8192)
    return pl.pallas_call(
        _stage_kernel,
        grid=(bsz, mm // tn),
        in_specs=[pl.BlockSpec(f.shape, lambda b, j: (0, 0)),
                  pl.BlockSpec((1, k, tn), lambda b, j: (b, 0, j))],
        out_specs=pl.BlockSpec((1, f.shape[0], tn), lambda b, j: (b, 0, j)),
        out_shape=jax.ShapeDtypeStruct((bsz, f.shape[0], mm), F32),
        compiler_params=_cparams(("parallel", "parallel")),
        name="dft_stage",
    )(f, x)


def _spectrum_kernel(a_ref, g_ref, ssq_ref, o_ref):
    x = jnp.dot(g_ref[0], a_ref[0, 0], precision=HIGHEST, preferred_element_type=F32)
    o_ref[0, 0] = x * lax.rsqrt(ssq_ref[0, 0:1] + ssq_ref[0, 1:2] + 1e-6)


def _spectrum(a, g, ssq):
    nb, q, k, ch = a.shape
    rr = g.shape[1]
    return pl.pallas_call(
        _spectrum_kernel,
        grid=(nb, q),
        in_specs=[pl.BlockSpec((1, 1, k, ch), lambda b, j: (b, j, 0, 0)),
                  pl.BlockSpec((1, rr, k), lambda b, j: (j, 0, 0)),
                  pl.BlockSpec((1, 2, ch), lambda b, j: (b, 0, 0))],
        out_specs=pl.BlockSpec((1, 1, rr, ch), lambda b, j: (b, j, 0, 0)),
        out_shape=jax.ShapeDtypeStruct((nb, q, rr, ch), F32),
        compiler_params=_cparams(("parallel", "parallel")),
        name="hyena_spectrum",
    )(a, g, ssq)


def _spectral_product(x, h_ref):
    half = x.shape[0] // 2
    xr, xi = x[:half], x[half:]
    hr, hi = h_ref[0, 0, :half], h_ref[0, 0, half:]
    return jnp.concatenate([xr * hr - xi * hi, xr * hi + xi * hr], 0)


def _mid_kernel(a_ref, g_ref, h_ref, gi_ref, o_ref):
    x = jnp.dot(g_ref[0], a_ref[0, 0], precision=HIGHEST, preferred_element_type=F32)
    y = _spectral_product(x, h_ref)
    o_ref[0, 0] = jnp.dot(gi_ref[0], y, precision=HIGHEST, preferred_element_type=F32)


def _mid_gate_kernel(a_ref, g_ref, h_ref, gi_ref, hx_ref, bias_ref, o_ref):
    a = a_ref[0, 0]
    x = jnp.dot(g_ref[0], a, precision=HIGHEST, preferred_element_type=F32)
    y = _spectral_product(x, h_ref)
    conv = jnp.dot(gi_ref[0], y, precision=HIGHEST, preferred_element_type=F32)
    o_ref[0, 0] = hx_ref[0, 0] * (conv + bias_ref[...] * a)


def _mid(a, g, h, gi, order, gate=None):
    bsz, q, k, ch = a.shape
    rr = g.shape[1]
    ro = gi.shape[1]
    in_specs = [pl.BlockSpec((1, 1, k, ch), lambda b, j: (b, j, 0, 0)),
                pl.BlockSpec((1, rr, k), lambda b, j: (j, 0, 0)),
                pl.BlockSpec((1, 1, rr, ch), lambda b, j: (order, j, 0, 0)),
                pl.BlockSpec((1, ro, rr), lambda b, j: (j, 0, 0))]
    args = [a, g, h, gi]
    kern = _mid_kernel
    if gate is not None:
        in_specs += [pl.BlockSpec((1, 1, ro, ch), lambda b, j: (b, j, 0, 0)),
                     pl.BlockSpec((1, ch), lambda b, j: (0, 0))]
        args += list(gate)
        kern = _mid_gate_kernel
    return pl.pallas_call(
        kern,
        grid=(bsz, q),
        in_specs=in_specs,
        out_specs=pl.BlockSpec((1, 1, ro, ch), lambda b, j: (b, j, 0, 0)),
        out_shape=jax.ShapeDtypeStruct((bsz, q, ro, ch), F32),
        compiler_params=_cparams(("parallel", "parallel")),
        name="hyena_mid",
    )(*args)


def _last_stage_kernel(f3_ref, bt_ref, hx_ref, z_ref, bias_ref, o_ref):
    conv = jnp.dot(f3_ref[...], bt_ref[0], precision=HIGHEST, preferred_element_type=F32)
    o_ref[0] = hx_ref[0] * (conv + bias_ref[...] * z_ref[0])


def _last_stage(f3, bt, hx, z, bias_tiled):
    bsz, k, mm = bt.shape
    rows = f3.shape[0]
    tn = bias_tiled.shape[1]
    act = pl.BlockSpec((1, rows, tn), lambda b, j: (b, 0, j))
    return pl.pallas_call(
        _last_stage_kernel,
        grid=(bsz, mm // tn),
        in_specs=[pl.BlockSpec(f3.shape, lambda b, j: (0, 0)),
                  pl.BlockSpec((1, k, tn), lambda b, j: (b, 0, j)),
                  act, act,
                  pl.BlockSpec((1, tn), lambda b, j: (0, 0))],
        out_specs=act,
        out_shape=jax.ShapeDtypeStruct((bsz, rows, mm), F32),
        compiler_params=_cparams(("parallel", "parallel")),
        name="dft_last_stage",
    )(f3, bt, hx, z, bias_tiled)


def _hyena(hv, hx1, hx2, filt, ssq, bias, bsz, seq):
    ch = W_HALF
    circ = _circular(filt, seq)
    ssq = ssq.reshape(2, 2, ch)
    if seq <= 512:
        fwd_full, fwd_half, inv = (jnp.asarray(m) for m in _dft_consts_one_stage(seq))
        spec = _spectrum(circ[:, None], fwd_full[None], ssq)
        shape4 = (bsz, 1, seq, ch)
        z = _mid(hv.reshape(shape4), fwd_half[None], spec, inv[None], 0,
                 gate=(hx1.reshape(shape4), bias[0:1]))
        y = _mid(z, fwd_half[None], spec, inv[None], 1, gate=(hx2.reshape(shape4), bias[1:2]))
        return y.reshape(bsz * seq, ch)
    f1_full, f1_half, f3, g, ginv = (jnp.asarray(m) for m in _dft_consts_two_stage(seq))
    p = DFT_P
    q = 2 * seq // p
    spec = _spectrum(_stage(f1_full, circ.reshape(2, q, p * ch)).reshape(2, q, 2 * p, ch), g, ssq)
    flat = (bsz, q // 2, p * ch)
    tn = 8192
    out = hv.reshape(flat)
    for order, hx in enumerate((hx1, hx2)):
        a = _stage(f1_half, out).reshape(bsz, q, 2 * p, ch)
        bt = _mid(a, g, spec, ginv, order).reshape(bsz, 2 * q, p * ch)
        out = _last_stage(f3, bt, hx.reshape(flat), out, jnp.tile(bias[order:order + 1], (1, tn // ch)))
    return out.reshape(bsz * seq, ch)


def _block_diag(w):
    hh, blk, _ = w.shape
    eye = jnp.eye(hh, dtype=w.dtype)
    return jnp.einsum('hij,hg->higj', w, eye).reshape(hh * blk, hh * blk)


def _group_ones(width):
    idx = np.arange(width) // RW_HEAD
    return jnp.asarray((idx[:, None] == idx[None, :]).astype(np.float32), dtype=BF16)


def _even_layer(x, mods, h0, bsz, seq, line, pw):
    ya, gg, a, u = _ab_in(x, mods, pw['w_in'], pw['sc_conv'], pw['lru_conv'], pw['lru_conv_b'],
                          pw['wg'], pw['bg'], pw['nsp'], line)
    hf, hb, fin = _lru_scan(a, u, h0, bsz, seq)
    x = _ab_out(x, mods, ya, gg, hf.reshape(-1, W_HALF), hb.reshape(-1, W_HALF),
                pw['w_out'], pw['ln1_g'], pw['ln1_b'])
    return x, fin


def _odd_layer(x, mods, s0, bsz, seq, line, pw):
    ones = _group_ones(W_HALF)
    (hv, hx1, hx2, r, v, kk, w, kd, b, bonus, sg) = _cd_in(
        x, mods, pw['w_in'], pw['hy_conv'], pw['rw_mu'], pw['rw_mu_x'], pw['l1'], pw['l2'], pw['l0'],
        pw['rw_kk'], pw['rw_ka'], pw['rw_rk'], ones, line)
    filt, ssq = _hyena_filters(seq, pw['hy_w1'], pw['hy_b1'], pw['hy_w2'], pw['hy_b2'], pw['hy_w3'],
                               pw['hy_freq'])
    yc = _hyena(hv, hx1, hx2, filt, ssq, pw['hy_bias'], bsz, seq)

    s0 = jnp.transpose(s0, (0, 1, 3, 2, 4)).reshape(bsz, 2, RW_HEAD, W_HALF)
    lane = np.arange(W_HALF)
    eye = jnp.asarray(((lane[None, :] % RW_HEAD) == np.arange(RW_HEAD)[:, None]).astype(np.float32))
    of, ob, fin = _rwkv_scan(r, v, kk, w, kd, b, s0, _group_ones(128), eye, bsz, seq)

    def rows(o):
        o = o.reshape(bsz, seq // RW_HEAD, RW_HEAD, RW_HEADS, RW_HEAD)
        return jnp.transpose(o, (0, 1, 4, 3, 2)).reshape(bsz * seq, W_HALF)

    x = _cd_out(x, mods, yc, rows(of), rows(ob), bonus, sg, pw['rw_gn_g'], pw['rw_gn_b'], ones,
                pw['w_out'], pw['ln1_g'], pw['ln1_b'])
    fin = jnp.transpose(fin.reshape(bsz, 2, RW_HEAD, RW_HEADS, RW_HEAD), (0, 1, 3, 2, 4))
    return x, fin


def _layer_weights(p, l):
    j = l // 2
    row = lambda a: a.reshape(1, -1)
    pw = {'w_out': p['w_out'][l].astype(BF16), 'ln1_g': row(p['ln1_g'][l]), 'ln1_b': row(p['ln1_b'][l]),
          'ln2_g': row(p['ln2_g'][l]), 'ln2_b': row(p['ln2_b'][l]),
          'mlp_w1': p['mlp_w1'][l].astype(BF16), 'mlp_w2': p['mlp_w2'][l].astype(BF16)}
    if l % 2 == 0:
        gates = [_block_diag(p[name][j, d]) for d in range(2) for name in ('lru_wa', 'lru_wi')]
        biases = [p[name][j, d] for d in range(2) for name in ('lru_ba', 'lru_bi')]
        pw.update({'w_in': p['ab_w_in'][j].astype(BF16), 'sc_conv': p['sc_conv'][j],
                   'lru_conv': p['lru_conv'][j], 'lru_conv_b': row(p['lru_conv_b'][j]),
                   'wg': jnp.concatenate(gates, 1).astype(BF16), 'bg': row(jnp.concatenate(biases)),
                   'nsp': jax.nn.softplus(-p['lru_lambda'][j])})
    else:
        zeros = jnp.zeros((64, W_HALF), F32)
        l2 = jnp.concatenate([
            jnp.concatenate([p['rw_w2'][j, 0], zeros], 1), jnp.concatenate([zeros, p['rw_w2'][j, 1]], 1),
            jnp.concatenate([p['rw_a2'][j, 0], zeros], 1), jnp.concatenate([zeros, p['rw_a2'][j, 1]], 1)], 0)
        pw.update({'w_in': p['cd_w_in'][j].astype(BF16), 'hy_conv': p['hy_conv'][j],
                   'rw_mu': p['rw_mu'][j], 'rw_mu_x': p['rw_mu_x'][j],
                   'l1': jnp.concatenate([p['rw_w1'][j, 0], p['rw_w1'][j, 1],
                                          p['rw_a1'][j, 0], p['rw_a1'][j, 1]], 1).astype(BF16),
                   'l2': l2.astype(BF16),
                   'l0': jnp.stack([p['rw_w0'][j].reshape(-1), p['rw_a0'][j].reshape(-1)]),
                   'rw_kk': row(p['rw_kk'][j]), 'rw_ka': row(p['rw_ka'][j]), 'rw_rk': row(p['rw_rk'][j]),
                   'rw_gn_g': row(p['rw_gn_g'][j]), 'rw_gn_b': row(p['rw_gn_b'][j]),
                   'hy_w1': p['hy_w1'][j], 'hy_b1': p['hy_b1'][j], 'hy_w2': p['hy_w2'][j],
                   'hy_b2': p['hy_b2'][j], 'hy_w3': p['hy_w3'][j], 'hy_freq': p['hy_freq'][j],
                   'hy_bias': p['hy_bias'][j]})
    return pw


def _to_colmajor(x, bsz, rows):
    return jnp.transpose(x.reshape(bsz, rows, GRID_W, D_MODEL), (0, 2, 1, 3)).reshape(-1, D_MODEL)


def _from_colmajor(x, bsz, rows):
    return jnp.transpose(x.reshape(bsz, GRID_W, rows, D_MODEL), (0, 2, 1, 3)).reshape(-1, D_MODEL)


def _trunk(x, mods, init_lru, init_rwkv, rows, weights):
    bsz, seq, _ = x.shape
    x = x.reshape(bsz * seq, D_MODEL)
    new_lru, new_rwkv = [], []
    for l in range(DEPTH):
        j = l // 2
        pw = weights[l]
        if l % 2 == 0:
            line = seq if rows is None else GRID_W
            x, st = _even_layer(x, mods[l], init_lru[:, j], bsz, seq, line, pw)
            new_lru.append(st)
        else:
            if rows is None:
                x, st = _odd_layer(x, mods[l], init_rwkv[:, j], bsz, seq, seq, pw)
                x = _mlp(x, mods[l], pw['mlp_w1'], pw['mlp_w2'], pw['ln2_g'], pw['ln2_b'])
            else:
                x = _to_colmajor(x, bsz, rows)
                x, st = _odd_layer(x, mods[l], init_rwkv[:, j], bsz, seq, rows, pw)
                x = _mlp(x, mods[l], pw['mlp_w1'], pw['mlp_w2'], pw['ln2_g'], pw['ln2_b'])
                x = _from_colmajor(x, bsz, rows)
            new_rwkv.append(st)
            continue
        x = _mlp(x, mods[l], pw['mlp_w1'], pw['mlp_w2'], pw['ln2_g'], pw['ln2_b'])
    return x.reshape(bsz, seq, D_MODEL), jnp.stack(new_lru, 1), jnp.stack(new_rwkv, 1)


def kernel(x_prompt, x_sample, state_lru, state_rwkv, c, c_ctx, w_mod, b_mod, ln1_g, ln1_b, ln2_g, ln2_b, mlp_w1, mlp_w2, w_out, ab_w_in, sc_conv, lru_conv, lru_conv_b, lru_wa, lru_ba, lru_wi, lru_bi, lru_lambda, cd_w_in, hy_conv, hy_w1, hy_b1, hy_w2, hy_b2, hy_w3, hy_freq, hy_bias, rw_mu, rw_mu_x, rw_w0, rw_w1, rw_w2, rw_a0, rw_a1, rw_a2, rw_kk, rw_ka, rw_rk, rw_gn_g, rw_gn_b):
    p = dict(ln1_g=ln1_g, ln1_b=ln1_b, ln2_g=ln2_g, ln2_b=ln2_b,
             mlp_w1=mlp_w1, mlp_w2=mlp_w2, w_out=w_out, ab_w_in=ab_w_in, sc_conv=sc_conv,
             lru_conv=lru_conv, lru_conv_b=lru_conv_b, lru_wa=lru_wa, lru_ba=lru_ba, lru_wi=lru_wi,
             lru_bi=lru_bi, lru_lambda=lru_lambda, cd_w_in=cd_w_in, hy_conv=hy_conv, hy_w1=hy_w1,
             hy_b1=hy_b1, hy_w2=hy_w2, hy_b2=hy_b2, hy_w3=hy_w3, hy_freq=hy_freq, hy_bias=hy_bias,
             rw_mu=rw_mu, rw_mu_x=rw_mu_x, rw_w0=rw_w0, rw_w1=rw_w1, rw_w2=rw_w2, rw_a0=rw_a0,
             rw_a1=rw_a1, rw_a2=rw_a2, rw_kk=rw_kk, rw_ka=rw_ka, rw_rk=rw_rk, rw_gn_g=rw_gn_g,
             rw_gn_b=rw_gn_b)
    weights = [_layer_weights(p, l) for l in range(DEPTH)]
    nb, dec = x_prompt.shape[0], x_sample.shape[0]
    rpad = -(1 + dec) % 8
    cvec = jnp.concatenate([c_ctx[None, :], c, jnp.zeros((rpad, D_MODEL), F32)], 0)
    mods = _mods(cvec, w_mod, b_mod)
    mods = jnp.transpose(mods, (0, 2, 1, 3))
    zero_lru = jnp.zeros((nb, (DEPTH + 1) // 2, 2, W_HALF), F32)
    zero_rwkv = jnp.zeros((nb, DEPTH // 2, 2, RW_HEADS, RW_HEAD, RW_HEAD), F32)
    y_prompt, new_lru, new_rwkv = _trunk(x_prompt, mods[:, 0:1], zero_lru, zero_rwkv, None, weights)
    rows = x_sample.shape[1] // GRID_W
    y_sample, _, _ = _trunk(x_sample, mods[:, 1:1 + dec], state_lru, state_rwkv, rows, weights)
    return (y_prompt, y_sample, new_lru, new_rwkv)
```

```python
import functools
import math

import jax
import jax.numpy as jnp
import numpy as np
from jax import lax
from jax.experimental import pallas as pl
from jax.experimental.pallas import tpu as pltpu

F32 = jnp.float32
BF16 = jnp.bfloat16
HIGHEST = lax.Precision.HIGHEST

D_MODEL = 1024
DEPTH = 4
GRID_W = 64
W_HALF = D_MODEL // 2
LRU_HEADS = 8
RG_C = 8.0
HY_BANDS = 16
HY_TARGET = 1e-2
HY_SHORT_PCT = 0.3
HY_LONG_PCT = 1.5
RW_HEAD = 64
RW_HEADS = W_HALF // RW_HEAD
D_FF = 4 * D_MODEL
DN_ALPHA = (2 * DEPTH) ** 0.25
LN_EPS = 1e-5
GN_EPS = 64e-5

TM = 256
VMEM_LIMIT = 56 * 1024 * 1024
DFT_P = 128
DFT_ROWS = 8
FF_CHUNK = 1024
SCAN_LANES = 256
SCAN_BATCH = 8


def _cparams(sem):
    return pltpu.CompilerParams(dimension_semantics=sem, vmem_limit_bytes=VMEM_LIMIT)


def _shift_rows(x, off, line):
    if off == 0:
        return x
    n = x.shape[0]
    rolled = pltpu.roll(x, (-off) % n, 0)
    pos = lax.broadcasted_iota(jnp.int32, x.shape, 0) & (line - 1)
    valid = (pos + off >= 0) if off < 0 else (pos + off < line)
    return jnp.where(valid, rolled, 0.0)


def _dwconv_rows(x, w, pad_left, line):
    out = None
    for k in range(w.shape[0]):
        term = _shift_rows(x, k - pad_left, line) * w[k:k + 1]
        out = term if out is None else out + term
    return out


def _tshift_rows(x, line):
    return 0.5 * (_shift_rows(x, -1, line) + _shift_rows(x, 1, line))


def _layer_norm(v, g, b):
    mu = jnp.mean(v, -1, keepdims=True)
    vc = v - mu
    var = jnp.mean(vc * vc, -1, keepdims=True)
    return vc * lax.rsqrt(var + LN_EPS) * g + b


def _dot_bf16(a, b):
    return jnp.dot(a.astype(BF16), b, preferred_element_type=F32)


def _dot_f32_lhs(a, b):
    return lax.dot_general(a, b, (((1,), (0,)), ((), ())), preferred_element_type=F32)


def _group_sum(x, ones):
    hi = x.astype(BF16)
    lo = (x - hi.astype(F32)).astype(BF16)
    return (jnp.dot(hi, ones, preferred_element_type=F32)
            + jnp.dot(lo, ones, preferred_element_type=F32))


def _softplus(x):
    return jnp.maximum(x, 0.0) + jnp.log(1.0 + jnp.exp(-jnp.abs(x)))


def _sigmoid(x):
    return 1.0 / (1.0 + jnp.exp(-x))


def _mods_kernel(c_ref, w_ref, b_ref, o_ref):
    c = c_ref[...]
    s = c * _sigmoid(c)
    o_ref[0, 0] = jnp.dot(s, w_ref[0], precision=HIGHEST, preferred_element_type=F32) + b_ref[0, 0]


def _mods(cvec, w_mod, b_mod):
    r = cvec.shape[0]
    return pl.pallas_call(
        _mods_kernel,
        grid=(DEPTH, 6),
        in_specs=[pl.BlockSpec((r, D_MODEL), lambda l, n: (0, 0)),
                  pl.BlockSpec((1, D_MODEL, D_MODEL), lambda l, n: (l, 0, n)),
                  pl.BlockSpec((1, 1, 1, D_MODEL), lambda l, n: (l, n, 0, 0))],
        out_specs=pl.BlockSpec((1, 1, r, D_MODEL), lambda l, n: (l, n, 0, 0)),
        out_shape=jax.ShapeDtypeStruct((DEPTH, 6, r, D_MODEL), F32),
        compiler_params=_cparams(("parallel", "parallel")),
        name="mods",
    )(cvec, w_mod, b_mod.reshape(DEPTH, 6, 1, D_MODEL))


def _ab_in_kernel(x_ref, mod_ref, w_ref, scw_ref, lcw_ref, lcb_ref, wg_ref, bg_ref, nsp_ref,
                  ya_ref, gg_ref, a_ref, u_ref, *, line):
    x = x_ref[...]
    m = mod_ref[0]
    h = x * (1.0 + m[1:2]) + m[0:1]
    proj = _dot_bf16(h, w_ref[...])
    w = W_HALF
    s_b, s_c, s_v = proj[:, 0:w], proj[:, w:2 * w], proj[:, 2 * w:3 * w]
    g_lru, x_lru = proj[:, 3 * w:4 * w], proj[:, 4 * w:5 * w]
    ya_ref[...] = s_b * _dwconv_rows(s_c * s_v, scw_ref[...], 1, line)
    gg_ref[...] = 0.5 * g_lru * (1.0 + jnp.tanh(
        math.sqrt(2.0 / math.pi) * (g_lru + 0.044715 * (g_lru * g_lru * g_lru))))
    xc = _dwconv_rows(x_lru, lcw_ref[...], 2, line) + lcb_ref[...]
    gates = _dot_bf16(xc, wg_ref[...]) + bg_ref[...]
    for d in range(2):
        r = _sigmoid(gates[:, (2 * d) * w:(2 * d + 1) * w])
        i = _sigmoid(gates[:, (2 * d + 1) * w:(2 * d + 2) * w])
        a = jnp.exp(-RG_C * r * nsp_ref[d:d + 1])
        a_ref[d] = a
        u_ref[d] = jnp.sqrt(1.0 - a * a) * (i * xc)


def _ab_in(x, mods, w_in, scw, lcw, lcb, wg, bg, nsp, line):
    n = x.shape[0]
    tpb = n // mods.shape[0] // TM
    tok = lambda i: (i, 0)
    const2 = lambda i: (0, 0)
    half = jax.ShapeDtypeStruct((n, W_HALF), F32)
    both = jax.ShapeDtypeStruct((2, n, W_HALF), F32)
    return pl.pallas_call(
        functools.partial(_ab_in_kernel, line=line),
        grid=(n // TM,),
        in_specs=[pl.BlockSpec((TM, D_MODEL), tok),
                  pl.BlockSpec((1, 6, D_MODEL), lambda i: (i // tpb, 0, 0)),
                  pl.BlockSpec(w_in.shape, const2),
                  pl.BlockSpec(scw.shape, const2),
                  pl.BlockSpec(lcw.shape, const2),
                  pl.BlockSpec(lcb.shape, const2),
                  pl.BlockSpec(wg.shape, const2),
                  pl.BlockSpec(bg.shape, const2),
                  pl.BlockSpec(nsp.shape, const2)],
        out_specs=[pl.BlockSpec((TM, W_HALF), tok), pl.BlockSpec((TM, W_HALF), tok),
                   pl.BlockSpec((2, TM, W_HALF), lambda i: (0, i, 0)),
                   pl.BlockSpec((2, TM, W_HALF), lambda i: (0, i, 0))],
        out_shape=[half, half, both, both],
        compiler_params=_cparams(("parallel",)),
        name="ab_in",
    )(x, mods, w_in, scw, lcw, lcb, wg, bg, nsp)


def _lru_scan_kernel(af_ref, uf_ref, ab_ref, ub_ref, h0_ref, hf_ref, hb_ref, fin_ref, carry, *, tc, nc):
    c = pl.program_id(1)

    @pl.when(c == 0)
    def _():
        carry[...] = h0_ref[0]

    def step(s, hs):
        hf, hb = hs
        tb = tc - 1 - s
        hf = af_ref[0, 0, pl.ds(s, 1), :] * hf + uf_ref[0, 0, pl.ds(s, 1), :]
        hb = ab_ref[0, 0, pl.ds(tb, 1), :] * hb + ub_ref[0, 0, pl.ds(tb, 1), :]
        hf_ref[0, pl.ds(s, 1), :] = hf
        hb_ref[0, pl.ds(tb, 1), :] = hb
        return hf, hb

    hf, hb = lax.fori_loop(0, tc, step, (carry[0:1], carry[1:2]), unroll=8)
    carry[0:1] = hf
    carry[1:2] = hb

    @pl.when(c == nc - 1)
    def _():
        fin_ref[0, 0:1] = hf
        fin_ref[0, 1:2] = hb


def _lru_scan(a, u, h0, bsz, seq):
    tc = min(seq, 512)
    nc = seq // tc
    a = a.reshape(2, bsz, seq, W_HALF)
    u = u.reshape(2, bsz, seq, W_HALF)
    fwd = lambda b, c: (0, b, c, 0)
    bwd = lambda b, c: (1, b, nc - 1 - c, 0)
    blk = (1, 1, tc, W_HALF)
    seq_shape = jax.ShapeDtypeStruct((bsz, seq, W_HALF), F32)
    return pl.pallas_call(
        functools.partial(_lru_scan_kernel, tc=tc, nc=nc),
        grid=(bsz, nc),
        in_specs=[pl.BlockSpec(blk, fwd), pl.BlockSpec(blk, fwd),
                  pl.BlockSpec(blk, bwd), pl.BlockSpec(blk, bwd),
                  pl.BlockSpec((1, 2, W_HALF), lambda b, c: (b, 0, 0))],
        out_specs=[pl.BlockSpec((1, tc, W_HALF), lambda b, c: (b, c, 0)),
                   pl.BlockSpec((1, tc, W_HALF), lambda b, c: (b, nc - 1 - c, 0)),
                   pl.BlockSpec((1, 2, W_HALF), lambda b, c: (b, 0, 0))],
        out_shape=[seq_shape, seq_shape, jax.ShapeDtypeStruct((bsz, 2, W_HALF), F32)],
        scratch_shapes=[pltpu.VMEM((2, W_HALF), F32)],
        compiler_params=_cparams(("parallel", "arbitrary")),
        name="lru_scan",
    )(a, u, a, u, h0)


def _ab_out_kernel(x_ref, mod_ref, ya_ref, gg_ref, hf_ref, hb_ref, w_ref, g_ref, b_ref, o_ref):
    x = x_ref[...]
    m = mod_ref[0]
    yb = gg_ref[...] * (hf_ref[...] + hb_ref[...])
    y = _dot_bf16(ya_ref[...], w_ref[0:W_HALF, :]) + _dot_bf16(yb, w_ref[W_HALF:, :])
    o_ref[...] = _layer_norm(DN_ALPHA * x + m[2:3] * y, g_ref[...], b_ref[...])


def _ab_out(x, mods, ya, gg, hf, hb, w_out, ln_g, ln_b):
    n = x.shape[0]
    tpb = n // mods.shape[0] // TM
    tok = lambda i: (i, 0)
    const2 = lambda i: (0, 0)
    half = pl.BlockSpec((TM, W_HALF), tok)
    return pl.pallas_call(
        _ab_out_kernel,
        grid=(n // TM,),
        in_specs=[pl.BlockSpec((TM, D_MODEL), tok),
                  pl.BlockSpec((1, 6, D_MODEL), lambda i: (i // tpb, 0, 0)),
                  half, half, half, half,
                  pl.BlockSpec(w_out.shape, const2),
                  pl.BlockSpec((1, D_MODEL), const2), pl.BlockSpec((1, D_MODEL), const2)],
        out_specs=pl.BlockSpec((TM, D_MODEL), tok),
        out_shape=jax.ShapeDtypeStruct((n, D_MODEL), F32),
        compiler_params=_cparams(("parallel",)),
        name="ab_out",
    )(x, mods, ya, gg, hf, hb, w_out, ln_g, ln_b)


def _mlp_kernel(x_ref, mod_ref, w1_ref, w2_ref, g_ref, b_ref, o_ref):
    x = x_ref[...]
    m = mod_ref[0]
    h = (x * (1.0 + m[4:5]) + m[3:4]).astype(BF16)
    y = None
    for c in range(D_FF // FF_CHUNK):
        t = jnp.dot(h, w1_ref[:, c * FF_CHUNK:(c + 1) * FF_CHUNK], preferred_element_type=F32)
        t = jnp.maximum(t, 0.0)
        part = _dot_bf16(t * t, w2_ref[c * FF_CHUNK:(c + 1) * FF_CHUNK, :])
        y = part if y is None else y + part
    o_ref[...] = _layer_norm(DN_ALPHA * x + m[5:6] * y, g_ref[...], b_ref[...])


def _mlp(x, mods, w1, w2, ln_g, ln_b):
    n = x.shape[0]
    tpb = n // mods.shape[0] // TM
    tok = lambda i: (i, 0)
    const2 = lambda i: (0, 0)
    return pl.pallas_call(
        _mlp_kernel,
        grid=(n // TM,),
        in_specs=[pl.BlockSpec((TM, D_MODEL), tok),
                  pl.BlockSpec((1, 6, D_MODEL), lambda i: (i // tpb, 0, 0)),
                  pl.BlockSpec(w1.shape, const2, pipeline_mode=pl.Buffered(1)),
                  pl.BlockSpec(w2.shape, const2, pipeline_mode=pl.Buffered(1)),
                  pl.BlockSpec((1, D_MODEL), const2), pl.BlockSpec((1, D_MODEL), const2)],
        out_specs=pl.BlockSpec((TM, D_MODEL), tok),
        out_shape=jax.ShapeDtypeStruct((n, D_MODEL), F32),
        compiler_params=_cparams(("parallel",)),
        name="mlp",
    )(x, mods, w1, w2, ln_g, ln_b)


def _cd_in_kernel(x_ref, mod_ref, w_ref, hyc_ref, mu_ref, mux_ref, l1_ref, l2_ref, l0_ref,
                  kkp_ref, ka_ref, rk_ref, ones_ref,
                  hv_ref, hx1_ref, hx2_ref, r_ref, v_ref, kk_ref, w_out_ref, kd_ref, b_ref,
                  bonus_ref, sg_ref, *, line):
    x = x_ref[...]
    m = mod_ref[0]
    h = x * (1.0 + m[1:2]) + m[0:1]
    proj = _dot_bf16(h, w_ref[...])
    w = W_HALF
    u = _dwconv_rows(proj[:, 0:3 * w], hyc_ref[...], 1, line)
    hv_ref[...] = u[:, 0:w]
    hx1_ref[...] = u[:, w:2 * w]
    hx2_ref[...] = u[:, 2 * w:3 * w]

    mixed = []
    for n in range(4):
        t = proj[:, (3 + n) * w:(4 + n) * w]
        mixed.append(t + (_tshift_rows(t, line) - t) * mu_ref[n:n + 1])
    r, k, v, g = mixed
    dh = _tshift_rows(h, line) - h
    xw = h + dh * mux_ref[0:1]
    xa = h + dh * mux_ref[1:2]
    tw = jnp.tanh(_dot_bf16(xw, l1_ref[:, 0:128]))
    ta = _dot_bf16(xa, l1_ref[:, 128:256])
    zw = _dot_bf16(tw, l2_ref[0:128, :]) + l0_ref[0:1]
    za = _dot_bf16(ta, l2_ref[128:256, :]) + l0_ref[1:2]

    ones = ones_ref[...]
    kk = k * kkp_ref[...]
    kk = kk * lax.rsqrt(_group_sum(kk * kk, ones) + 1e-12)
    r_ref[...] = r
    v_ref[...] = v
    kk_ref[...] = kk
    kd_sum = None
    for d in range(2):
        w_raw = -_softplus(-zw[:, d * w:(d + 1) * w]) - 0.5
        w_out_ref[d] = jnp.exp(-jnp.exp(w_raw))
        a = _sigmoid(za[:, d * w:(d + 1) * w])
        kd = k * (1.0 + (a - 1.0) * ka_ref[...])
        kd_ref[d] = kd
        b_ref[d] = kk * a
        kd_sum = kd if kd_sum is None else kd_sum + kd
    bonus_ref[...] = _group_sum(r * kd_sum * rk_ref[...], ones) * v
    sg_ref[...] = _sigmoid(g)


def _cd_in(x, mods, w_in, hyc, mu, mux, l1, l2, l0, kkp, ka, rk, ones, line):
    n = x.shape[0]
    tpb = n // mods.shape[0] // TM
    tok = lambda i: (i, 0)
    const2 = lambda i: (0, 0)
    half = jax.ShapeDtypeStruct((n, W_HALF), F32)
    both = jax.ShapeDtypeStruct((2, n, W_HALF), F32)
    half_spec = pl.BlockSpec((TM, W_HALF), tok)
    both_spec = pl.BlockSpec((2, TM, W_HALF), lambda i: (0, i, 0))
    consts = [w_in, hyc, mu, mux, l1, l2, l0, kkp, ka, rk, ones]
    return pl.pallas_call(
        functools.partial(_cd_in_kernel, line=line),
        grid=(n // TM,),
        in_specs=[pl.BlockSpec((TM, D_MODEL), tok),
                  pl.BlockSpec((1, 6, D_MODEL), lambda i: (i // tpb, 0, 0))]
                 + [pl.BlockSpec(c.shape, const2) for c in consts],
        out_specs=[half_spec] * 6 + [both_spec] * 3 + [half_spec] * 2,
        out_shape=[half] * 6 + [both] * 3 + [half] * 2,
        compiler_params=_cparams(("parallel",)),
        name="cd_in",
    )(x, mods, *consts)


def _rwkv_scan_kernel(rf_ref, vf_ref, kkf_ref, wf_ref, kdf_ref, bf_ref,
                      rb_ref, vb_ref, kkb_ref, wb_ref, kdb_ref, bb_ref,
                      s0_ref, ones_ref, eye_ref,
                      of_ref, ob_ref, fin_ref, s_scr, o_scr, *, tc, nc, gb):
    c = pl.program_id(1)
    nsub = tc // RW_HEAD
    grp = 8
    per_sub = RW_HEAD // grp

    @pl.when(c == 0)
    def _():
        s_scr[...] = s0_ref[...]
        o_scr[...] = jnp.zeros_like(o_scr)

    ones = ones_ref[...]
    lane_pos = lax.broadcasted_iota(jnp.int32, (RW_HEAD, SCAN_LANES), 1) & (RW_HEAD - 1)
    chains = []
    for g in range(gb):
        chains.append((g, 0, rf_ref, vf_ref, kkf_ref, wf_ref, kdf_ref, bf_ref))
        chains.append((g, 1, rb_ref, vb_ref, kkb_ref, wb_ref, kdb_ref, bb_ref))

    hd = RW_HEAD

    streams = [(ch, pl.ds(j * SCAN_LANES, SCAN_LANES)) for j in range(W_HALF // SCAN_LANES) for ch in chains]
    nst = len(streams)

    def group(i, carry):
        sub = i // per_sub
        base = (i % per_sub) * grp
        ops = []
        for ((g, d, r_ref, v_ref, kk_ref, w_ref, kd_ref, b_ref), ls) in streams:
            t0 = pl.multiple_of(i * grp if d == 0 else tc - grp - i * grp, grp)
            rows = pl.ds(t0, grp)
            ops.append((kk_ref[g, rows, ls], r_ref[g, rows, ls], v_ref[g, rows, ls],
                        w_ref[0, g, rows, ls], kd_ref[0, g, rows, ls], b_ref[0, g, rows, ls]))
        for s in range(grp):
            lhs = []
            for si, ((g, d, *_), ls) in enumerate(streams):
                q = s if d == 0 else grp - 1 - s
                lhs.append(s_scr[g, d, :, ls] * ops[si][0][q:q + 1])
            for si, ((g, d, *_), ls) in enumerate(streams):
                q = s if d == 0 else grp - 1 - s
                lhs.append(eye_ref[:, ls] * ops[si][2][q:q + 1])
            res = _dot_bf16(jnp.concatenate(lhs, 0), ones)
            lhs2 = []
            for si, ((g, d, *_), ls) in enumerate(streams):
                q = s if d == 0 else grp - 1 - s
                kap8, rr8, vv8, ww8, kd8, bb8 = ops[si]
                sa = res[si * hd:(si + 1) * hd]
                vcol = res[(nst + si) * hd:(nst + si + 1) * hd]
                st = s_scr[g, d, :, ls] * ww8[q:q + 1] - sa * bb8[q:q + 1] + vcol * kd8[q:q + 1]
                s_scr[g, d, :, ls] = st
                lhs2.append(st * rr8[q:q + 1])
            res2 = _dot_bf16(jnp.concatenate(lhs2, 0), ones)
            for si, ((g, d, *_), ls) in enumerate(streams):
                pos = base + s if d == 0 else RW_HEAD - 1 - (base + s)
                o_scr[g, d, :, ls] = jnp.where(lane_pos == pos, res2[si * hd:(si + 1) * hd],
                                               o_scr[g, d, :, ls])

        @pl.when(i % per_sub == per_sub - 1)
        def _():
            for g in range(gb):
                of_ref[g, sub] = o_scr[g, 0]
                ob_ref[g, nsub - 1 - sub] = o_scr[g, 1]

        return carry

    lax.fori_loop(0, tc // grp, group, 0)

    @pl.when(c == nc - 1)
    def _():
        fin_ref[...] = s_scr[...]


def _rwkv_scan(r, v, kk, w, kd, b, s0, ones2, eye, bsz, seq):
    tc = 64
    nc = seq // tc
    gb = min(bsz, SCAN_BATCH)
    r, v, kk = (t.reshape(bsz, seq, W_HALF) for t in (r, v, kk))
    w, kd, b = (t.reshape(2, bsz, seq, W_HALF) for t in (w, kd, b))
    shared = (gb, tc, W_HALF)
    perdir = (1, gb, tc, W_HALF)
    sf = lambda i, c: (i, c, 0)
    sb = lambda i, c: (i, nc - 1 - c, 0)
    pf = lambda i, c: (0, i, c, 0)
    pb = lambda i, c: (1, i, nc - 1 - c, 0)
    nsub = tc // RW_HEAD
    o_shape = jax.ShapeDtypeStruct((bsz, seq // RW_HEAD, RW_HEAD, W_HALF), F32)
    o_blk = (gb, nsub, RW_HEAD, W_HALF)
    st_blk = (gb, 2, RW_HEAD, W_HALF)
    return pl.pallas_call(
        functools.partial(_rwkv_scan_kernel, tc=tc, nc=nc, gb=gb),
        grid=(bsz // gb, nc),
        in_specs=[pl.BlockSpec(shared, sf)] * 3 + [pl.BlockSpec(perdir, pf)] * 3
                 + [pl.BlockSpec(shared, sb)] * 3 + [pl.BlockSpec(perdir, pb)] * 3
                 + [pl.BlockSpec(st_blk, lambda i, c: (i, 0, 0, 0)),
                    pl.BlockSpec(ones2.shape, lambda i, c: (0, 0)),
                    pl.BlockSpec(eye.shape, lambda i, c: (0, 0))],
        out_specs=[pl.BlockSpec(o_blk, lambda i, c: (i, c, 0, 0)),
                   pl.BlockSpec(o_blk, lambda i, c: (i, nc - 1 - c, 0, 0)),
                   pl.BlockSpec(st_blk, lambda i, c: (i, 0, 0, 0))],
        out_shape=[o_shape, o_shape, jax.ShapeDtypeStruct((bsz, 2, RW_HEAD, W_HALF), F32)],
        scratch_shapes=[pltpu.VMEM(st_blk, F32), pltpu.VMEM(st_blk, F32)],
        compiler_params=_cparams(("parallel", "arbitrary")),
        name="rwkv_scan",
    )(r, v, kk, w, kd, b, r, v, kk, w, kd, b, s0, ones2, eye)


def _cd_out_kernel(x_ref, mod_ref, yc_ref, of_ref, ob_ref, bonus_ref, sg_ref, gng_ref, gnb_ref,
                   ones_ref, w_ref, g_ref, b_ref, o_ref):
    x = x_ref[...]
    m = mod_ref[0]
    ones = ones_ref[...]
    o = of_ref[...] + ob_ref[...]
    oc = o - _group_sum(o, ones) * (1.0 / RW_HEAD)
    on = oc * lax.rsqrt(_group_sum(oc * oc, ones) * (1.0 / RW_HEAD) + GN_EPS)
    yd = (on * gng_ref[...] + gnb_ref[...] + bonus_ref[...]) * sg_ref[...]
    y = _dot_bf16(yc_ref[...], w_ref[0:W_HALF, :]) + _dot_bf16(yd, w_ref[W_HALF:, :])
    o_ref[...] = _layer_norm(DN_ALPHA * x + m[2:3] * y, g_ref[...], b_ref[...])


def _cd_out(x, mods, yc, of, ob, bonus, sg, gng, gnb, ones, w_out, ln_g, ln_b):
    n = x.shape[0]
    tpb = n // mods.shape[0] // TM
    tok = lambda i: (i, 0)
    const2 = lambda i: (0, 0)
    half = pl.BlockSpec((TM, W_HALF), tok)
    vec = pl.BlockSpec((1, W_HALF), const2)
    return pl.pallas_call(
        _cd_out_kernel,
        grid=(n // TM,),
        in_specs=[pl.BlockSpec((TM, D_MODEL), tok),
                  pl.BlockSpec((1, 6, D_MODEL), lambda i: (i // tpb, 0, 0)),
                  half, half, half, half, half, vec, vec,
                  pl.BlockSpec(ones.shape, const2),
                  pl.BlockSpec(w_out.shape, const2),
                  pl.BlockSpec((1, D_MODEL), const2), pl.BlockSpec((1, D_MODEL), const2)],
        out_specs=pl.BlockSpec((TM, D_MODEL), tok),
        out_shape=jax.ShapeDtypeStruct((n, D_MODEL), F32),
        compiler_params=_cparams(("parallel",)),
        name="cd_out",
    )(x, mods, yc, of, ob, bonus, sg, gng, gnb, ones, w_out, ln_g, ln_b)


def _hyena_consts(seq):
    tn = np.linspace(0.0, 1.0, seq, dtype=np.float32)
    tr = np.arange(seq, dtype=np.float32)
    bands = np.linspace(1e-4, HY_BANDS - 1, HY_BANDS, dtype=np.float32)
    ang = np.float32(2.0 * math.pi / seq) * tr[:, None] * bands[None, :]
    feats = np.concatenate([tn[:, None], np.cos(ang), -np.sin(ang)], -1).astype(np.float32)
    feats = np.pad(feats, ((0, 0), (0, 40 - feats.shape[1])))
    deltas = np.abs(np.linspace(math.log(HY_TARGET) / HY_LONG_PCT, math.log(HY_TARGET) / HY_SHORT_PCT,
                                W_HALF, dtype=np.float32))
    decay = np.exp(-tn[:, None] * deltas[None, :]).astype(np.float32)
    return feats, decay


def _filt_kernel(f_ref, dec_ref, w1_ref, b1_ref, w2_ref, b2_ref, w3_ref, fr_ref, o_ref, ssq_ref):
    i = pl.program_id(0)
    fr = fr_ref[...]
    hid = jnp.sin(fr * (jnp.dot(f_ref[...], w1_ref[...], precision=HIGHEST,
                                preferred_element_type=F32) + b1_ref[...]))
    hid = jnp.sin(fr * (jnp.dot(hid, w2_ref[...], precision=HIGHEST,
                                preferred_element_type=F32) + b2_ref[...]))
    raw = jnp.dot(hid, w3_ref[...], precision=HIGHEST, preferred_element_type=F32)
    dec = dec_ref[...]
    rows = raw.shape[0]
    first = (lax.broadcasted_iota(jnp.int32, (rows, W_HALF), 0) == 0) & (i == 0)
    parts = []
    for n in range(4):
        f = raw[:, n * W_HALF:(n + 1) * W_HALF] * dec
        o_ref[:, n * W_HALF:(n + 1) * W_HALF] = f
        if n % 2 == 1:
            f = jnp.where(first, 0.0, f)
        parts.append(jnp.sum(f * f, 0, keepdims=True))
    ssq = jnp.concatenate(parts, -1)

    @pl.when(i == 0)
    def _():
        ssq_ref[...] = ssq

    @pl.when(i > 0)
    def _():
        ssq_ref[...] += ssq


def _hyena_filters(seq, w1, b1, w2, b2, w3, freq):
    feats, decay = _hyena_consts(seq)
    w1 = jnp.pad(w1, ((0, 40 - w1.shape[0]), (0, 0)))
    rows = 256
    const2 = lambda i: (0, 0)
    tok = lambda i: (i, 0)
    vec = lambda a: a.reshape(1, -1)
    args = [jnp.asarray(feats), jnp.asarray(decay), w1, vec(b1), w2, vec(b2), w3, vec(freq)]
    return pl.pallas_call(
        _filt_kernel,
        grid=(seq // rows,),
        in_specs=[pl.BlockSpec((rows, 40), tok), pl.BlockSpec((rows, W_HALF), tok)]
                 + [pl.BlockSpec(a.shape, const2) for a in args[2:]],
        out_specs=[pl.BlockSpec((rows, 4 * W_HALF), tok), pl.BlockSpec((1, 4 * W_HALF), const2)],
        out_shape=[jax.ShapeDtypeStruct((seq, 4 * W_HALF), F32),
                   jax.ShapeDtypeStruct((1, 4 * W_HALF), F32)],
        compiler_params=_cparams(("arbitrary",)),
        name="hyena_filters",
    )(*args)


def _circular(filt, seq):
    f = filt.reshape(seq, 2, 2, W_HALF)
    fwd = jnp.moveaxis(f[:, :, 0], 1, 0)
    bwd = jnp.moveaxis(f[:, :, 1], 1, 0)
    return jnp.concatenate([fwd, jnp.zeros((2, 1, W_HALF), F32), bwd[:, :0:-1]], 1)


def _dft_consts_two_stage(seq):
    n = 2 * seq
    p = DFT_P
    q = n // p
    a = np.arange(q)[None, :]
    k1 = np.arange(q)[:, None]
    ang1 = 2.0 * np.pi * a * k1 / q
    f1 = np.empty((2 * q, q))
    f1[0::2] = np.cos(ang1)
    f1[1::2] = -np.sin(ang1)
    f3 = np.empty((q, 2 * q))
    f3[:, 0::2] = np.cos(ang1).T / n
    f3[:, 1::2] = -np.sin(ang1).T / n
    qq = np.arange(p)[None, None, :]
    k2 = np.arange(p)[None, :, None]
    kk1 = np.arange(q)[:, None, None]
    ang = 2.0 * np.pi * (qq * k2 / p + qq * kk1 / n)
    gr, gi = np.cos(ang), -np.sin(ang)
    g = np.concatenate([np.concatenate([gr, -gi], 2), np.concatenate([gi, gr], 2)], 1)
    grt, git = np.swapaxes(gr, 1, 2), np.swapaxes(gi, 1, 2)
    ginv = np.concatenate([np.concatenate([grt, git], 2), np.concatenate([-git, grt], 2)], 1)
    half = q // 2
    return (f1.astype(np.float32), f1[:, :half].astype(np.float32), f3[:half].astype(np.float32),
            g.astype(np.float32), ginv.astype(np.float32))


def _dft_consts_one_stage(seq):
    n = 2 * seq
    t = np.arange(n)[None, :]
    f = np.arange(n)[:, None]
    ang = 2.0 * np.pi * t * f / n
    fwd = np.concatenate([np.cos(ang), -np.sin(ang)], 0)
    inv = np.concatenate([np.cos(ang), -np.sin(ang)], 1)[:seq] / n
    return fwd.astype(np.float32), fwd[:, :seq].astype(np.float32), inv.astype(np.float32)


def _expand_rows(f):
    return np.kron(f, np.eye(DFT_ROWS, dtype=f.dtype))


def _stage_kernel(f_ref, x_ref, o_ref, *, exact):
    k, rows, ch = x_ref.shape[1:]
    x = x_ref[0].reshape(k * rows, ch)
    if exact:
        res = jnp.dot(f_ref[...], x, precision=HIGHEST, preferred_element_type=F32)
    else:
        res = _dot_bf16(f_ref[...], x.astype(BF16))
    o_ref[0] = res.reshape(o_ref.shape[1:])


def _stage(f, x, exact=False):
    bsz, k, p, ch = x.shape
    rr = f.shape[0]
    fx = jnp.asarray(_expand_rows(f), dtype=F32 if exact else BF16)
    return pl.pallas_call(
        functools.partial(_stage_kernel, exact=exact),
        grid=(bsz, p // DFT_ROWS),
        in_specs=[pl.BlockSpec(fx.shape, lambda b, j: (0, 0)),
                  pl.BlockSpec((1, k, DFT_ROWS, ch), lambda b, j: (b, 0, j, 0))],
        out_specs=pl.BlockSpec((1, rr, DFT_ROWS, ch), lambda b, j: (b, 0, j, 0)),
        out_shape=jax.ShapeDtypeStruct((bsz, rr, p, ch), F32),
        compiler_params=_cparams(("parallel", "parallel")),
        name="dft_stage",
    )(fx, x)


def _spectrum_kernel(a_ref, g_ref, ssq_ref, o_ref):
    x = jnp.dot(g_ref[0], a_ref[0, 0], precision=HIGHEST, preferred_element_type=F32)
    o_ref[0, 0] = x * lax.rsqrt(ssq_ref[0, 0:1] + ssq_ref[0, 1:2] + 1e-6)


def _spectrum(a, g, ssq):
    nb, q, k, ch = a.shape
    rr = g.shape[1]
    return pl.pallas_call(
        _spectrum_kernel,
        grid=(nb, q),
        in_specs=[pl.BlockSpec((1, 1, k, ch), lambda b, j: (b, j, 0, 0)),
                  pl.BlockSpec((1, rr, k), lambda b, j: (j, 0, 0)),
                  pl.BlockSpec((1, 2, ch), lambda b, j: (b, 0, 0))],
        out_specs=pl.BlockSpec((1, 1, rr, ch), lambda b, j: (b, j, 0, 0)),
        out_shape=jax.ShapeDtypeStruct((nb, q, rr, ch), F32),
        compiler_params=_cparams(("parallel", "parallel")),
        name="hyena_spectrum",
    )(a, g, ssq)


def _spectral_conv(a, g_ref, h_ref, gi_ref):
    x = _dot_bf16(g_ref[0], a.astype(BF16))
    half = x.shape[0] // 2
    xr, xi = x[:half], x[half:]
    hr, hi = h_ref[0, 0, :half], h_ref[0, 0, half:]
    y = jnp.concatenate([xr * hr - xi * hi, xr * hi + xi * hr], 0)
    return _dot_bf16(gi_ref[0], y.astype(BF16))


def _mid_kernel(a_ref, g_ref, h_ref, gi_ref, o_ref):
    o_ref[0, 0] = _spectral_conv(a_ref[0, 0], g_ref, h_ref, gi_ref)


def _mid_gate_kernel(a_ref, g_ref, h_ref, gi_ref, hx_ref, bias_ref, o_ref):
    a = a_ref[0, 0]
    o_ref[0, 0] = hx_ref[0, 0] * (_spectral_conv(a, g_ref, h_ref, gi_ref) + bias_ref[...] * a)


def _mid(a, g, h, gi, order, gate=None):
    bsz, q, k, ch = a.shape
    rr = g.shape[1]
    ro = gi.shape[1]
    in_specs = [pl.BlockSpec((1, 1, k, ch), lambda b, j: (b, j, 0, 0)),
                pl.BlockSpec((1, rr, k), lambda b, j: (j, 0, 0)),
                pl.BlockSpec((1, 1, rr, ch), lambda b, j: (order, j, 0, 0)),
                pl.BlockSpec((1, ro, rr), lambda b, j: (j, 0, 0))]
    args = [a, g, h, gi]
    kern = _mid_kernel
    if gate is not None:
        in_specs += [pl.BlockSpec((1, 1, ro, ch), lambda b, j: (b, j, 0, 0)),
                     pl.BlockSpec((1, ch), lambda b, j: (0, 0))]
        args += list(gate)
        kern = _mid_gate_kernel
    return pl.pallas_call(
        kern,
        grid=(bsz, q),
        in_specs=in_specs,
        out_specs=pl.BlockSpec((1, 1, ro, ch), lambda b, j: (b, j, 0, 0)),
        out_shape=jax.ShapeDtypeStruct((bsz, q, ro, ch), F32),
        compiler_params=_cparams(("parallel", "parallel")),
        name="hyena_mid",
    )(*args)


def _last_stage_kernel(f3_ref, bt_ref, hx_ref, z_ref, bias_ref, o_ref):
    k, rows, ch = bt_ref.shape[1:]
    conv = _dot_bf16(f3_ref[...], bt_ref[0].reshape(k * rows, ch).astype(BF16))
    flat = conv.shape
    out = hx_ref[0].reshape(flat) * (conv + bias_ref[...] * z_ref[0].reshape(flat))
    o_ref[0] = out.reshape(o_ref.shape[1:])


def _last_stage(f3, bt, hx, z, bias):
    bsz, k, p, ch = bt.shape
    rr = f3.shape[0]
    fx = jnp.asarray(_expand_rows(f3), dtype=BF16)
    act = pl.BlockSpec((1, rr, DFT_ROWS, ch), lambda b, j: (b, 0, j, 0))
    return pl.pallas_call(
        _last_stage_kernel,
        grid=(bsz, p // DFT_ROWS),
        in_specs=[pl.BlockSpec(fx.shape, lambda b, j: (0, 0)),
                  pl.BlockSpec((1, k, DFT_ROWS, ch), lambda b, j: (b, 0, j, 0)),
                  act, act,
                  pl.BlockSpec((1, ch), lambda b, j: (0, 0))],
        out_specs=act,
        out_shape=jax.ShapeDtypeStruct((bsz, rr, p, ch), F32),
        compiler_params=_cparams(("parallel", "parallel")),
        name="dft_last_stage",
    )(fx, bt, hx, z, bias)


def _hyena(hv, hx1, hx2, filt, ssq, bias, bsz, seq):
    ch = W_HALF
    circ = _circular(filt, seq)
    ssq = ssq.reshape(2, 2, ch)
    if seq <= 512:
        fwd_full, fwd_half, inv = _dft_consts_one_stage(seq)
        spec = _spectrum(circ[:, None], jnp.asarray(fwd_full)[None], ssq)
        fwd_half = jnp.asarray(fwd_half, dtype=BF16)[None]
        inv = jnp.asarray(inv, dtype=BF16)[None]
        shape4 = (bsz, 1, seq, ch)
        z = _mid(hv.reshape(shape4), fwd_half, spec, inv, 0, gate=(hx1.reshape(shape4), bias[0:1]))
        y = _mid(z, fwd_half, spec, inv, 1, gate=(hx2.reshape(shape4), bias[1:2]))
        return y.reshape(bsz * seq, ch)
    f1_full, f1_half, f3, g, ginv = _dft_consts_two_stage(seq)
    p = DFT_P
    q = 2 * seq // p
    spec = _spectrum(_stage(f1_full, circ.reshape(2, q, p, ch), exact=True).reshape(2, q, 2 * p, ch),
                     jnp.asarray(g), ssq)
    g = jnp.asarray(g, dtype=BF16)
    ginv = jnp.asarray(ginv, dtype=BF16)
    nat = (bsz, q // 2, p, ch)
    out = hv.reshape(nat)
    for order, hx in enumerate((hx1, hx2)):
        a = _stage(f1_half, out).reshape(bsz, q, 2 * p, ch)
        bt = _mid(a, g, spec, ginv, order).reshape(bsz, 2 * q, p, ch)
        out = _last_stage(f3, bt, hx.reshape(nat), out, bias[order:order + 1])
    return out.reshape(bsz * seq, ch)


def _block_diag(w):
    hh, blk, _ = w.shape
    eye = jnp.eye(hh, dtype=w.dtype)
    return jnp.einsum('hij,hg->higj', w, eye).reshape(hh * blk, hh * blk)


def _group_ones(width):
    idx = np.arange(width) // RW_HEAD
    return jnp.asarray((idx[:, None] == idx[None, :]).astype(np.float32), dtype=BF16)


def _even_layer(x, mods, h0, bsz, seq, line, pw):
    ya, gg, a, u = _ab_in(x, mods, pw['w_in'], pw['sc_conv'], pw['lru_conv'], pw['lru_conv_b'],
                          pw['wg'], pw['bg'], pw['nsp'], line)
    hf, hb, fin = _lru_scan(a, u, h0, bsz, seq)
    x = _ab_out(x, mods, ya, gg, hf.reshape(-1, W_HALF), hb.reshape(-1, W_HALF),
                pw['w_out'], pw['ln1_g'], pw['ln1_b'])
    return x, fin


def _odd_layer(x, mods, s0, bsz, seq, line, pw):
    ones = _group_ones(W_HALF)
    (hv, hx1, hx2, r, v, kk, w, kd, b, bonus, sg) = _cd_in(
        x, mods, pw['w_in'], pw['hy_conv'], pw['rw_mu'], pw['rw_mu_x'], pw['l1'], pw['l2'], pw['l0'],
        pw['rw_kk'], pw['rw_ka'], pw['rw_rk'], ones, line)
    filt, ssq = _hyena_filters(seq, pw['hy_w1'], pw['hy_b1'], pw['hy_w2'], pw['hy_b2'], pw['hy_w3'],
                               pw['hy_freq'])
    yc = _hyena(hv, hx1, hx2, filt, ssq, pw['hy_bias'], bsz, seq)

    s0 = jnp.transpose(s0, (0, 1, 3, 2, 4)).reshape(bsz, 2, RW_HEAD, W_HALF)
    lane = np.arange(W_HALF)
    eye = jnp.asarray(((lane[None, :] % RW_HEAD) == np.arange(RW_HEAD)[:, None]).astype(np.float32))
    of, ob, fin = _rwkv_scan(r, v, kk, w, kd, b, s0, _group_ones(SCAN_LANES), eye, bsz, seq)

    def rows(o):
        o = o.reshape(bsz, seq // RW_HEAD, RW_HEAD, RW_HEADS, RW_HEAD)
        return jnp.transpose(o, (0, 1, 4, 3, 2)).reshape(bsz * seq, W_HALF)

    x = _cd_out(x, mods, yc, rows(of), rows(ob), bonus, sg, pw['rw_gn_g'], pw['rw_gn_b'], ones,
                pw['w_out'], pw['ln1_g'], pw['ln1_b'])
    fin = jnp.transpose(fin.reshape(bsz, 2, RW_HEAD, RW_HEADS, RW_HEAD), (0, 1, 3, 2, 4))
    return x, fin


def _layer_weights(p, l):
    j = l // 2
    row = lambda a: a.reshape(1, -1)
    pw = {'w_out': p['w_out'][l].astype(BF16), 'ln1_g': row(p['ln1_g'][l]), 'ln1_b': row(p['ln1_b'][l]),
          'ln2_g': row(p['ln2_g'][l]), 'ln2_b': row(p['ln2_b'][l]),
          'mlp_w1': p['mlp_w1'][l].astype(BF16), 'mlp_w2': p['mlp_w2'][l].astype(BF16)}
    if l % 2 == 0:
        gates = [_block_diag(p[name][j, d]) for d in range(2) for name in ('lru_wa', 'lru_wi')]
        biases = [p[name][j, d] for d in range(2) for name in ('lru_ba', 'lru_bi')]
        pw.update({'w_in': p['ab_w_in'][j].astype(BF16), 'sc_conv': p['sc_conv'][j],
                   'lru_conv': p['lru_conv'][j], 'lru_conv_b': row(p['lru_conv_b'][j]),
                   'wg': jnp.concatenate(gates, 1).astype(BF16), 'bg': row(jnp.concatenate(biases)),
                   'nsp': jax.nn.softplus(-p['lru_lambda'][j])})
    else:
        zeros = jnp.zeros((64, W_HALF), F32)
        l2 = jnp.concatenate([
            jnp.concatenate([p['rw_w2'][j, 0], zeros], 1), jnp.concatenate([zeros, p['rw_w2'][j, 1]], 1),
            jnp.concatenate([p['rw_a2'][j, 0], zeros], 1), jnp.concatenate([zeros, p['rw_a2'][j, 1]], 1)], 0)
        pw.update({'w_in': p['cd_w_in'][j].astype(BF16), 'hy_conv': p['hy_conv'][j],
                   'rw_mu': p['rw_mu'][j], 'rw_mu_x': p['rw_mu_x'][j],
                   'l1': jnp.concatenate([p['rw_w1'][j, 0], p['rw_w1'][j, 1],
                                          p['rw_a1'][j, 0], p['rw_a1'][j, 1]], 1).astype(BF16),
                   'l2': l2.astype(BF16),
                   'l0': jnp.stack([p['rw_w0'][j].reshape(-1), p['rw_a0'][j].reshape(-1)]),
                   'rw_kk': row(p['rw_kk'][j]), 'rw_ka': row(p['rw_ka'][j]), 'rw_rk': row(p['rw_rk'][j]),
                   'rw_gn_g': row(p['rw_gn_g'][j]), 'rw_gn_b': row(p['rw_gn_b'][j]),
                   'hy_w1': p['hy_w1'][j], 'hy_b1': p['hy_b1'][j], 'hy_w2': p['hy_w2'][j],
                   'hy_b2': p['hy_b2'][j], 'hy_w3': p['hy_w3'][j], 'hy_freq': p['hy_freq'][j],
                   'hy_bias': p['hy_bias'][j]})
    return pw


def _to_colmajor(x, bsz, rows):
    return jnp.transpose(x.reshape(bsz, rows, GRID_W, D_MODEL), (0, 2, 1, 3)).reshape(-1, D_MODEL)


def _from_colmajor(x, bsz, rows):
    return jnp.transpose(x.reshape(bsz, GRID_W, rows, D_MODEL), (0, 2, 1, 3)).reshape(-1, D_MODEL)


def _trunk(x, mods, init_lru, init_rwkv, rows, weights):
    bsz, seq, _ = x.shape
    x = x.reshape(bsz * seq, D_MODEL)
    new_lru, new_rwkv = [], []
    for l in range(DEPTH):
        j = l // 2
        pw = weights[l]
        if l % 2 == 0:
            line = seq if rows is None else GRID_W
            x, st = _even_layer(x, mods[l], init_lru[:, j], bsz, seq, line, pw)
            new_lru.append(st)
        else:
            if rows is None:
                x, st = _odd_layer(x, mods[l], init_rwkv[:, j], bsz, seq, seq, pw)
                x = _mlp(x, mods[l], pw['mlp_w1'], pw['mlp_w2'], pw['ln2_g'], pw['ln2_b'])
            else:
                x = _to_colmajor(x, bsz, rows)
                x, st = _odd_layer(x, mods[l], init_rwkv[:, j], bsz, seq, rows, pw)
                x = _mlp(x, mods[l], pw['mlp_w1'], pw['mlp_w2'], pw['ln2_g'], pw['ln2_b'])
                x = _from_colmajor(x, bsz, rows)
            new_rwkv.append(st)
            continue
        x = _mlp(x, mods[l], pw['mlp_w1'], pw['mlp_w2'], pw['ln2_g'], pw['ln2_b'])
    return x.reshape(bsz, seq, D_MODEL), jnp.stack(new_lru, 1), jnp.stack(new_rwkv, 1)


def kernel(x_prompt, x_sample, state_lru, state_rwkv, c, c_ctx, w_mod, b_mod, ln1_g, ln1_b, ln2_g, ln2_b, mlp_w1, mlp_w2, w_out, ab_w_in, sc_conv, lru_conv, lru_conv_b, lru_wa, lru_ba, lru_wi, lru_bi, lru_lambda, cd_w_in, hy_conv, hy_w1, hy_b1, hy_w2, hy_b2, hy_w3, hy_freq, hy_bias, rw_mu, rw_mu_x, rw_w0, rw_w1, rw_w2, rw_a0, rw_a1, rw_a2, rw_kk, rw_ka, rw_rk, rw_gn_g, rw_gn_b):
    p = dict(ln1_g=ln1_g, ln1_b=ln1_b, ln2_g=ln2_g, ln2_b=ln2_b,
             mlp_w1=mlp_w1, mlp_w2=mlp_w2, w_out=w_out, ab_w_in=ab_w_in, sc_conv=sc_conv,
             lru_conv=lru_conv, lru_conv_b=lru_conv_b, lru_wa=lru_wa, lru_ba=lru_ba, lru_wi=lru_wi,
             lru_bi=lru_bi, lru_lambda=lru_lambda, cd_w_in=cd_w_in, hy_conv=hy_conv, hy_w1=hy_w1,
             hy_b1=hy_b1, hy_w2=hy_w2, hy_b2=hy_b2, hy_w3=hy_w3, hy_freq=hy_freq, hy_bias=hy_bias,
             rw_mu=rw_mu, rw_mu_x=rw_mu_x, rw_w0=rw_w0, rw_w1=rw_w1, rw_w2=rw_w2, rw_a0=rw_a0,
             rw_a1=rw_a1, rw_a2=rw_a2, rw_kk=rw_kk, rw_ka=rw_ka, rw_rk=rw_rk, rw_gn_g=rw_gn_g,
             rw_gn_b=rw_gn_b)
    weights = [_layer_weights(p, l) for l in range(DEPTH)]
    nb, dec = x_prompt.shape[0], x_sample.shape[0]
    rpad = -(1 + dec) % 8
    cvec = jnp.concatenate([c_ctx[None, :], c, jnp.zeros((rpad, D_MODEL), F32)], 0)
    mods = _mods(cvec, w_mod, b_mod)
    mods = jnp.transpose(mods, (0, 2, 1, 3))
    zero_lru = jnp.zeros((nb, (DEPTH + 1) // 2, 2, W_HALF), F32)
    zero_rwkv = jnp.zeros((nb, DEPTH // 2, 2, RW_HEADS, RW_HEAD, RW_HEAD), F32)
    y_prompt, new_lru, new_rwkv = _trunk(x_prompt, mods[:, 0:1], zero_lru, zero_rwkv, None, weights)
    rows = x_sample.shape[1] // GRID_W
    y_sample, _, _ = _trunk(x_sample, mods[:, 1:1 + dec], state_lru, state_rwkv, rows, weights)
    return (y_prompt, y_sample, new_lru, new_rwkv)
```

```python
import functools
import math

import jax
import jax.numpy as jnp
import numpy as np
from jax import lax
from jax.experimental import pallas as pl
from jax.experimental.pallas import tpu as pltpu

F32 = jnp.float32
BF16 = jnp.bfloat16
HIGHEST = lax.Precision.HIGHEST

D_MODEL = 1024
DEPTH = 4
GRID_W = 64
W_HALF = D_MODEL // 2
LRU_HEADS = 8
RG_C = 8.0
HY_BANDS = 16
HY_TARGET = 1e-2
HY_SHORT_PCT = 0.3
HY_LONG_PCT = 1.5
RW_HEAD = 64
RW_HEADS = W_HALF // RW_HEAD
D_FF = 4 * D_MODEL
DN_ALPHA = (2 * DEPTH) ** 0.25
LN_EPS = 1e-5
GN_EPS = 64e-5

TM = 256
VMEM_LIMIT = 56 * 1024 * 1024
DFT_P = 128
DFT_ROWS = 8
FF_CHUNK = 1024
SCAN_LANES = 256
SCAN_BATCH = 8
SCAN_CHAINS = 8


def _cparams(sem):
    return pltpu.CompilerParams(dimension_semantics=sem, vmem_limit_bytes=VMEM_LIMIT)


def _shift_rows(x, off, line):
    if off == 0:
        return x
    n = x.shape[0]
    rolled = pltpu.roll(x, (-off) % n, 0)
    pos = lax.broadcasted_iota(jnp.int32, x.shape, 0) & (line - 1)
    valid = (pos + off >= 0) if off < 0 else (pos + off < line)
    return jnp.where(valid, rolled, 0.0)


def _dwconv_rows(x, w, pad_left, line):
    out = None
    for k in range(w.shape[0]):
        term = _shift_rows(x, k - pad_left, line) * w[k:k + 1]
        out = term if out is None else out + term
    return out


def _tshift_rows(x, line):
    return 0.5 * (_shift_rows(x, -1, line) + _shift_rows(x, 1, line))


def _layer_norm(v, g, b):
    mu = jnp.mean(v, -1, keepdims=True)
    vc = v - mu
    var = jnp.mean(vc * vc, -1, keepdims=True)
    return vc * lax.rsqrt(var + LN_EPS) * g + b


def _dot_bf16(a, b):
    return jnp.dot(a.astype(BF16), b, preferred_element_type=F32)


def _dot_f32_lhs(a, b):
    return lax.dot_general(a, b, (((1,), (0,)), ((), ())), preferred_element_type=F32)


def _group_sum(x, ones):
    hi = x.astype(BF16)
    lo = (x - hi.astype(F32)).astype(BF16)
    return (jnp.dot(hi, ones, preferred_element_type=F32)
            + jnp.dot(lo, ones, preferred_element_type=F32))


def _softplus(x):
    return jnp.maximum(x, 0.0) + jnp.log(1.0 + jnp.exp(-jnp.abs(x)))


def _sigmoid(x):
    return 1.0 / (1.0 + jnp.exp(-x))


def _mods_kernel(c_ref, w_ref, b_ref, o_ref):
    c = c_ref[...]
    s = c * _sigmoid(c)
    o_ref[0, 0] = jnp.dot(s, w_ref[0], precision=HIGHEST, preferred_element_type=F32) + b_ref[0, 0]


def _mods(cvec, w_mod, b_mod):
    r = cvec.shape[0]
    return pl.pallas_call(
        _mods_kernel,
        grid=(DEPTH, 6),
        in_specs=[pl.BlockSpec((r, D_MODEL), lambda l, n: (0, 0)),
                  pl.BlockSpec((1, D_MODEL, D_MODEL), lambda l, n: (l, 0, n)),
                  pl.BlockSpec((1, 1, 1, D_MODEL), lambda l, n: (l, n, 0, 0))],
        out_specs=pl.BlockSpec((1, 1, r, D_MODEL), lambda l, n: (l, n, 0, 0)),
        out_shape=jax.ShapeDtypeStruct((DEPTH, 6, r, D_MODEL), F32),
        compiler_params=_cparams(("parallel", "parallel")),
        name="mods",
    )(cvec, w_mod, b_mod.reshape(DEPTH, 6, 1, D_MODEL))


def _ab_in_kernel(x_ref, mod_ref, w_ref, scw_ref, lcw_ref, lcb_ref, wg_ref, bg_ref, nsp_ref,
                  ya_ref, gg_ref, a_ref, u_ref, *, line):
    x = x_ref[...]
    m = mod_ref[0]
    h = x * (1.0 + m[1:2]) + m[0:1]
    proj = _dot_bf16(h, w_ref[...])
    w = W_HALF
    s_b, s_c, s_v = proj[:, 0:w], proj[:, w:2 * w], proj[:, 2 * w:3 * w]
    g_lru, x_lru = proj[:, 3 * w:4 * w], proj[:, 4 * w:5 * w]
    ya_ref[...] = s_b * _dwconv_rows(s_c * s_v, scw_ref[...], 1, line)
    gg_ref[...] = 0.5 * g_lru * (1.0 + jnp.tanh(
        math.sqrt(2.0 / math.pi) * (g_lru + 0.044715 * (g_lru * g_lru * g_lru))))
    xc = _dwconv_rows(x_lru, lcw_ref[...], 2, line) + lcb_ref[...]
    gates = _dot_bf16(xc, wg_ref[...]) + bg_ref[...]
    for d in range(2):
        r = _sigmoid(gates[:, (2 * d) * w:(2 * d + 1) * w])
        i = _sigmoid(gates[:, (2 * d + 1) * w:(2 * d + 2) * w])
        a = jnp.exp(-RG_C * r * nsp_ref[d:d + 1])
        a_ref[d] = a
        u_ref[d] = jnp.sqrt(1.0 - a * a) * (i * xc)


def _ab_in(x, mods, w_in, scw, lcw, lcb, wg, bg, nsp, line):
    n = x.shape[0]
    tpb = n // mods.shape[0] // TM
    tok = lambda i: (i, 0)
    const2 = lambda i: (0, 0)
    half = jax.ShapeDtypeStruct((n, W_HALF), F32)
    both = jax.ShapeDtypeStruct((2, n, W_HALF), F32)
    return pl.pallas_call(
        functools.partial(_ab_in_kernel, line=line),
        grid=(n // TM,),
        in_specs=[pl.BlockSpec((TM, D_MODEL), tok),
                  pl.BlockSpec((1, 6, D_MODEL), lambda i: (i // tpb, 0, 0)),
                  pl.BlockSpec(w_in.shape, const2),
                  pl.BlockSpec(scw.shape, const2),
                  pl.BlockSpec(lcw.shape, const2),
                  pl.BlockSpec(lcb.shape, const2),
                  pl.BlockSpec(wg.shape, const2),
                  pl.BlockSpec(bg.shape, const2),
                  pl.BlockSpec(nsp.shape, const2)],
        out_specs=[pl.BlockSpec((TM, W_HALF), tok), pl.BlockSpec((TM, W_HALF), tok),
                   pl.BlockSpec((2, TM, W_HALF), lambda i: (0, i, 0)),
                   pl.BlockSpec((2, TM, W_HALF), lambda i: (0, i, 0))],
        out_shape=[half, half, both, both],
        compiler_params=_cparams(("parallel",)),
        name="ab_in",
    )(x, mods, w_in, scw, lcw, lcb, wg, bg, nsp)


def _lru_scan_kernel(af_ref, uf_ref, ab_ref, ub_ref, h0_ref, hf_ref, hb_ref, fin_ref, carry, *, tc, nc):
    c = pl.program_id(1)

    @pl.when(c == 0)
    def _():
        carry[...] = h0_ref[0]

    def step(s, hs):
        hf, hb = hs
        tb = tc - 1 - s
        hf = af_ref[0, 0, pl.ds(s, 1), :] * hf + uf_ref[0, 0, pl.ds(s, 1), :]
        hb = ab_ref[0, 0, pl.ds(tb, 1), :] * hb + ub_ref[0, 0, pl.ds(tb, 1), :]
        hf_ref[0, pl.ds(s, 1), :] = hf
        hb_ref[0, pl.ds(tb, 1), :] = hb
        return hf, hb

    hf, hb = lax.fori_loop(0, tc, step, (carry[0:1], carry[1:2]), unroll=8)
    carry[0:1] = hf
    carry[1:2] = hb

    @pl.when(c == nc - 1)
    def _():
        fin_ref[0, 0:1] = hf
        fin_ref[0, 1:2] = hb


def _lru_scan(a, u, h0, bsz, seq):
    tc = min(seq, 512)
    nc = seq // tc
    a = a.reshape(2, bsz, seq, W_HALF)
    u = u.reshape(2, bsz, seq, W_HALF)
    fwd = lambda b, c: (0, b, c, 0)
    bwd = lambda b, c: (1, b, nc - 1 - c, 0)
    blk = (1, 1, tc, W_HALF)
    seq_shape = jax.ShapeDtypeStruct((bsz, seq, W_HALF), F32)
    return pl.pallas_call(
        functools.partial(_lru_scan_kernel, tc=tc, nc=nc),
        grid=(bsz, nc),
        in_specs=[pl.BlockSpec(blk, fwd), pl.BlockSpec(blk, fwd),
                  pl.BlockSpec(blk, bwd), pl.BlockSpec(blk, bwd),
                  pl.BlockSpec((1, 2, W_HALF), lambda b, c: (b, 0, 0))],
        out_specs=[pl.BlockSpec((1, tc, W_HALF), lambda b, c: (b, c, 0)),
                   pl.BlockSpec((1, tc, W_HALF), lambda b, c: (b, nc - 1 - c, 0)),
                   pl.BlockSpec((1, 2, W_HALF), lambda b, c: (b, 0, 0))],
        out_shape=[seq_shape, seq_shape, jax.ShapeDtypeStruct((bsz, 2, W_HALF), F32)],
        scratch_shapes=[pltpu.VMEM((2, W_HALF), F32)],
        compiler_params=_cparams(("parallel", "arbitrary")),
        name="lru_scan",
    )(a, u, a, u, h0)


def _ab_out_kernel(x_ref, mod_ref, ya_ref, gg_ref, hf_ref, hb_ref, w_ref, g_ref, b_ref, o_ref):
    x = x_ref[...]
    m = mod_ref[0]
    yb = gg_ref[...] * (hf_ref[...] + hb_ref[...])
    y = _dot_bf16(ya_ref[...], w_ref[0:W_HALF, :]) + _dot_bf16(yb, w_ref[W_HALF:, :])
    o_ref[...] = _layer_norm(DN_ALPHA * x + m[2:3] * y, g_ref[...], b_ref[...])


def _ab_out(x, mods, ya, gg, hf, hb, w_out, ln_g, ln_b):
    n = x.shape[0]
    tpb = n // mods.shape[0] // TM
    tok = lambda i: (i, 0)
    const2 = lambda i: (0, 0)
    half = pl.BlockSpec((TM, W_HALF), tok)
    return pl.pallas_call(
        _ab_out_kernel,
        grid=(n // TM,),
        in_specs=[pl.BlockSpec((TM, D_MODEL), tok),
                  pl.BlockSpec((1, 6, D_MODEL), lambda i: (i // tpb, 0, 0)),
                  half, half, half, half,
                  pl.BlockSpec(w_out.shape, const2),
                  pl.BlockSpec((1, D_MODEL), const2), pl.BlockSpec((1, D_MODEL), const2)],
        out_specs=pl.BlockSpec((TM, D_MODEL), tok),
        out_shape=jax.ShapeDtypeStruct((n, D_MODEL), F32),
        compiler_params=_cparams(("parallel",)),
        name="ab_out",
    )(x, mods, ya, gg, hf, hb, w_out, ln_g, ln_b)


def _mlp_kernel(x_ref, mod_ref, w1_ref, w2_ref, g_ref, b_ref, o_ref):
    x = x_ref[...]
    m = mod_ref[0]
    h = (x * (1.0 + m[4:5]) + m[3:4]).astype(BF16)
    y = None
    for c in range(D_FF // FF_CHUNK):
        t = jnp.dot(h, w1_ref[:, c * FF_CHUNK:(c + 1) * FF_CHUNK], preferred_element_type=F32)
        t = jnp.maximum(t, 0.0)
        part = _dot_bf16(t * t, w2_ref[c * FF_CHUNK:(c + 1) * FF_CHUNK, :])
        y = part if y is None else y + part
    o_ref[...] = _layer_norm(DN_ALPHA * x + m[5:6] * y, g_ref[...], b_ref[...])


def _mlp(x, mods, w1, w2, ln_g, ln_b):
    n = x.shape[0]
    tpb = n // mods.shape[0] // TM
    tok = lambda i: (i, 0)
    const2 = lambda i: (0, 0)
    return pl.pallas_call(
        _mlp_kernel,
        grid=(n // TM,),
        in_specs=[pl.BlockSpec((TM, D_MODEL), tok),
                  pl.BlockSpec((1, 6, D_MODEL), lambda i: (i // tpb, 0, 0)),
                  pl.BlockSpec(w1.shape, const2, pipeline_mode=pl.Buffered(1)),
                  pl.BlockSpec(w2.shape, const2, pipeline_mode=pl.Buffered(1)),
                  pl.BlockSpec((1, D_MODEL), const2), pl.BlockSpec((1, D_MODEL), const2)],
        out_specs=pl.BlockSpec((TM, D_MODEL), tok),
        out_shape=jax.ShapeDtypeStruct((n, D_MODEL), F32),
        compiler_params=_cparams(("parallel",)),
        name="mlp",
    )(x, mods, w1, w2, ln_g, ln_b)


def _cd_in_kernel(x_ref, mod_ref, w_ref, hyc_ref, mu_ref, mux_ref, l1_ref, l2_ref, l0_ref,
                  kkp_ref, ka_ref, rk_ref, ones_ref,
                  hv_ref, hx1_ref, hx2_ref, r_ref, v_ref, kk_ref, w_out_ref, kd_ref, b_ref,
                  bonus_ref, sg_ref, *, line):
    x = x_ref[...]
    m = mod_ref[0]
    h = x * (1.0 + m[1:2]) + m[0:1]
    proj = _dot_bf16(h, w_ref[...])
    w = W_HALF
    u = _dwconv_rows(proj[:, 0:3 * w], hyc_ref[...], 1, line)
    hv_ref[...] = u[:, 0:w]
    hx1_ref[...] = u[:, w:2 * w]
    hx2_ref[...] = u[:, 2 * w:3 * w]

    mixed = []
    for n in range(4):
        t = proj[:, (3 + n) * w:(4 + n) * w]
        mixed.append(t + (_tshift_rows(t, line) - t) * mu_ref[n:n + 1])
    r, k, v, g = mixed
    dh = _tshift_rows(h, line) - h
    xw = h + dh * mux_ref[0:1]
    xa = h + dh * mux_ref[1:2]
    tw = jnp.tanh(_dot_bf16(xw, l1_ref[:, 0:128]))
    ta = _dot_bf16(xa, l1_ref[:, 128:256])
    zw = _dot_bf16(tw, l2_ref[0:128, :]) + l0_ref[0:1]
    za = _dot_bf16(ta, l2_ref[128:256, :]) + l0_ref[1:2]

    ones = ones_ref[...]
    kk = k * kkp_ref[...]
    kk = kk * lax.rsqrt(_group_sum(kk * kk, ones) + 1e-12)
    r_ref[...] = r
    v_ref[...] = v
    kk_ref[...] = kk
    kd_sum = None
    for d in range(2):
        w_raw = -_softplus(-zw[:, d * w:(d + 1) * w]) - 0.5
        w_out_ref[d] = jnp.exp(-jnp.exp(w_raw))
        a = _sigmoid(za[:, d * w:(d + 1) * w])
        kd = k * (1.0 + (a - 1.0) * ka_ref[...])
        kd_ref[d] = kd
        b_ref[d] = kk * a
        kd_sum = kd if kd_sum is None else kd_sum + kd
    bonus_ref[...] = _group_sum(r * kd_sum * rk_ref[...], ones) * v
    sg_ref[...] = _sigmoid(g)


def _cd_in(x, mods, w_in, hyc, mu, mux, l1, l2, l0, kkp, ka, rk, ones, line):
    n = x.shape[0]
    tpb = n // mods.shape[0] // TM
    tok = lambda i: (i, 0)
    const2 = lambda i: (0, 0)
    half = jax.ShapeDtypeStruct((n, W_HALF), F32)
    both = jax.ShapeDtypeStruct((2, n, W_HALF), F32)
    half_spec = pl.BlockSpec((TM, W_HALF), tok)
    both_spec = pl.BlockSpec((2, TM, W_HALF), lambda i: (0, i, 0))
    consts = [w_in, hyc, mu, mux, l1, l2, l0, kkp, ka, rk, ones]
    return pl.pallas_call(
        functools.partial(_cd_in_kernel, line=line),
        grid=(n // TM,),
        in_specs=[pl.BlockSpec((TM, D_MODEL), tok),
                  pl.BlockSpec((1, 6, D_MODEL), lambda i: (i // tpb, 0, 0))]
                 + [pl.BlockSpec(c.shape, const2) for c in consts],
        out_specs=[half_spec] * 6 + [both_spec] * 3 + [half_spec] * 2,
        out_shape=[half] * 6 + [both] * 3 + [half] * 2,
        compiler_params=_cparams(("parallel",)),
        name="cd_in",
    )(x, mods, *consts)


def _rwkv_scan_kernel(rf_ref, vf_ref, kkf_ref, wf_ref, kdf_ref, bf_ref,
                      rb_ref, vb_ref, kkb_ref, wb_ref, kdb_ref, bb_ref,
                      s0_ref, ones_ref, eye_ref,
                      of_ref, ob_ref, fin_ref, s_scr, sb_scr, o_scr, *, tc, nc, gb):
    c = pl.program_id(1)
    nsub = tc // RW_HEAD
    grp = 8
    per_sub = RW_HEAD // grp

    @pl.when(c == 0)
    def _():
        s_scr[...] = s0_ref[...]
        sb_scr[...] = s0_ref[...].astype(BF16)
        o_scr[...] = jnp.zeros_like(o_scr)

    ones = ones_ref[...]
    lane_pos = lax.broadcasted_iota(jnp.int32, (RW_HEAD, SCAN_LANES), 1) & (RW_HEAD - 1)
    chains = []
    for g in range(gb):
        chains.append((g, 0, rf_ref, vf_ref, kkf_ref, wf_ref, kdf_ref, bf_ref))
        chains.append((g, 1, rb_ref, vb_ref, kkb_ref, wb_ref, kdb_ref, bb_ref))

    hd = RW_HEAD

    tiles = [[(ch, pl.ds(j * SCAN_LANES, SCAN_LANES)) for ch in chains[c0:c0 + SCAN_CHAINS]]
             for c0 in range(0, len(chains), SCAN_CHAINS) for j in range(W_HALF // SCAN_LANES)]

    def put_outputs(streams, res_rows, step, base):
        for si, ((g, d, *_), ls) in enumerate(streams):
            pos = base + step if d == 0 else RW_HEAD - 1 - (base + step)
            o_scr[g, d, :, ls] = jnp.where(lane_pos == pos, res_rows[si * hd:(si + 1) * hd], o_scr[g, d, :, ls])

    def group(i, carry):
        sub = i // per_sub
        base = (i % per_sub) * grp
        row_f = pl.ds(pl.multiple_of(i * grp, grp), grp)
        row_b = pl.ds(pl.multiple_of(tc - grp - i * grp, grp), grp)

        def row(ref, g, d, ls, s, per_dir=False):
            q = s if d == 0 else grp - 1 - s
            tile = ref[0, g, row_b if d else row_f, ls] if per_dir else ref[g, row_b if d else row_f, ls]
            return tile[q:q + 1]

        pending = [None] * len(tiles)
        for s in range(grp):
            for ti, streams in enumerate(tiles):
                nst = len(streams)
                lhs = []
                for ((g, d, r_ref, v_ref, kk_ref, w_ref, kd_ref, b_ref), ls) in streams:
                    lhs.append(sb_scr[g, d, :, ls] * row(kk_ref, g, d, ls, s).astype(BF16))
                for ((g, d, r_ref, v_ref, kk_ref, w_ref, kd_ref, b_ref), ls) in streams:
                    lhs.append(eye_ref[:, ls] * row(v_ref, g, d, ls, s).astype(BF16))
                if pending[ti] is not None:
                    lhs += pending[ti]
                res = jnp.dot(jnp.concatenate(lhs, 0), ones, preferred_element_type=F32)
                if pending[ti] is not None:
                    put_outputs(streams, res[2 * nst * hd:], s - 1, base)
                nxt = []
                for si, ((g, d, r_ref, v_ref, kk_ref, w_ref, kd_ref, b_ref), ls) in enumerate(streams):
                    sa = res[si * hd:(si + 1) * hd]
                    vcol = res[(nst + si) * hd:(nst + si + 1) * hd]
                    st = (s_scr[g, d, :, ls] * row(w_ref, g, d, ls, s, True)
                          - sa * row(b_ref, g, d, ls, s, True) + vcol * row(kd_ref, g, d, ls, s, True))
                    s_scr[g, d, :, ls] = st
                    stb = st.astype(BF16)
                    sb_scr[g, d, :, ls] = stb
                    nxt.append(stb * row(r_ref, g, d, ls, s).astype(BF16))
                pending[ti] = nxt
        for ti, streams in enumerate(tiles):
            res = jnp.dot(jnp.concatenate(pending[ti], 0), ones, preferred_element_type=F32)
            put_outputs(streams, res, grp - 1, base)

        @pl.when(i % per_sub == per_sub - 1)
        def _():
            for g in range(gb):
                for d, (o_ref, blk) in enumerate(((of_ref, sub), (ob_ref, nsub - 1 - sub))):
                    ot = o_scr[g, d].T
                    for h in range(RW_HEADS):
                        o_ref[g, blk, :, h * hd:(h + 1) * hd] = ot[h * hd:(h + 1) * hd, :]

        return carry

    lax.fori_loop(0, tc // grp, group, 0)

    @pl.when(c == nc - 1)
    def _():
        fin_ref[...] = s_scr[...]


def _rwkv_scan(r, v, kk, w, kd, b, s0, ones2, eye, bsz, seq):
    tc = 64
    nc = seq // tc
    gb = min(bsz, SCAN_BATCH)
    r, v, kk = (t.reshape(bsz, seq, W_HALF) for t in (r, v, kk))
    w, kd, b = (t.reshape(2, bsz, seq, W_HALF) for t in (w, kd, b))
    shared = (gb, tc, W_HALF)
    perdir = (1, gb, tc, W_HALF)
    sf = lambda i, c: (i, c, 0)
    sb = lambda i, c: (i, nc - 1 - c, 0)
    pf = lambda i, c: (0, i, c, 0)
    pb = lambda i, c: (1, i, nc - 1 - c, 0)
    nsub = tc // RW_HEAD
    o_shape = jax.ShapeDtypeStruct((bsz, seq // RW_HEAD, RW_HEAD, W_HALF), F32)
    o_blk = (gb, nsub, RW_HEAD, W_HALF)
    st_blk = (gb, 2, RW_HEAD, W_HALF)
    return pl.pallas_call(
        functools.partial(_rwkv_scan_kernel, tc=tc, nc=nc, gb=gb),
        grid=(bsz // gb, nc),
        in_specs=[pl.BlockSpec(shared, sf)] * 3 + [pl.BlockSpec(perdir, pf)] * 3
                 + [pl.BlockSpec(shared, sb)] * 3 + [pl.BlockSpec(perdir, pb)] * 3
                 + [pl.BlockSpec(st_blk, lambda i, c: (i, 0, 0, 0)),
                    pl.BlockSpec(ones2.shape, lambda i, c: (0, 0)),
                    pl.BlockSpec(eye.shape, lambda i, c: (0, 0))],
        out_specs=[pl.BlockSpec(o_blk, lambda i, c: (i, c, 0, 0)),
                   pl.BlockSpec(o_blk, lambda i, c: (i, nc - 1 - c, 0, 0)),
                   pl.BlockSpec(st_blk, lambda i, c: (i, 0, 0, 0))],
        out_shape=[o_shape, o_shape, jax.ShapeDtypeStruct((bsz, 2, RW_HEAD, W_HALF), F32)],
        scratch_shapes=[pltpu.VMEM(st_blk, F32), pltpu.VMEM(st_blk, BF16), pltpu.VMEM(st_blk, F32)],
        compiler_params=_cparams(("parallel", "arbitrary")),
        name="rwkv_scan",
    )(r, v, kk, w, kd, b, r, v, kk, w, kd, b, s0, ones2, eye)


def _cd_out_kernel(x_ref, mod_ref, yc_ref, of_ref, ob_ref, bonus_ref, sg_ref, gng_ref, gnb_ref,
                   ones_ref, w_ref, g_ref, b_ref, o_ref):
    x = x_ref[...]
    m = mod_ref[0]
    ones = ones_ref[...]
    o = of_ref[...] + ob_ref[...]
    oc = o - _group_sum(o, ones) * (1.0 / RW_HEAD)
    on = oc * lax.rsqrt(_group_sum(oc * oc, ones) * (1.0 / RW_HEAD) + GN_EPS)
    yd = (on * gng_ref[...] + gnb_ref[...] + bonus_ref[...]) * sg_ref[...]
    y = _dot_bf16(yc_ref[...], w_ref[0:W_HALF, :]) + _dot_bf16(yd, w_ref[W_HALF:, :])
    o_ref[...] = _layer_norm(DN_ALPHA * x + m[2:3] * y, g_ref[...], b_ref[...])


def _cd_out(x, mods, yc, of, ob, bonus, sg, gng, gnb, ones, w_out, ln_g, ln_b):
    n = x.shape[0]
    tpb = n // mods.shape[0] // TM
    tok = lambda i: (i, 0)
    const2 = lambda i: (0, 0)
    half = pl.BlockSpec((TM, W_HALF), tok)
    vec = pl.BlockSpec((1, W_HALF), const2)
    return pl.pallas_call(
        _cd_out_kernel,
        grid=(n // TM,),
        in_specs=[pl.BlockSpec((TM, D_MODEL), tok),
                  pl.BlockSpec((1, 6, D_MODEL), lambda i: (i // tpb, 0, 0)),
                  half, half, half, half, half, vec, vec,
                  pl.BlockSpec(ones.shape, const2),
                  pl.BlockSpec(w_out.shape, const2),
                  pl.BlockSpec((1, D_MODEL), const2), pl.BlockSpec((1, D_MODEL), const2)],
        out_specs=pl.BlockSpec((TM, D_MODEL), tok),
        out_shape=jax.ShapeDtypeStruct((n, D_MODEL), F32),
        compiler_params=_cparams(("parallel",)),
        name="cd_out",
    )(x, mods, yc, of, ob, bonus, sg, gng, gnb, ones, w_out, ln_g, ln_b)


def _hyena_consts(seq):
    tn = np.linspace(0.0, 1.0, seq, dtype=np.float32)
    tr = np.arange(seq, dtype=np.float32)
    bands = np.linspace(1e-4, HY_BANDS - 1, HY_BANDS, dtype=np.float32)
    ang = np.float32(2.0 * math.pi / seq) * tr[:, None] * bands[None, :]
    feats = np.concatenate([tn[:, None], np.cos(ang), -np.sin(ang)], -1).astype(np.float32)
    feats = np.pad(feats, ((0, 0), (0, 40 - feats.shape[1])))
    deltas = np.abs(np.linspace(math.log(HY_TARGET) / HY_LONG_PCT, math.log(HY_TARGET) / HY_SHORT_PCT,
                                W_HALF, dtype=np.float32))
    decay = np.exp(-tn[:, None] * deltas[None, :]).astype(np.float32)
    lag = np.concatenate([np.arange(seq), [0], np.arange(seq - 1, 0, -1)])
    return feats[lag], decay[lag]


def _filt_kernel(f_ref, dec_ref, w1_ref, b1_ref, w2_ref, b2_ref, w3_ref, fr_ref, o_ref, ssq_ref, *, half):
    i = pl.program_id(0)
    fr = fr_ref[...]
    hid = jnp.sin(fr * (jnp.dot(f_ref[...], w1_ref[...], precision=HIGHEST,
                                preferred_element_type=F32) + b1_ref[...]))
    hid = jnp.sin(fr * (jnp.dot(hid, w2_ref[...], precision=HIGHEST,
                                preferred_element_type=F32) + b2_ref[...]))
    raw = jnp.dot(hid, w3_ref[...], precision=HIGHEST, preferred_element_type=F32)
    dec = dec_ref[...]
    rows = raw.shape[0]
    backward = i >= half
    middle = (lax.broadcasted_iota(jnp.int32, (rows, W_HALF), 0) == 0) & (i == half)
    parts = []
    for order in range(2):
        fwd = raw[:, (2 * order) * W_HALF:(2 * order + 1) * W_HALF]
        bwd = raw[:, (2 * order + 1) * W_HALF:(2 * order + 2) * W_HALF]
        f = jnp.where(middle, 0.0, jnp.where(backward, bwd, fwd) * dec)
        o_ref[order] = f
        parts.append(jnp.sum(f * f, 0, keepdims=True))
    ssq = jnp.concatenate(parts, 0)

    @pl.when(i == 0)
    def _():
        ssq_ref[...] = ssq

    @pl.when(i > 0)
    def _():
        ssq_ref[...] += ssq


def _hyena_filters(seq, w1, b1, w2, b2, w3, freq):
    feats, decay = _hyena_consts(seq)
    w1 = jnp.pad(w1, ((0, 40 - w1.shape[0]), (0, 0)))
    rows = 256
    const2 = lambda i: (0, 0)
    tok = lambda i: (i, 0)
    vec = lambda a: a.reshape(1, -1)
    args = [jnp.asarray(feats), jnp.asarray(decay), w1, vec(b1), w2, vec(b2), w3, vec(freq)]
    return pl.pallas_call(
        functools.partial(_filt_kernel, half=seq // rows),
        grid=(2 * seq // rows,),
        in_specs=[pl.BlockSpec((rows, 40), tok), pl.BlockSpec((rows, W_HALF), tok)]
                 + [pl.BlockSpec(a.shape, const2) for a in args[2:]],
        out_specs=[pl.BlockSpec((2, rows, W_HALF), lambda i: (0, i, 0)), pl.BlockSpec((2, W_HALF), const2)],
        out_shape=[jax.ShapeDtypeStruct((2, 2 * seq, W_HALF), F32),
                   jax.ShapeDtypeStruct((2, W_HALF), F32)],
        compiler_params=_cparams(("arbitrary",)),
        name="hyena_filters",
    )(*args)


def _dft_consts_two_stage(seq):
    n = 2 * seq
    p = DFT_P
    q = n // p
    a = np.arange(q)[None, :]
    k1 = np.arange(q)[:, None]
    ang1 = 2.0 * np.pi * a * k1 / q
    f1 = np.empty((2 * q, q))
    f1[0::2] = np.cos(ang1)
    f1[1::2] = -np.sin(ang1)
    f3 = np.empty((q, 2 * q))
    f3[:, 0::2] = np.cos(ang1).T / n
    f3[:, 1::2] = -np.sin(ang1).T / n
    qq = np.arange(p)[None, None, :]
    k2 = np.arange(p)[None, :, None]
    kk1 = np.arange(q)[:, None, None]
    ang = 2.0 * np.pi * (qq * k2 / p + qq * kk1 / n)
    gr, gi = np.cos(ang), -np.sin(ang)
    g = np.concatenate([np.concatenate([gr, -gi], 2), np.concatenate([gi, gr], 2)], 1)
    grt, git = np.swapaxes(gr, 1, 2), np.swapaxes(gi, 1, 2)
    ginv = np.concatenate([np.concatenate([grt, git], 2), np.concatenate([-git, grt], 2)], 1)
    half = q // 2
    return (f1.astype(np.float32), f1[:, :half].astype(np.float32), f3[:half].astype(np.float32),
            g.astype(np.float32), ginv.astype(np.float32))


def _dft_consts_one_stage(seq):
    n = 2 * seq
    t = np.arange(n)[None, :]
    f = np.arange(n)[:, None]
    ang = 2.0 * np.pi * t * f / n
    fwd = np.concatenate([np.cos(ang), -np.sin(ang)], 0)
    inv = np.concatenate([np.cos(ang), -np.sin(ang)], 1)[:seq] / n
    return fwd.astype(np.float32), fwd[:, :seq].astype(np.float32), inv.astype(np.float32)


def _expand_rows(f):
    return np.kron(f, np.eye(DFT_ROWS, dtype=f.dtype))


def _stage_kernel(f_ref, x_ref, o_ref, *, exact):
    k, rows, ch = x_ref.shape[1:]
    x = x_ref[0].reshape(k * rows, ch)
    if exact:
        res = jnp.dot(f_ref[...], x, precision=HIGHEST, preferred_element_type=F32)
    else:
        res = _dot_bf16(f_ref[...], x.astype(BF16))
    o_ref[0] = res.reshape(o_ref.shape[1:])


def _stage(f, x, exact=False):
    bsz, k, p, ch = x.shape
    rr = f.shape[0]
    fx = jnp.asarray(_expand_rows(f), dtype=F32 if exact else BF16)
    return pl.pallas_call(
        functools.partial(_stage_kernel, exact=exact),
        grid=(bsz, p // DFT_ROWS),
        in_specs=[pl.BlockSpec(fx.shape, lambda b, j: (0, 0)),
                  pl.BlockSpec((1, k, DFT_ROWS, ch), lambda b, j: (b, 0, j, 0))],
        out_specs=pl.BlockSpec((1, rr, DFT_ROWS, ch), lambda b, j: (b, 0, j, 0)),
        out_shape=jax.ShapeDtypeStruct((bsz, rr, p, ch), F32),
        compiler_params=_cparams(("parallel", "parallel")),
        name="dft_stage",
    )(fx, x)


def _spectrum_kernel(a_ref, g_ref, ssq_ref, o_ref):
    x = jnp.dot(g_ref[0], a_ref[0, 0], precision=HIGHEST, preferred_element_type=F32)
    o_ref[0, 0] = x * lax.rsqrt(ssq_ref[0] + 1e-6)


def _spectrum(a, g, ssq):
    nb, q, k, ch = a.shape
    rr = g.shape[1]
    return pl.pallas_call(
        _spectrum_kernel,
        grid=(nb, q),
        in_specs=[pl.BlockSpec((1, 1, k, ch), lambda b, j: (b, j, 0, 0)),
                  pl.BlockSpec((1, rr, k), lambda b, j: (j, 0, 0)),
                  pl.BlockSpec((1, 1, ch), lambda b, j: (b, 0, 0))],
        out_specs=pl.BlockSpec((1, 1, rr, ch), lambda b, j: (b, j, 0, 0)),
        out_shape=jax.ShapeDtypeStruct((nb, q, rr, ch), F32),
        compiler_params=_cparams(("parallel", "parallel")),
        name="hyena_spectrum",
    )(a, g, ssq)


def _spectral_conv(a, g_ref, h_ref, gi_ref):
    x = _dot_bf16(g_ref[0], a.astype(BF16))
    half = x.shape[0] // 2
    xr, xi = x[:half], x[half:]
    hr, hi = h_ref[0, 0, :half], h_ref[0, 0, half:]
    y = jnp.concatenate([xr * hr - xi * hi, xr * hi + xi * hr], 0)
    return _dot_bf16(gi_ref[0], y.astype(BF16))


def _mid_kernel(a_ref, g_ref, h_ref, gi_ref, o_ref):
    o_ref[0, 0] = _spectral_conv(a_ref[0, 0], g_ref, h_ref, gi_ref)


def _mid_gate_kernel(a_ref, g_ref, h_ref, gi_ref, hx_ref, bias_ref, o_ref):
    a = a_ref[0, 0]
    o_ref[0, 0] = hx_ref[0, 0] * (_spectral_conv(a, g_ref, h_ref, gi_ref) + bias_ref[...] * a)


def _mid(a, g, h, gi, order, gate=None):
    bsz, q, k, ch = a.shape
    rr = g.shape[1]
    ro = gi.shape[1]
    in_specs = [pl.BlockSpec((1, 1, k, ch), lambda b, j: (b, j, 0, 0)),
                pl.BlockSpec((1, rr, k), lambda b, j: (j, 0, 0)),
                pl.BlockSpec((1, 1, rr, ch), lambda b, j: (order, j, 0, 0)),
                pl.BlockSpec((1, ro, rr), lambda b, j: (j, 0, 0))]
    args = [a, g, h, gi]
    kern = _mid_kernel
    if gate is not None:
        in_specs += [pl.BlockSpec((1, 1, ro, ch), lambda b, j: (b, j, 0, 0)),
                     pl.BlockSpec((1, ch), lambda b, j: (0, 0))]
        args += list(gate)
        kern = _mid_gate_kernel
    return pl.pallas_call(
        kern,
        grid=(bsz, q),
        in_specs=in_specs,
        out_specs=pl.BlockSpec((1, 1, ro, ch), lambda b, j: (b, j, 0, 0)),
        out_shape=jax.ShapeDtypeStruct((bsz, q, ro, ch), F32),
        compiler_params=_cparams(("parallel", "parallel")),
        name="hyena_mid",
    )(*args)


def _last_stage_kernel(f3_ref, bt_ref, hx_ref, z_ref, bias_ref, o_ref):
    k, rows, ch = bt_ref.shape[1:]
    conv = _dot_bf16(f3_ref[...], bt_ref[0].reshape(k * rows, ch).astype(BF16))
    flat = conv.shape
    out = hx_ref[0].reshape(flat) * (conv + bias_ref[...] * z_ref[0].reshape(flat))
    o_ref[0] = out.reshape(o_ref.shape[1:])


def _last_stage(f3, bt, hx, z, bias):
    bsz, k, p, ch = bt.shape
    rr = f3.shape[0]
    fx = jnp.asarray(_expand_rows(f3), dtype=BF16)
    act = pl.BlockSpec((1, rr, DFT_ROWS, ch), lambda b, j: (b, 0, j, 0))
    return pl.pallas_call(
        _last_stage_kernel,
        grid=(bsz, p // DFT_ROWS),
        in_specs=[pl.BlockSpec(fx.shape, lambda b, j: (0, 0)),
                  pl.BlockSpec((1, k, DFT_ROWS, ch), lambda b, j: (b, 0, j, 0)),
                  act, act,
                  pl.BlockSpec((1, ch), lambda b, j: (0, 0))],
        out_specs=act,
        out_shape=jax.ShapeDtypeStruct((bsz, rr, p, ch), F32),
        compiler_params=_cparams(("parallel", "parallel")),
        name="dft_last_stage",
    )(fx, bt, hx, z, bias)


def _hyena(hv, hx1, hx2, circ, ssq, bias, bsz, seq):
    ch = W_HALF
    ssq = ssq.reshape(2, 1, ch)
    if seq <= 512:
        fwd_full, fwd_half, inv = _dft_consts_one_stage(seq)
        spec = _spectrum(circ[:, None], jnp.asarray(fwd_full)[None], ssq)
        fwd_half = jnp.asarray(fwd_half, dtype=BF16)[None]
        inv = jnp.asarray(inv, dtype=BF16)[None]
        shape4 = (bsz, 1, seq, ch)
        z = _mid(hv.reshape(shape4), fwd_half, spec, inv, 0, gate=(hx1.reshape(shape4), bias[0:1]))
        y = _mid(z, fwd_half, spec, inv, 1, gate=(hx2.reshape(shape4), bias[1:2]))
        return y.reshape(bsz * seq, ch)
    f1_full, f1_half, f3, g, ginv = _dft_consts_two_stage(seq)
    p = DFT_P
    q = 2 * seq // p
    spec = _spectrum(_stage(f1_full, circ.reshape(2, q, p, ch), exact=True).reshape(2, q, 2 * p, ch),
                     jnp.asarray(g), ssq)
    g = jnp.asarray(g, dtype=BF16)
    ginv = jnp.asarray(ginv, dtype=BF16)
    nat = (bsz, q // 2, p, ch)
    out = hv.reshape(nat)
    for order, hx in enumerate((hx1, hx2)):
        a = _stage(f1_half, out).reshape(bsz, q, 2 * p, ch)
        bt = _mid(a, g, spec, ginv, order).reshape(bsz, 2 * q, p, ch)
        out = _last_stage(f3, bt, hx.reshape(nat), out, bias[order:order + 1])
    return out.reshape(bsz * seq, ch)


def _block_diag(w):
    hh, blk, _ = w.shape
    eye = jnp.eye(hh, dtype=w.dtype)
    return jnp.einsum('hij,hg->higj', w, eye).reshape(hh * blk, hh * blk)


def _group_ones(width):
    idx = np.arange(width) // RW_HEAD
    return jnp.asarray((idx[:, None] == idx[None, :]).astype(np.float32), dtype=BF16)


def _even_layer(x, mods, h0, bsz, seq, line, pw):
    ya, gg, a, u = _ab_in(x, mods, pw['w_in'], pw['sc_conv'], pw['lru_conv'], pw['lru_conv_b'],
                          pw['wg'], pw['bg'], pw['nsp'], line)
    hf, hb, fin = _lru_scan(a, u, h0, bsz, seq)
    x = _ab_out(x, mods, ya, gg, hf.reshape(-1, W_HALF), hb.reshape(-1, W_HALF),
                pw['w_out'], pw['ln1_g'], pw['ln1_b'])
    return x, fin


def _odd_layer(x, mods, s0, bsz, seq, line, pw):
    ones = _group_ones(W_HALF)
    (hv, hx1, hx2, r, v, kk, w, kd, b, bonus, sg) = _cd_in(
        x, mods, pw['w_in'], pw['hy_conv'], pw['rw_mu'], pw['rw_mu_x'], pw['l1'], pw['l2'], pw['l0'],
        pw['rw_kk'], pw['rw_ka'], pw['rw_rk'], ones, line)
    filt, ssq = _hyena_filters(seq, pw['hy_w1'], pw['hy_b1'], pw['hy_w2'], pw['hy_b2'], pw['hy_w3'],
                               pw['hy_freq'])
    yc = _hyena(hv, hx1, hx2, filt, ssq, pw['hy_bias'], bsz, seq)

    s0 = jnp.transpose(s0, (0, 1, 3, 2, 4)).reshape(bsz, 2, RW_HEAD, W_HALF)
    lane = np.arange(W_HALF)
    eye = jnp.asarray(((lane[None, :] % RW_HEAD) == np.arange(RW_HEAD)[:, None]).astype(np.float32),
                      dtype=BF16)
    of, ob, fin = _rwkv_scan(r, v, kk, w, kd, b, s0, _group_ones(SCAN_LANES), eye, bsz, seq)
    x = _cd_out(x, mods, yc, of.reshape(-1, W_HALF), ob.reshape(-1, W_HALF), bonus, sg,
                pw['rw_gn_g'], pw['rw_gn_b'], ones, pw['w_out'], pw['ln1_g'], pw['ln1_b'])
    fin = jnp.transpose(fin.reshape(bsz, 2, RW_HEAD, RW_HEADS, RW_HEAD), (0, 1, 3, 2, 4))
    return x, fin


def _layer_weights(p, l):
    j = l // 2
    row = lambda a: a.reshape(1, -1)
    pw = {'w_out': p['w_out'][l].astype(BF16), 'ln1_g': row(p['ln1_g'][l]), 'ln1_b': row(p['ln1_b'][l]),
          'ln2_g': row(p['ln2_g'][l]), 'ln2_b': row(p['ln2_b'][l]),
          'mlp_w1': p['mlp_w1'][l].astype(BF16), 'mlp_w2': p['mlp_w2'][l].astype(BF16)}
    if l % 2 == 0:
        gates = [_block_diag(p[name][j, d]) for d in range(2) for name in ('lru_wa', 'lru_wi')]
        biases = [p[name][j, d] for d in range(2) for name in ('lru_ba', 'lru_bi')]
        pw.update({'w_in': p['ab_w_in'][j].astype(BF16), 'sc_conv': p['sc_conv'][j],
                   'lru_conv': p['lru_conv'][j], 'lru_conv_b': row(p['lru_conv_b'][j]),
                   'wg': jnp.concatenate(gates, 1).astype(BF16), 'bg': row(jnp.concatenate(biases)),
                   'nsp': jax.nn.softplus(-p['lru_lambda'][j])})
    else:
        zeros = jnp.zeros((64, W_HALF), F32)
        l2 = jnp.concatenate([
            jnp.concatenate([p['rw_w2'][j, 0], zeros], 1), jnp.concatenate([zeros, p['rw_w2'][j, 1]], 1),
            jnp.concatenate([p['rw_a2'][j, 0], zeros], 1), jnp.concatenate([zeros, p['rw_a2'][j, 1]], 1)], 0)
        pw.update({'w_in': p['cd_w_in'][j].astype(BF16), 'hy_conv': p['hy_conv'][j],
                   'rw_mu': p['rw_mu'][j], 'rw_mu_x': p['rw_mu_x'][j],
                   'l1': jnp.concatenate([p['rw_w1'][j, 0], p['rw_w1'][j, 1],
                                          p['rw_a1'][j, 0], p['rw_a1'][j, 1]], 1).astype(BF16),
                   'l2': l2.astype(BF16),
                   'l0': jnp.stack([p['rw_w0'][j].reshape(-1), p['rw_a0'][j].reshape(-1)]),
                   'rw_kk': row(p['rw_kk'][j]), 'rw_ka': row(p['rw_ka'][j]), 'rw_rk': row(p['rw_rk'][j]),
                   'rw_gn_g': row(p['rw_gn_g'][j]), 'rw_gn_b': row(p['rw_gn_b'][j]),
                   'hy_w1': p['hy_w1'][j], 'hy_b1': p['hy_b1'][j], 'hy_w2': p['hy_w2'][j],
                   'hy_b2': p['hy_b2'][j], 'hy_w3': p['hy_w3'][j], 'hy_freq': p['hy_freq'][j],
                   'hy_bias': p['hy_bias'][j]})
    return pw


def _to_colmajor(x, bsz, rows):
    return jnp.transpose(x.reshape(bsz, rows, GRID_W, D_MODEL), (0, 2, 1, 3)).reshape(-1, D_MODEL)


def _from_colmajor(x, bsz, rows):
    return jnp.transpose(x.reshape(bsz, GRID_W, rows, D_MODEL), (0, 2, 1, 3)).reshape(-1, D_MODEL)


def _trunk(x, mods, init_lru, init_rwkv, rows, weights):
    bsz, seq, _ = x.shape
    x = x.reshape(bsz * seq, D_MODEL)
    new_lru, new_rwkv = [], []
    for l in range(DEPTH):
        j = l // 2
        pw = weights[l]
        if l % 2 == 0:
            line = seq if rows is None else GRID_W
            x, st = _even_layer(x, mods[l], init_lru[:, j], bsz, seq, line, pw)
            new_lru.append(st)
        else:
            if rows is None:
                x, st = _odd_layer(x, mods[l], init_rwkv[:, j], bsz, seq, seq, pw)
                x = _mlp(x, mods[l], pw['mlp_w1'], pw['mlp_w2'], pw['ln2_g'], pw['ln2_b'])
            else:
                x = _to_colmajor(x, bsz, rows)
                x, st = _odd_layer(x, mods[l], init_rwkv[:, j], bsz, seq, rows, pw)
                x = _mlp(x, mods[l], pw['mlp_w1'], pw['mlp_w2'], pw['ln2_g'], pw['ln2_b'])
                x = _from_colmajor(x, bsz, rows)
            new_rwkv.append(st)
            continue
        x = _mlp(x, mods[l], pw['mlp_w1'], pw['mlp_w2'], pw['ln2_g'], pw['ln2_b'])
    return x.reshape(bsz, seq, D_MODEL), jnp.stack(new_lru, 1), jnp.stack(new_rwkv, 1)


def kernel(x_prompt, x_sample, state_lru, state_rwkv, c, c_ctx, w_mod, b_mod, ln1_g, ln1_b, ln2_g, ln2_b, mlp_w1, mlp_w2, w_out, ab_w_in, sc_conv, lru_conv, lru_conv_b, lru_wa, lru_ba, lru_wi, lru_bi, lru_lambda, cd_w_in, hy_conv, hy_w1, hy_b1, hy_w2, hy_b2, hy_w3, hy_freq, hy_bias, rw_mu, rw_mu_x, rw_w0, rw_w1, rw_w2, rw_a0, rw_a1, rw_a2, rw_kk, rw_ka, rw_rk, rw_gn_g, rw_gn_b):
    p = dict(ln1_g=ln1_g, ln1_b=ln1_b, ln2_g=ln2_g, ln2_b=ln2_b,
             mlp_w1=mlp_w1, mlp_w2=mlp_w2, w_out=w_out, ab_w_in=ab_w_in, sc_conv=sc_conv,
             lru_conv=lru_conv, lru_conv_b=lru_conv_b, lru_wa=lru_wa, lru_ba=lru_ba, lru_wi=lru_wi,
             lru_bi=lru_bi, lru_lambda=lru_lambda, cd_w_in=cd_w_in, hy_conv=hy_conv, hy_w1=hy_w1,
             hy_b1=hy_b1, hy_w2=hy_w2, hy_b2=hy_b2, hy_w3=hy_w3, hy_freq=hy_freq, hy_bias=hy_bias,
             rw_mu=rw_mu, rw_mu_x=rw_mu_x, rw_w0=rw_w0, rw_w1=rw_w1, rw_w2=rw_w2, rw_a0=rw_a0,
             rw_a1=rw_a1, rw_a2=rw_a2, rw_kk=rw_kk, rw_ka=rw_ka, rw_rk=rw_rk, rw_gn_g=rw_gn_g,
             rw_gn_b=rw_gn_b)
    weights = [_layer_weights(p, l) for l in range(DEPTH)]
    nb, dec = x_prompt.shape[0], x_sample.shape[0]
    rpad = -(1 + dec) % 8
    cvec = jnp.concatenate([c_ctx[None, :], c, jnp.zeros((rpad, D_MODEL), F32)], 0)
    mods = _mods(cvec, w_mod, b_mod)
    mods = jnp.transpose(mods, (0, 2, 1, 3))
    zero_lru = jnp.zeros((nb, (DEPTH + 1) // 2, 2, W_HALF), F32)
    zero_rwkv = jnp.zeros((nb, DEPTH // 2, 2, RW_HEADS, RW_HEAD, RW_HEAD), F32)
    y_prompt, new_lru, new_rwkv = _trunk(x_prompt, mods[:, 0:1], zero_lru, zero_rwkv, None, weights)
    rows = x_sample.shape[1] // GRID_W
    y_sample, _, _ = _trunk(x_sample, mods[:, 1:1 + dec], state_lru, state_rwkv, rows, weights)
    return (y_prompt, y_sample, new_lru, new_rwkv)
```

```python
import functools
import math

import jax
import jax.numpy as jnp
import numpy as np
from jax import lax
from jax.experimental import pallas as pl
from jax.experimental.pallas import tpu as pltpu

F32 = jnp.float32
BF16 = jnp.bfloat16
HIGHEST = lax.Precision.HIGHEST

D_MODEL = 1024
DEPTH = 4
GRID_W = 64
W_HALF = D_MODEL // 2
LRU_HEADS = 8
RG_C = 8.0
HY_BANDS = 16
HY_TARGET = 1e-2
HY_SHORT_PCT = 0.3
HY_LONG_PCT = 1.5
RW_HEAD = 64
RW_HEADS = W_HALF // RW_HEAD
D_FF = 4 * D_MODEL
DN_ALPHA = (2 * DEPTH) ** 0.25
LN_EPS = 1e-5
GN_EPS = 64e-5

TM = 256
VMEM_LIMIT = 56 * 1024 * 1024
DFT_P = 128
DFT_ROWS = 16
MID_BATCH = 8
FF_CHUNK = 1024
SCAN_LANES = 256
SCAN_BATCH = 8
SCAN_CHAINS = 8


def _cparams(sem):
    return pltpu.CompilerParams(dimension_semantics=sem, vmem_limit_bytes=VMEM_LIMIT)


def _shift_rows(x, off, line):
    if off == 0:
        return x
    n = x.shape[0]
    rolled = pltpu.roll(x, (-off) % n, 0)
    pos = lax.broadcasted_iota(jnp.int32, x.shape, 0) & (line - 1)
    valid = (pos + off >= 0) if off < 0 else (pos + off < line)
    return jnp.where(valid, rolled, 0.0)


def _dwconv_rows(x, w, pad_left, line):
    out = None
    for k in range(w.shape[0]):
        term = _shift_rows(x, k - pad_left, line) * w[k:k + 1]
        out = term if out is None else out + term
    return out


def _tshift_rows(x, line):
    return 0.5 * (_shift_rows(x, -1, line) + _shift_rows(x, 1, line))


def _layer_norm(v, g, b):
    mu = jnp.mean(v, -1, keepdims=True)
    vc = v - mu
    var = jnp.mean(vc * vc, -1, keepdims=True)
    return vc * lax.rsqrt(var + LN_EPS) * g + b


def _dot_bf16(a, b):
    return jnp.dot(a.astype(BF16), b, preferred_element_type=F32)


def _dot_f32_lhs(a, b):
    return lax.dot_general(a, b, (((1,), (0,)), ((), ())), preferred_element_type=F32)


def _group_sum(x, ones):
    hi = x.astype(BF16)
    lo = (x - hi.astype(F32)).astype(BF16)
    return (jnp.dot(hi, ones, preferred_element_type=F32)
            + jnp.dot(lo, ones, preferred_element_type=F32))


def _softplus(x):
    return jnp.maximum(x, 0.0) + jnp.log(1.0 + jnp.exp(-jnp.abs(x)))


def _sigmoid(x):
    return 1.0 / (1.0 + jnp.exp(-x))


def _mods_kernel(c_ref, w_ref, b_ref, o_ref):
    c = c_ref[...]
    s = c * _sigmoid(c)
    o_ref[0, 0] = jnp.dot(s, w_ref[0], precision=HIGHEST, preferred_element_type=F32) + b_ref[0, 0]


def _mods(cvec, w_mod, b_mod):
    r = cvec.shape[0]
    return pl.pallas_call(
        _mods_kernel,
        grid=(DEPTH, 6),
        in_specs=[pl.BlockSpec((r, D_MODEL), lambda l, n: (0, 0)),
                  pl.BlockSpec((1, D_MODEL, D_MODEL), lambda l, n: (l, 0, n)),
                  pl.BlockSpec((1, 1, 1, D_MODEL), lambda l, n: (l, n, 0, 0))],
        out_specs=pl.BlockSpec((1, 1, r, D_MODEL), lambda l, n: (l, n, 0, 0)),
        out_shape=jax.ShapeDtypeStruct((DEPTH, 6, r, D_MODEL), F32),
        compiler_params=_cparams(("parallel", "parallel")),
        name="mods",
    )(cvec, w_mod, b_mod.reshape(DEPTH, 6, 1, D_MODEL))


def _ab_in_kernel(x_ref, mod_ref, w_ref, scw_ref, lcw_ref, lcb_ref, wg_ref, bg_ref, nsp_ref,
                  ya_ref, gg_ref, a_ref, u_ref, *, line):
    x = x_ref[...]
    m = mod_ref[0]
    h = x * (1.0 + m[1:2]) + m[0:1]
    proj = _dot_bf16(h, w_ref[...])
    w = W_HALF
    s_b, s_c, s_v = proj[:, 0:w], proj[:, w:2 * w], proj[:, 2 * w:3 * w]
    g_lru, x_lru = proj[:, 3 * w:4 * w], proj[:, 4 * w:5 * w]
    ya_ref[...] = s_b * _dwconv_rows(s_c * s_v, scw_ref[...], 1, line)
    gg_ref[...] = 0.5 * g_lru * (1.0 + jnp.tanh(
        math.sqrt(2.0 / math.pi) * (g_lru + 0.044715 * (g_lru * g_lru * g_lru))))
    xc = _dwconv_rows(x_lru, lcw_ref[...], 2, line) + lcb_ref[...]
    gates = _dot_bf16(xc, wg_ref[...]) + bg_ref[...]
    for d in range(2):
        r = _sigmoid(gates[:, (2 * d) * w:(2 * d + 1) * w])
        i = _sigmoid(gates[:, (2 * d + 1) * w:(2 * d + 2) * w])
        a = jnp.exp(-RG_C * r * nsp_ref[d:d + 1])
        a_ref[d] = a
        u_ref[d] = jnp.sqrt(1.0 - a * a) * (i * xc)


def _ab_in(x, mods, w_in, scw, lcw, lcb, wg, bg, nsp, line):
    n = x.shape[0]
    tpb = n // mods.shape[0] // TM
    tok = lambda i: (i, 0)
    const2 = lambda i: (0, 0)
    half = jax.ShapeDtypeStruct((n, W_HALF), F32)
    both = jax.ShapeDtypeStruct((2, n, W_HALF), F32)
    return pl.pallas_call(
        functools.partial(_ab_in_kernel, line=line),
        grid=(n // TM,),
        in_specs=[pl.BlockSpec((TM, D_MODEL), tok),
                  pl.BlockSpec((1, 6, D_MODEL), lambda i: (i // tpb, 0, 0)),
                  pl.BlockSpec(w_in.shape, const2),
                  pl.BlockSpec(scw.shape, const2),
                  pl.BlockSpec(lcw.shape, const2),
                  pl.BlockSpec(lcb.shape, const2),
                  pl.BlockSpec(wg.shape, const2),
                  pl.BlockSpec(bg.shape, const2),
                  pl.BlockSpec(nsp.shape, const2)],
        out_specs=[pl.BlockSpec((TM, W_HALF), tok), pl.BlockSpec((TM, W_HALF), tok),
                   pl.BlockSpec((2, TM, W_HALF), lambda i: (0, i, 0)),
                   pl.BlockSpec((2, TM, W_HALF), lambda i: (0, i, 0))],
        out_shape=[half, half, both, both],
        compiler_params=_cparams(("parallel",)),
        name="ab_in",
    )(x, mods, w_in, scw, lcw, lcb, wg, bg, nsp)


def _lru_scan_kernel(af_ref, uf_ref, ab_ref, ub_ref, h0_ref, hf_ref, hb_ref, fin_ref, carry, *, tc, nc):
    c = pl.program_id(1)

    @pl.when(c == 0)
    def _():
        carry[...] = h0_ref[0]

    def step(s, hs):
        hf, hb = hs
        tb = tc - 1 - s
        hf = af_ref[0, 0, pl.ds(s, 1), :] * hf + uf_ref[0, 0, pl.ds(s, 1), :]
        hb = ab_ref[0, 0, pl.ds(tb, 1), :] * hb + ub_ref[0, 0, pl.ds(tb, 1), :]
        hf_ref[0, pl.ds(s, 1), :] = hf
        hb_ref[0, pl.ds(tb, 1), :] = hb
        return hf, hb

    hf, hb = lax.fori_loop(0, tc, step, (carry[0:1], carry[1:2]), unroll=8)
    carry[0:1] = hf
    carry[1:2] = hb

    @pl.when(c == nc - 1)
    def _():
        fin_ref[0, 0:1] = hf
        fin_ref[0, 1:2] = hb


def _lru_scan(a, u, h0, bsz, seq):
    tc = min(seq, 512)
    nc = seq // tc
    a = a.reshape(2, bsz, seq, W_HALF)
    u = u.reshape(2, bsz, seq, W_HALF)
    fwd = lambda b, c: (0, b, c, 0)
    bwd = lambda b, c: (1, b, nc - 1 - c, 0)
    blk = (1, 1, tc, W_HALF)
    seq_shape = jax.ShapeDtypeStruct((bsz, seq, W_HALF), F32)
    return pl.pallas_call(
        functools.partial(_lru_scan_kernel, tc=tc, nc=nc),
        grid=(bsz, nc),
        in_specs=[pl.BlockSpec(blk, fwd), pl.BlockSpec(blk, fwd),
                  pl.BlockSpec(blk, bwd), pl.BlockSpec(blk, bwd),
                  pl.BlockSpec((1, 2, W_HALF), lambda b, c: (b, 0, 0))],
        out_specs=[pl.BlockSpec((1, tc, W_HALF), lambda b, c: (b, c, 0)),
                   pl.BlockSpec((1, tc, W_HALF), lambda b, c: (b, nc - 1 - c, 0)),
                   pl.BlockSpec((1, 2, W_HALF), lambda b, c: (b, 0, 0))],
        out_shape=[seq_shape, seq_shape, jax.ShapeDtypeStruct((bsz, 2, W_HALF), F32)],
        scratch_shapes=[pltpu.VMEM((2, W_HALF), F32)],
        compiler_params=_cparams(("parallel", "arbitrary")),
        name="lru_scan",
    )(a, u, a, u, h0)


def _ab_out_kernel(x_ref, mod_ref, ya_ref, gg_ref, hf_ref, hb_ref, w_ref, g_ref, b_ref, o_ref):
    x = x_ref[...]
    m = mod_ref[0]
    yb = gg_ref[...] * (hf_ref[...] + hb_ref[...])
    y = _dot_bf16(ya_ref[...], w_ref[0:W_HALF, :]) + _dot_bf16(yb, w_ref[W_HALF:, :])
    o_ref[...] = _layer_norm(DN_ALPHA * x + m[2:3] * y, g_ref[...], b_ref[...])


def _ab_out(x, mods, ya, gg, hf, hb, w_out, ln_g, ln_b):
    n = x.shape[0]
    tpb = n // mods.shape[0] // TM
    tok = lambda i: (i, 0)
    const2 = lambda i: (0, 0)
    half = pl.BlockSpec((TM, W_HALF), tok)
    return pl.pallas_call(
        _ab_out_kernel,
        grid=(n // TM,),
        in_specs=[pl.BlockSpec((TM, D_MODEL), tok),
                  pl.BlockSpec((1, 6, D_MODEL), lambda i: (i // tpb, 0, 0)),
                  half, half, half, half,
                  pl.BlockSpec(w_out.shape, const2),
                  pl.BlockSpec((1, D_MODEL), const2), pl.BlockSpec((1, D_MODEL), const2)],
        out_specs=pl.BlockSpec((TM, D_MODEL), tok),
        out_shape=jax.ShapeDtypeStruct((n, D_MODEL), F32),
        compiler_params=_cparams(("parallel",)),
        name="ab_out",
    )(x, mods, ya, gg, hf, hb, w_out, ln_g, ln_b)


def _mlp_kernel(x_ref, mod_ref, w1_ref, w2_ref, g_ref, b_ref, o_ref):
    x = x_ref[...]
    m = mod_ref[0]
    h = (x * (1.0 + m[4:5]) + m[3:4]).astype(BF16)
    y = None
    for c in range(D_FF // FF_CHUNK):
        t = jnp.dot(h, w1_ref[:, c * FF_CHUNK:(c + 1) * FF_CHUNK], preferred_element_type=F32)
        t = jnp.maximum(t, 0.0)
        part = _dot_bf16(t * t, w2_ref[c * FF_CHUNK:(c + 1) * FF_CHUNK, :])
        y = part if y is None else y + part
    o_ref[...] = _layer_norm(DN_ALPHA * x + m[5:6] * y, g_ref[...], b_ref[...])


def _mlp(x, mods, w1, w2, ln_g, ln_b):
    n = x.shape[0]
    tpb = n // mods.shape[0] // TM
    tok = lambda i: (i, 0)
    const2 = lambda i: (0, 0)
    return pl.pallas_call(
        _mlp_kernel,
        grid=(n // TM,),
        in_specs=[pl.BlockSpec((TM, D_MODEL), tok),
                  pl.BlockSpec((1, 6, D_MODEL), lambda i: (i // tpb, 0, 0)),
                  pl.BlockSpec(w1.shape, const2, pipeline_mode=pl.Buffered(1)),
                  pl.BlockSpec(w2.shape, const2, pipeline_mode=pl.Buffered(1)),
                  pl.BlockSpec((1, D_MODEL), const2), pl.BlockSpec((1, D_MODEL), const2)],
        out_specs=pl.BlockSpec((TM, D_MODEL), tok),
        out_shape=jax.ShapeDtypeStruct((n, D_MODEL), F32),
        compiler_params=_cparams(("parallel",)),
        name="mlp",
    )(x, mods, w1, w2, ln_g, ln_b)


def _cd_in_kernel(x_ref, mod_ref, w_ref, hyc_ref, mu_ref, mux_ref, l1_ref, l2_ref, l0_ref,
                  kkp_ref, ka_ref, rk_ref, ones_ref,
                  hv_ref, hx1_ref, hx2_ref, r_ref, v_ref, kk_ref, w_out_ref, kd_ref, b_ref,
                  bonus_ref, sg_ref, *, line):
    x = x_ref[...]
    m = mod_ref[0]
    h = x * (1.0 + m[1:2]) + m[0:1]
    proj = _dot_bf16(h, w_ref[...])
    w = W_HALF
    u = _dwconv_rows(proj[:, 0:3 * w], hyc_ref[...], 1, line)
    hv_ref[...] = u[:, 0:w]
    hx1_ref[...] = u[:, w:2 * w]
    hx2_ref[...] = u[:, 2 * w:3 * w]

    mixed = []
    for n in range(4):
        t = proj[:, (3 + n) * w:(4 + n) * w]
        mixed.append(t + (_tshift_rows(t, line) - t) * mu_ref[n:n + 1])
    r, k, v, g = mixed
    dh = _tshift_rows(h, line) - h
    xw = h + dh * mux_ref[0:1]
    xa = h + dh * mux_ref[1:2]
    tw = jnp.tanh(_dot_bf16(xw, l1_ref[:, 0:128]))
    ta = _dot_bf16(xa, l1_ref[:, 128:256])
    zw = _dot_bf16(tw, l2_ref[0:128, :]) + l0_ref[0:1]
    za = _dot_bf16(ta, l2_ref[128:256, :]) + l0_ref[1:2]

    ones = ones_ref[...]
    kk = k * kkp_ref[...]
    kk = kk * lax.rsqrt(_group_sum(kk * kk, ones) + 1e-12)
    r_ref[...] = r
    v_ref[...] = v
    kk_ref[...] = kk
    kd_sum = None
    for d in range(2):
        w_raw = -_softplus(-zw[:, d * w:(d + 1) * w]) - 0.5
        w_out_ref[d] = jnp.exp(-jnp.exp(w_raw))
        a = _sigmoid(za[:, d * w:(d + 1) * w])
        kd = k * (1.0 + (a - 1.0) * ka_ref[...])
        kd_ref[d] = kd
        b_ref[d] = kk * a
        kd_sum = kd if kd_sum is None else kd_sum + kd
    bonus_ref[...] = _group_sum(r * kd_sum * rk_ref[...], ones) * v
    sg_ref[...] = _sigmoid(g)


def _cd_in(x, mods, w_in, hyc, mu, mux, l1, l2, l0, kkp, ka, rk, ones, line):
    n = x.shape[0]
    tpb = n // mods.shape[0] // TM
    tok = lambda i: (i, 0)
    const2 = lambda i: (0, 0)
    half = jax.ShapeDtypeStruct((n, W_HALF), F32)
    both = jax.ShapeDtypeStruct((2, n, W_HALF), F32)
    half_spec = pl.BlockSpec((TM, W_HALF), tok)
    both_spec = pl.BlockSpec((2, TM, W_HALF), lambda i: (0, i, 0))
    consts = [w_in, hyc, mu, mux, l1, l2, l0, kkp, ka, rk, ones]
    return pl.pallas_call(
        functools.partial(_cd_in_kernel, line=line),
        grid=(n // TM,),
        in_specs=[pl.BlockSpec((TM, D_MODEL), tok),
                  pl.BlockSpec((1, 6, D_MODEL), lambda i: (i // tpb, 0, 0))]
                 + [pl.BlockSpec(c.shape, const2) for c in consts],
        out_specs=[half_spec] * 6 + [both_spec] * 3 + [half_spec] * 2,
        out_shape=[half] * 6 + [both] * 3 + [half] * 2,
        compiler_params=_cparams(("parallel",)),
        name="cd_in",
    )(x, mods, *consts)


def _rwkv_scan_kernel(rf_ref, vf_ref, kkf_ref, wf_ref, kdf_ref, bf_ref,
                      rb_ref, vb_ref, kkb_ref, wb_ref, kdb_ref, bb_ref,
                      s0_ref, ones_ref, eye_ref,
                      of_ref, ob_ref, fin_ref, s_scr, sb_scr, o_scr, *, tc, nc, gb):
    c = pl.program_id(1)
    nsub = tc // RW_HEAD
    grp = 8
    per_sub = RW_HEAD // grp

    @pl.when(c == 0)
    def _():
        s_scr[...] = s0_ref[...]
        sb_scr[...] = s0_ref[...].astype(BF16)
        o_scr[...] = jnp.zeros_like(o_scr)

    ones = ones_ref[...]
    lane_pos = lax.broadcasted_iota(jnp.int32, (RW_HEAD, SCAN_LANES), 1) & (RW_HEAD - 1)
    chains = []
    for g in range(gb):
        chains.append((g, 0, rf_ref, vf_ref, kkf_ref, wf_ref, kdf_ref, bf_ref))
        chains.append((g, 1, rb_ref, vb_ref, kkb_ref, wb_ref, kdb_ref, bb_ref))

    hd = RW_HEAD

    tiles = [[(ch, pl.ds(j * SCAN_LANES, SCAN_LANES)) for ch in chains[c0:c0 + SCAN_CHAINS]]
             for c0 in range(0, len(chains), SCAN_CHAINS) for j in range(W_HALF // SCAN_LANES)]

    def put_outputs(streams, res_rows, step, base):
        for si, ((g, d, *_), ls) in enumerate(streams):
            pos = base + step if d == 0 else RW_HEAD - 1 - (base + step)
            o_scr[g, d, :, ls] = jnp.where(lane_pos == pos, res_rows[si * hd:(si + 1) * hd], o_scr[g, d, :, ls])

    def group(i, carry):
        sub = i // per_sub
        base = (i % per_sub) * grp
        row_f = pl.ds(pl.multiple_of(i * grp, grp), grp)
        row_b = pl.ds(pl.multiple_of(tc - grp - i * grp, grp), grp)

        def row(ref, g, d, ls, s, per_dir=False):
            q = s if d == 0 else grp - 1 - s
            tile = ref[0, g, row_b if d else row_f, ls] if per_dir else ref[g, row_b if d else row_f, ls]
            return tile[q:q + 1]

        pending = [None] * len(tiles)
        for s in range(grp):
            for ti, streams in enumerate(tiles):
                nst = len(streams)
                lhs = []
                for ((g, d, r_ref, v_ref, kk_ref, w_ref, kd_ref, b_ref), ls) in streams:
                    lhs.append(sb_scr[g, d, :, ls] * row(kk_ref, g, d, ls, s).astype(BF16))
                for ((g, d, r_ref, v_ref, kk_ref, w_ref, kd_ref, b_ref), ls) in streams:
                    lhs.append(eye_ref[:, ls] * row(v_ref, g, d, ls, s).astype(BF16))
                if pending[ti] is not None:
                    lhs += pending[ti]
                res = jnp.dot(jnp.concatenate(lhs, 0), ones, preferred_element_type=F32)
                if pending[ti] is not None:
                    put_outputs(streams, res[2 * nst * hd:], s - 1, base)
                nxt = []
                for si, ((g, d, r_ref, v_ref, kk_ref, w_ref, kd_ref, b_ref), ls) in enumerate(streams):
                    sa = res[si * hd:(si + 1) * hd]
                    vcol = res[(nst + si) * hd:(nst + si + 1) * hd]
                    st = (s_scr[g, d, :, ls] * row(w_ref, g, d, ls, s, True)
                          - sa * row(b_ref, g, d, ls, s, True) + vcol * row(kd_ref, g, d, ls, s, True))
                    s_scr[g, d, :, ls] = st
                    stb = st.astype(BF16)
                    sb_scr[g, d, :, ls] = stb
                    nxt.append(stb * row(r_ref, g, d, ls, s).astype(BF16))
                pending[ti] = nxt
        for ti, streams in enumerate(tiles):
            res = jnp.dot(jnp.concatenate(pending[ti], 0), ones, preferred_element_type=F32)
            put_outputs(streams, res, grp - 1, base)

        @pl.when(i % per_sub == per_sub - 1)
        def _():
            for g in range(gb):
                for d, (o_ref, blk) in enumerate(((of_ref, sub), (ob_ref, nsub - 1 - sub))):
                    ot = o_scr[g, d].T
                    for h in range(RW_HEADS):
                        o_ref[g, blk, :, h * hd:(h + 1) * hd] = ot[h * hd:(h + 1) * hd, :]

        return carry

    lax.fori_loop(0, tc // grp, group, 0)

    @pl.when(c == nc - 1)
    def _():
        fin_ref[...] = s_scr[...]


def _rwkv_scan(r, v, kk, w, kd, b, s0, ones2, eye, bsz, seq):
    tc = 64
    nc = seq // tc
    gb = min(bsz, SCAN_BATCH)
    r, v, kk = (t.reshape(bsz, seq, W_HALF) for t in (r, v, kk))
    w, kd, b = (t.reshape(2, bsz, seq, W_HALF) for t in (w, kd, b))
    shared = (gb, tc, W_HALF)
    perdir = (1, gb, tc, W_HALF)
    sf = lambda i, c: (i, c, 0)
    sb = lambda i, c: (i, nc - 1 - c, 0)
    pf = lambda i, c: (0, i, c, 0)
    pb = lambda i, c: (1, i, nc - 1 - c, 0)
    nsub = tc // RW_HEAD
    o_shape = jax.ShapeDtypeStruct((bsz, seq // RW_HEAD, RW_HEAD, W_HALF), F32)
    o_blk = (gb, nsub, RW_HEAD, W_HALF)
    st_blk = (gb, 2, RW_HEAD, W_HALF)
    return pl.pallas_call(
        functools.partial(_rwkv_scan_kernel, tc=tc, nc=nc, gb=gb),
        grid=(bsz // gb, nc),
        in_specs=[pl.BlockSpec(shared, sf)] * 3 + [pl.BlockSpec(perdir, pf)] * 3
                 + [pl.BlockSpec(shared, sb)] * 3 + [pl.BlockSpec(perdir, pb)] * 3
                 + [pl.BlockSpec(st_blk, lambda i, c: (i, 0, 0, 0)),
                    pl.BlockSpec(ones2.shape, lambda i, c: (0, 0)),
                    pl.BlockSpec(eye.shape, lambda i, c: (0, 0))],
        out_specs=[pl.BlockSpec(o_blk, lambda i, c: (i, c, 0, 0)),
                   pl.BlockSpec(o_blk, lambda i, c: (i, nc - 1 - c, 0, 0)),
                   pl.BlockSpec(st_blk, lambda i, c: (i, 0, 0, 0))],
        out_shape=[o_shape, o_shape, jax.ShapeDtypeStruct((bsz, 2, RW_HEAD, W_HALF), F32)],
        scratch_shapes=[pltpu.VMEM(st_blk, F32), pltpu.VMEM(st_blk, BF16), pltpu.VMEM(st_blk, F32)],
        compiler_params=_cparams(("parallel", "arbitrary")),
        name="rwkv_scan",
    )(r, v, kk, w, kd, b, r, v, kk, w, kd, b, s0, ones2, eye)


def _cd_out_kernel(x_ref, mod_ref, yc_ref, of_ref, ob_ref, bonus_ref, sg_ref, gng_ref, gnb_ref,
                   ones_ref, w_ref, g_ref, b_ref, o_ref):
    x = x_ref[...]
    m = mod_ref[0]
    ones = ones_ref[...]
    o = of_ref[...] + ob_ref[...]
    oc = o - _group_sum(o, ones) * (1.0 / RW_HEAD)
    on = oc * lax.rsqrt(_group_sum(oc * oc, ones) * (1.0 / RW_HEAD) + GN_EPS)
    yd = (on * gng_ref[...] + gnb_ref[...] + bonus_ref[...]) * sg_ref[...]
    y = _dot_bf16(yc_ref[...], w_ref[0:W_HALF, :]) + _dot_bf16(yd, w_ref[W_HALF:, :])
    o_ref[...] = _layer_norm(DN_ALPHA * x + m[2:3] * y, g_ref[...], b_ref[...])


def _cd_out(x, mods, yc, of, ob, bonus, sg, gng, gnb, ones, w_out, ln_g, ln_b):
    n = x.shape[0]
    tpb = n // mods.shape[0] // TM
    tok = lambda i: (i, 0)
    const2 = lambda i: (0, 0)
    half = pl.BlockSpec((TM, W_HALF), tok)
    vec = pl.BlockSpec((1, W_HALF), const2)
    return pl.pallas_call(
        _cd_out_kernel,
        grid=(n // TM,),
        in_specs=[pl.BlockSpec((TM, D_MODEL), tok),
                  pl.BlockSpec((1, 6, D_MODEL), lambda i: (i // tpb, 0, 0)),
                  half, half, half, half, half, vec, vec,
                  pl.BlockSpec(ones.shape, const2),
                  pl.BlockSpec(w_out.shape, const2),
                  pl.BlockSpec((1, D_MODEL), const2), pl.BlockSpec((1, D_MODEL), const2)],
        out_specs=pl.BlockSpec((TM, D_MODEL), tok),
        out_shape=jax.ShapeDtypeStruct((n, D_MODEL), F32),
        compiler_params=_cparams(("parallel",)),
        name="cd_out",
    )(x, mods, yc, of, ob, bonus, sg, gng, gnb, ones, w_out, ln_g, ln_b)


def _hyena_consts(seq):
    tn = np.linspace(0.0, 1.0, seq, dtype=np.float32)
    tr = np.arange(seq, dtype=np.float32)
    bands = np.linspace(1e-4, HY_BANDS - 1, HY_BANDS, dtype=np.float32)
    ang = np.float32(2.0 * math.pi / seq) * tr[:, None] * bands[None, :]
    feats = np.concatenate([tn[:, None], np.cos(ang), -np.sin(ang)], -1).astype(np.float32)
    feats = np.pad(feats, ((0, 0), (0, 40 - feats.shape[1])))
    deltas = np.abs(np.linspace(math.log(HY_TARGET) / HY_LONG_PCT, math.log(HY_TARGET) / HY_SHORT_PCT,
                                W_HALF, dtype=np.float32))
    decay = np.exp(-tn[:, None] * deltas[None, :]).astype(np.float32)
    lag = np.concatenate([np.arange(seq), [0], np.arange(seq - 1, 0, -1)])
    return feats[lag], decay[lag]


def _filt_kernel(f_ref, dec_ref, w1_ref, b1_ref, w2_ref, b2_ref, w3_ref, fr_ref, o_ref, ssq_ref, *, half):
    i = pl.program_id(0)
    fr = fr_ref[...]
    hid = jnp.sin(fr * (jnp.dot(f_ref[...], w1_ref[...], precision=HIGHEST,
                                preferred_element_type=F32) + b1_ref[...]))
    hid = jnp.sin(fr * (jnp.dot(hid, w2_ref[...], precision=HIGHEST,
                                preferred_element_type=F32) + b2_ref[...]))
    raw = jnp.dot(hid, w3_ref[...], precision=HIGHEST, preferred_element_type=F32)
    dec = dec_ref[...]
    rows = raw.shape[0]
    backward = i >= half
    middle = (lax.broadcasted_iota(jnp.int32, (rows, W_HALF), 0) == 0) & (i == half)
    parts = []
    for order in range(2):
        fwd = raw[:, (2 * order) * W_HALF:(2 * order + 1) * W_HALF]
        bwd = raw[:, (2 * order + 1) * W_HALF:(2 * order + 2) * W_HALF]
        f = jnp.where(middle, 0.0, jnp.where(backward, bwd, fwd) * dec)
        o_ref[order] = f
        parts.append(jnp.sum(f * f, 0, keepdims=True))
    ssq = jnp.concatenate(parts, 0)

    @pl.when(i == 0)
    def _():
        ssq_ref[...] = ssq

    @pl.when(i > 0)
    def _():
        ssq_ref[...] += ssq


def _hyena_filters(seq, w1, b1, w2, b2, w3, freq):
    feats, decay = _hyena_consts(seq)
    w1 = jnp.pad(w1, ((0, 40 - w1.shape[0]), (0, 0)))
    rows = 256
    const2 = lambda i: (0, 0)
    tok = lambda i: (i, 0)
    vec = lambda a: a.reshape(1, -1)
    args = [jnp.asarray(feats), jnp.asarray(decay), w1, vec(b1), w2, vec(b2), w3, vec(freq)]
    return pl.pallas_call(
        functools.partial(_filt_kernel, half=seq // rows),
        grid=(2 * seq // rows,),
        in_specs=[pl.BlockSpec((rows, 40), tok), pl.BlockSpec((rows, W_HALF), tok)]
                 + [pl.BlockSpec(a.shape, const2) for a in args[2:]],
        out_specs=[pl.BlockSpec((2, rows, W_HALF), lambda i: (0, i, 0)), pl.BlockSpec((2, W_HALF), const2)],
        out_shape=[jax.ShapeDtypeStruct((2, 2 * seq, W_HALF), F32),
                   jax.ShapeDtypeStruct((2, W_HALF), F32)],
        compiler_params=_cparams(("arbitrary",)),
        name="hyena_filters",
    )(*args)


def _dft_consts_two_stage(seq):
    n = 2 * seq
    p = DFT_P
    q = n // p
    a = np.arange(q)[None, :]
    k1 = np.arange(q)[:, None]
    ang1 = 2.0 * np.pi * a * k1 / q
    f1 = np.empty((2 * q, q))
    f1[0::2] = np.cos(ang1)
    f1[1::2] = -np.sin(ang1)
    f3 = np.empty((q, 2 * q))
    f3[:, 0::2] = np.cos(ang1).T / n
    f3[:, 1::2] = -np.sin(ang1).T / n
    qq = np.arange(p)[None, None, :]
    k2 = np.arange(p)[None, :, None]
    kk1 = np.arange(q)[:, None, None]
    ang = 2.0 * np.pi * (qq * k2 / p + qq * kk1 / n)
    gr, gi = np.cos(ang), -np.sin(ang)
    g = np.concatenate([np.concatenate([gr, -gi], 2), np.concatenate([gi, gr], 2)], 1)
    grt, git = np.swapaxes(gr, 1, 2), np.swapaxes(gi, 1, 2)
    ginv = np.concatenate([np.concatenate([grt, git], 2), np.concatenate([-git, grt], 2)], 1)
    half = q // 2
    return (f1.astype(np.float32), f1[:, :half].astype(np.float32), f3[:half].astype(np.float32),
            g.astype(np.float32), ginv.astype(np.float32))


def _dft_consts_one_stage(seq):
    n = 2 * seq
    t = np.arange(n)[None, :]
    f = np.arange(n)[:, None]
    ang = 2.0 * np.pi * t * f / n
    fwd = np.concatenate([np.cos(ang), -np.sin(ang)], 0)
    inv = np.concatenate([np.cos(ang), -np.sin(ang)], 1)[:seq] / n
    return fwd.astype(np.float32), fwd[:, :seq].astype(np.float32), inv.astype(np.float32)


def _expand_rows(f):
    return np.kron(f, np.eye(DFT_ROWS, dtype=f.dtype))


def _stage_kernel(f_ref, x_ref, o_ref, *, exact):
    k, rows, ch = x_ref.shape[1:]
    x = x_ref[0].reshape(k * rows, ch)
    if exact:
        res = jnp.dot(f_ref[...], x, precision=HIGHEST, preferred_element_type=F32)
    else:
        res = _dot_bf16(f_ref[...], x.astype(BF16))
    o_ref[0] = res.reshape(o_ref.shape[1:]).astype(o_ref.dtype)


def _stage(f, x, exact=False):
    bsz, k, p, ch = x.shape
    rr = f.shape[0]
    fx = jnp.asarray(_expand_rows(f), dtype=F32 if exact else BF16)
    return pl.pallas_call(
        functools.partial(_stage_kernel, exact=exact),
        grid=(bsz, p // DFT_ROWS),
        in_specs=[pl.BlockSpec(fx.shape, lambda b, j: (0, 0)),
                  pl.BlockSpec((1, k, DFT_ROWS, ch), lambda b, j: (b, 0, j, 0))],
        out_specs=pl.BlockSpec((1, rr, DFT_ROWS, ch), lambda b, j: (b, 0, j, 0)),
        out_shape=jax.ShapeDtypeStruct((bsz, rr, p, ch), F32 if exact else BF16),
        compiler_params=_cparams(("parallel", "parallel")),
        name="dft_stage",
    )(fx, x)


def _spectrum_kernel(a_ref, g_ref, ssq_ref, o_ref):
    x = jnp.dot(g_ref[0], a_ref[0, 0], precision=HIGHEST, preferred_element_type=F32)
    o_ref[0, 0] = x * lax.rsqrt(ssq_ref[0] + 1e-6)


def _spectrum(a, g, ssq):
    nb, q, k, ch = a.shape
    rr = g.shape[1]
    return pl.pallas_call(
        _spectrum_kernel,
        grid=(nb, q),
        in_specs=[pl.BlockSpec((1, 1, k, ch), lambda b, j: (b, j, 0, 0)),
                  pl.BlockSpec((1, rr, k), lambda b, j: (j, 0, 0)),
                  pl.BlockSpec((1, 1, ch), lambda b, j: (b, 0, 0))],
        out_specs=pl.BlockSpec((1, 1, rr, ch), lambda b, j: (b, j, 0, 0)),
        out_shape=jax.ShapeDtypeStruct((nb, q, rr, ch), F32),
        compiler_params=_cparams(("parallel", "parallel")),
        name="hyena_spectrum",
    )(a, g, ssq)


def _spectral_conv(a, g_ref, h_ref, gi_ref):
    x = _dot_bf16(g_ref[0], a.astype(BF16))
    half = x.shape[0] // 2
    xr, xi = x[:half], x[half:]
    hr, hi = h_ref[0, 0, :half], h_ref[0, 0, half:]
    y = jnp.concatenate([xr * hr - xi * hi, xr * hi + xi * hr], 0)
    return _dot_bf16(gi_ref[0], y.astype(BF16))


def _mid_kernel(a_ref, g_ref, h_ref, gi_ref, o_ref):
    for b in range(a_ref.shape[0]):
        o_ref[b, 0] = _spectral_conv(a_ref[b, 0], g_ref, h_ref, gi_ref).astype(o_ref.dtype)


def _mid_gate_kernel(a_ref, g_ref, h_ref, gi_ref, hx_ref, bias_ref, o_ref):
    for b in range(a_ref.shape[0]):
        a = a_ref[b, 0]
        o_ref[b, 0] = hx_ref[b, 0] * (_spectral_conv(a, g_ref, h_ref, gi_ref) + bias_ref[...] * a)


def _mid(a, g, h, gi, order, gate=None, out_dtype=F32):
    bsz, q, k, ch = a.shape
    rr = g.shape[1]
    ro = gi.shape[1]
    nb = min(bsz, MID_BATCH)
    act = lambda j, b: (b, j, 0, 0)
    in_specs = [pl.BlockSpec((nb, 1, k, ch), act),
                pl.BlockSpec((1, rr, k), lambda j, b: (j, 0, 0)),
                pl.BlockSpec((1, 1, rr, ch), lambda j, b: (order, j, 0, 0)),
                pl.BlockSpec((1, ro, rr), lambda j, b: (j, 0, 0))]
    args = [a, g, h, gi]
    kern = _mid_kernel
    if gate is not None:
        in_specs += [pl.BlockSpec((nb, 1, ro, ch), act), pl.BlockSpec((1, ch), lambda j, b: (0, 0))]
        args += list(gate)
        kern = _mid_gate_kernel
    return pl.pallas_call(
        kern,
        grid=(q, bsz // nb),
        in_specs=in_specs,
        out_specs=pl.BlockSpec((nb, 1, ro, ch), act),
        out_shape=jax.ShapeDtypeStruct((bsz, q, ro, ch), out_dtype),
        compiler_params=_cparams(("parallel", "parallel")),
        name="hyena_mid",
    )(*args)


def _last_stage_kernel(f3_ref, bt_ref, hx_ref, z_ref, bias_ref, o_ref):
    k, rows, ch = bt_ref.shape[1:]
    conv = _dot_bf16(f3_ref[...], bt_ref[0].reshape(k * rows, ch))
    flat = conv.shape
    out = hx_ref[0].reshape(flat) * (conv + bias_ref[...] * z_ref[0].reshape(flat))
    o_ref[0] = out.reshape(o_ref.shape[1:])


def _last_stage(f3, bt, hx, z, bias):
    bsz, k, p, ch = bt.shape
    rr = f3.shape[0]
    fx = jnp.asarray(_expand_rows(f3), dtype=BF16)
    act = pl.BlockSpec((1, rr, DFT_ROWS, ch), lambda b, j: (b, 0, j, 0))
    return pl.pallas_call(
        _last_stage_kernel,
        grid=(bsz, p // DFT_ROWS),
        in_specs=[pl.BlockSpec(fx.shape, lambda b, j: (0, 0)),
                  pl.BlockSpec((1, k, DFT_ROWS, ch), lambda b, j: (b, 0, j, 0)),
                  act, act,
                  pl.BlockSpec((1, ch), lambda b, j: (0, 0))],
        out_specs=act,
        out_shape=jax.ShapeDtypeStruct((bsz, rr, p, ch), F32),
        compiler_params=_cparams(("parallel", "parallel")),
        name="dft_last_stage",
    )(fx, bt, hx, z, bias)


def _hyena(hv, hx1, hx2, circ, ssq, bias, bsz, seq):
    ch = W_HALF
    ssq = ssq.reshape(2, 1, ch)
    if seq <= 512:
        fwd_full, fwd_half, inv = _dft_consts_one_stage(seq)
        spec = _spectrum(circ[:, None], jnp.asarray(fwd_full)[None], ssq)
        fwd_half = jnp.asarray(fwd_half, dtype=BF16)[None]
        inv = jnp.asarray(inv, dtype=BF16)[None]
        shape4 = (bsz, 1, seq, ch)
        z = _mid(hv.reshape(shape4), fwd_half, spec, inv, 0, gate=(hx1.reshape(shape4), bias[0:1]))
        y = _mid(z, fwd_half, spec, inv, 1, gate=(hx2.reshape(shape4), bias[1:2]))
        return y.reshape(bsz * seq, ch)
    f1_full, f1_half, f3, g, ginv = _dft_consts_two_stage(seq)
    p = DFT_P
    q = 2 * seq // p
    spec = _spectrum(_stage(f1_full, circ.reshape(2, q, p, ch), exact=True).reshape(2, q, 2 * p, ch),
                     jnp.asarray(g), ssq)
    g = jnp.asarray(g, dtype=BF16)
    ginv = jnp.asarray(ginv, dtype=BF16)
    nat = (bsz, q // 2, p, ch)
    out = hv.reshape(nat)
    for order, hx in enumerate((hx1, hx2)):
        a = _stage(f1_half, out).reshape(bsz, q, 2 * p, ch)
        bt = _mid(a, g, spec, ginv, order, out_dtype=BF16).reshape(bsz, 2 * q, p, ch)
        out = _last_stage(f3, bt, hx.reshape(nat), out, bias[order:order + 1])
    return out.reshape(bsz * seq, ch)


def _block_diag(w):
    hh, blk, _ = w.shape
    eye = jnp.eye(hh, dtype=w.dtype)
    return jnp.einsum('hij,hg->higj', w, eye).reshape(hh * blk, hh * blk)


def _group_ones(width):
    idx = np.arange(width) // RW_HEAD
    return jnp.asarray((idx[:, None] == idx[None, :]).astype(np.float32), dtype=BF16)


def _even_layer(x, mods, h0, bsz, seq, line, pw):
    ya, gg, a, u = _ab_in(x, mods, pw['w_in'], pw['sc_conv'], pw['lru_conv'], pw['lru_conv_b'],
                          pw['wg'], pw['bg'], pw['nsp'], line)
    hf, hb, fin = _lru_scan(a, u, h0, bsz, seq)
    x = _ab_out(x, mods, ya, gg, hf.reshape(-1, W_HALF), hb.reshape(-1, W_HALF),
                pw['w_out'], pw['ln1_g'], pw['ln1_b'])
    return x, fin


def _odd_layer(x, mods, s0, bsz, seq, line, pw):
    ones = _group_ones(W_HALF)
    (hv, hx1, hx2, r, v, kk, w, kd, b, bonus, sg) = _cd_in(
        x, mods, pw['w_in'], pw['hy_conv'], pw['rw_mu'], pw['rw_mu_x'], pw['l1'], pw['l2'], pw['l0'],
        pw['rw_kk'], pw['rw_ka'], pw['rw_rk'], ones, line)
    filt, ssq = _hyena_filters(seq, pw['hy_w1'], pw['hy_b1'], pw['hy_w2'], pw['hy_b2'], pw['hy_w3'],
                               pw['hy_freq'])
    yc = _hyena(hv, hx1, hx2, filt, ssq, pw['hy_bias'], bsz, seq)

    s0 = jnp.transpose(s0, (0, 1, 3, 2, 4)).reshape(bsz, 2, RW_HEAD, W_HALF)
    lane = np.arange(W_HALF)
    eye = jnp.asarray(((lane[None, :] % RW_HEAD) == np.arange(RW_HEAD)[:, None]).astype(np.float32),
                      dtype=BF16)
    of, ob, fin = _rwkv_scan(r, v, kk, w, kd, b, s0, _group_ones(SCAN_LANES), eye, bsz, seq)
    x = _cd_out(x, mods, yc, of.reshape(-1, W_HALF), ob.reshape(-1, W_HALF), bonus, sg,
                pw['rw_gn_g'], pw['rw_gn_b'], ones, pw['w_out'], pw['ln1_g'], pw['ln1_b'])
    fin = jnp.transpose(fin.reshape(bsz, 2, RW_HEAD, RW_HEADS, RW_HEAD), (0, 1, 3, 2, 4))
    return x, fin


def _layer_weights(p, l):
    j = l // 2
    row = lambda a: a.reshape(1, -1)
    pw = {'w_out': p['w_out'][l].astype(BF16), 'ln1_g': row(p['ln1_g'][l]), 'ln1_b': row(p['ln1_b'][l]),
          'ln2_g': row(p['ln2_g'][l]), 'ln2_b': row(p['ln2_b'][l]),
          'mlp_w1': p['mlp_w1'][l].astype(BF16), 'mlp_w2': p['mlp_w2'][l].astype(BF16)}
    if l % 2 == 0:
        gates = [_block_diag(p[name][j, d]) for d in range(2) for name in ('lru_wa', 'lru_wi')]
        biases = [p[name][j, d] for d in range(2) for name in ('lru_ba', 'lru_bi')]
        pw.update({'w_in': p['ab_w_in'][j].astype(BF16), 'sc_conv': p['sc_conv'][j],
                   'lru_conv': p['lru_conv'][j], 'lru_conv_b': row(p['lru_conv_b'][j]),
                   'wg': jnp.concatenate(gates, 1).astype(BF16), 'bg': row(jnp.concatenate(biases)),
                   'nsp': jax.nn.softplus(-p['lru_lambda'][j])})
    else:
        zeros = jnp.zeros((64, W_HALF), F32)
        l2 = jnp.concatenate([
            jnp.concatenate([p['rw_w2'][j, 0], zeros], 1), jnp.concatenate([zeros, p['rw_w2'][j, 1]], 1),
            jnp.concatenate([p['rw_a2'][j, 0], zeros], 1), jnp.concatenate([zeros, p['rw_a2'][j, 1]], 1)], 0)
        pw.update({'w_in': p['cd_w_in'][j].astype(BF16), 'hy_conv': p['hy_conv'][j],
                   'rw_mu': p['rw_mu'][j], 'rw_mu_x': p['rw_mu_x'][j],
                   'l1': jnp.concatenate([p['rw_w1'][j, 0], p['rw_w1'][j, 1],
                                          p['rw_a1'][j, 0], p['rw_a1'][j, 1]], 1).astype(BF16),
                   'l2': l2.astype(BF16),
                   'l0': jnp.stack([p['rw_w0'][j].reshape(-1), p['rw_a0'][j].reshape(-1)]),
                   'rw_kk': row(p['rw_kk'][j]), 'rw_ka': row(p['rw_ka'][j]), 'rw_rk': row(p['rw_rk'][j]),
                   'rw_gn_g': row(p['rw_gn_g'][j]), 'rw_gn_b': row(p['rw_gn_b'][j]),
                   'hy_w1': p['hy_w1'][j], 'hy_b1': p['hy_b1'][j], 'hy_w2': p['hy_w2'][j],
                   'hy_b2': p['hy_b2'][j], 'hy_w3': p['hy_w3'][j], 'hy_freq': p['hy_freq'][j],
                   'hy_bias': p['hy_bias'][j]})
    return pw


def _to_colmajor(x, bsz, rows):
    return jnp.transpose(x.reshape(bsz, rows, GRID_W, D_MODEL), (0, 2, 1, 3)).reshape(-1, D_MODEL)


def _from_colmajor(x, bsz, rows):
    return jnp.transpose(x.reshape(bsz, GRID_W, rows, D_MODEL), (0, 2, 1, 3)).reshape(-1, D_MODEL)


def _trunk(x, mods, init_lru, init_rwkv, rows, weights):
    bsz, seq, _ = x.shape
    x = x.reshape(bsz * seq, D_MODEL)
    new_lru, new_rwkv = [], []
    for l in range(DEPTH):
        j = l // 2
        pw = weights[l]
        if l % 2 == 0:
            line = seq if rows is None else GRID_W
            x, st = _even_layer(x, mods[l], init_lru[:, j], bsz, seq, line, pw)
            new_lru.append(st)
        else:
            if rows is None:
                x, st = _odd_layer(x, mods[l], init_rwkv[:, j], bsz, seq, seq, pw)
                x = _mlp(x, mods[l], pw['mlp_w1'], pw['mlp_w2'], pw['ln2_g'], pw['ln2_b'])
            else:
                x = _to_colmajor(x, bsz, rows)
                x, st = _odd_layer(x, mods[l], init_rwkv[:, j], bsz, seq, rows, pw)
                x = _mlp(x, mods[l], pw['mlp_w1'], pw['mlp_w2'], pw['ln2_g'], pw['ln2_b'])
                x = _from_colmajor(x, bsz, rows)
            new_rwkv.append(st)
            continue
        x = _mlp(x, mods[l], pw['mlp_w1'], pw['mlp_w2'], pw['ln2_g'], pw['ln2_b'])
    return x.reshape(bsz, seq, D_MODEL), jnp.stack(new_lru, 1), jnp.stack(new_rwkv, 1)


def kernel(x_prompt, x_sample, state_lru, state_rwkv, c, c_ctx, w_mod, b_mod, ln1_g, ln1_b, ln2_g, ln2_b, mlp_w1, mlp_w2, w_out, ab_w_in, sc_conv, lru_conv, lru_conv_b, lru_wa, lru_ba, lru_wi, lru_bi, lru_lambda, cd_w_in, hy_conv, hy_w1, hy_b1, hy_w2, hy_b2, hy_w3, hy_freq, hy_bias, rw_mu, rw_mu_x, rw_w0, rw_w1, rw_w2, rw_a0, rw_a1, rw_a2, rw_kk, rw_ka, rw_rk, rw_gn_g, rw_gn_b):
    p = dict(ln1_g=ln1_g, ln1_b=ln1_b, ln2_g=ln2_g, ln2_b=ln2_b,
             mlp_w1=mlp_w1, mlp_w2=mlp_w2, w_out=w_out, ab_w_in=ab_w_in, sc_conv=sc_conv,
             lru_conv=lru_conv, lru_conv_b=lru_conv_b, lru_wa=lru_wa, lru_ba=lru_ba, lru_wi=lru_wi,
             lru_bi=lru_bi, lru_lambda=lru_lambda, cd_w_in=cd_w_in, hy_conv=hy_conv, hy_w1=hy_w1,
             hy_b1=hy_b1, hy_w2=hy_w2, hy_b2=hy_b2, hy_w3=hy_w3, hy_freq=hy_freq, hy_bias=hy_bias,
             rw_mu=rw_mu, rw_mu_x=rw_mu_x, rw_w0=rw_w0, rw_w1=rw_w1, rw_w2=rw_w2, rw_a0=rw_a0,
             rw_a1=rw_a1, rw_a2=rw_a2, rw_kk=rw_kk, rw_ka=rw_ka, rw_rk=rw_rk, rw_gn_g=rw_gn_g,
             rw_gn_b=rw_gn_b)
    weights = [_layer_weights(p, l) for l in range(DEPTH)]
    nb, dec = x_prompt.shape[0], x_sample.shape[0]
    rpad = -(1 + dec) % 8
    cvec = jnp.concatenate([c_ctx[None, :], c, jnp.zeros((rpad, D_MODEL), F32)], 0)
    mods = _mods(cvec, w_mod, b_mod)
    mods = jnp.transpose(mods, (0, 2, 1, 3))
    zero_lru = jnp.zeros((nb, (DEPTH + 1) // 2, 2, W_HALF), F32)
    zero_rwkv = jnp.zeros((nb, DEPTH // 2, 2, RW_HEADS, RW_HEAD, RW_HEAD), F32)
    y_prompt, new_lru, new_rwkv = _trunk(x_prompt, mods[:, 0:1], zero_lru, zero_rwkv, None, weights)
    rows = x_sample.shape[1] // GRID_W
    y_sample, _, _ = _trunk(x_sample, mods[:, 1:1 + dec], state_lru, state_rwkv, rows, weights)
    return (y_prompt, y_sample, new_lru, new_rwkv)
```

```python
import functools
import math

import jax
import jax.numpy as jnp
import numpy as np
from jax import lax
from jax.experimental import pallas as pl
from jax.experimental.pallas import tpu as pltpu

F32 = jnp.float32
BF16 = jnp.bfloat16
HIGHEST = lax.Precision.HIGHEST

D_MODEL = 1024
DEPTH = 4
GRID_W = 64
W_HALF = D_MODEL // 2
LRU_HEADS = 8
RG_C = 8.0
HY_BANDS = 16
HY_TARGET = 1e-2
HY_SHORT_PCT = 0.3
HY_LONG_PCT = 1.5
RW_HEAD = 64
RW_HEADS = W_HALF // RW_HEAD
D_FF = 4 * D_MODEL
DN_ALPHA = (2 * DEPTH) ** 0.25
LN_EPS = 1e-5
GN_EPS = 64e-5

LANES = 128
TM = 256
VMEM_LIMIT = 56 * 1024 * 1024
DFT_P = 128
DFT_ROWS = 16
MID_BATCH = 8
FF_CHUNK = 1024
LRU_BATCH = 4
SCAN_LANES = 256
SCAN_BATCH = 8
SCAN_CHAINS = 8


def _cparams(sem):
    return pltpu.CompilerParams(dimension_semantics=sem, vmem_limit_bytes=VMEM_LIMIT)


def _shift_rows(x, off, line):
    if off == 0:
        return x
    n, width = x.shape
    rolled = pltpu.roll(x, (-off) % n, 0)
    pos = lax.broadcasted_iota(jnp.int32, (n, LANES), 0) & (line - 1)
    valid = (pos + off >= 0) if off < 0 else (pos + off < line)
    return jnp.concatenate([jnp.where(valid, rolled[:, j:j + LANES], 0.0) for j in range(0, width, LANES)], 1)


def _dwconv_rows(x, w, pad_left, line):
    out = None
    for k in range(w.shape[0]):
        term = _shift_rows(x, k - pad_left, line) * w[k:k + 1]
        out = term if out is None else out + term
    return out


def _tshift_rows(x, line):
    return 0.5 * (_shift_rows(x, -1, line) + _shift_rows(x, 1, line))


def _layer_norm(v, g, b):
    mu = jnp.mean(v, -1, keepdims=True)
    vc = v - mu
    var = jnp.mean(vc * vc, -1, keepdims=True)
    return vc * lax.rsqrt(var + LN_EPS) * g + b


def _dot_bf16(a, b):
    return jnp.dot(a.astype(BF16), b, preferred_element_type=F32)


def _dot_f32_lhs(a, b):
    return lax.dot_general(a, b, (((1,), (0,)), ((), ())), preferred_element_type=F32)


def _group_sum(x, ones):
    hi = x.astype(BF16)
    lo = (x - hi.astype(F32)).astype(BF16)
    return (jnp.dot(hi, ones, preferred_element_type=F32)
            + jnp.dot(lo, ones, preferred_element_type=F32))


def _softplus(x):
    return jnp.maximum(x, 0.0) + jnp.log(1.0 + jnp.exp(-jnp.abs(x)))


def _sigmoid(x):
    return 1.0 / (1.0 + jnp.exp(-x))


def _mods_kernel(c_ref, w_ref, b_ref, o_ref):
    c = c_ref[...]
    s = c * _sigmoid(c)
    o_ref[0, 0] = jnp.dot(s, w_ref[0], precision=HIGHEST, preferred_element_type=F32) + b_ref[0, 0]


def _mods(cvec, w_mod, b_mod):
    r = cvec.shape[0]
    return pl.pallas_call(
        _mods_kernel,
        grid=(DEPTH, 6),
        in_specs=[pl.BlockSpec((r, D_MODEL), lambda l, n: (0, 0)),
                  pl.BlockSpec((1, D_MODEL, D_MODEL), lambda l, n: (l, 0, n)),
                  pl.BlockSpec((1, 1, 1, D_MODEL), lambda l, n: (l, n, 0, 0))],
        out_specs=pl.BlockSpec((1, 1, r, D_MODEL), lambda l, n: (l, n, 0, 0)),
        out_shape=jax.ShapeDtypeStruct((DEPTH, 6, r, D_MODEL), F32),
        compiler_params=_cparams(("parallel", "parallel")),
        name="mods",
    )(cvec, w_mod, b_mod.reshape(DEPTH, 6, 1, D_MODEL))


def _ab_in_kernel(x_ref, mod_ref, w_ref, scw_ref, lcw_ref, lcb_ref, wg_ref, bg_ref, nsp_ref,
                  ya_ref, gg_ref, a_ref, u_ref, *, line):
    x = x_ref[...]
    m = mod_ref[0]
    h = x * (1.0 + m[1:2]) + m[0:1]
    proj = _dot_bf16(h, w_ref[...])
    w = W_HALF
    s_b, s_c, s_v = proj[:, 0:w], proj[:, w:2 * w], proj[:, 2 * w:3 * w]
    g_lru, x_lru = proj[:, 3 * w:4 * w], proj[:, 4 * w:5 * w]
    ya_ref[...] = s_b * _dwconv_rows(s_c * s_v, scw_ref[...], 1, line)
    gg_ref[...] = 0.5 * g_lru * (1.0 + jnp.tanh(
        math.sqrt(2.0 / math.pi) * (g_lru + 0.044715 * (g_lru * g_lru * g_lru))))
    xc = _dwconv_rows(x_lru, lcw_ref[...], 2, line) + lcb_ref[...]
    gates = _dot_bf16(xc, wg_ref[...]) + bg_ref[...]
    for d in range(2):
        r = _sigmoid(gates[:, (2 * d) * w:(2 * d + 1) * w])
        i = _sigmoid(gates[:, (2 * d + 1) * w:(2 * d + 2) * w])
        a = jnp.exp(-RG_C * r * nsp_ref[d:d + 1])
        a_ref[d] = a
        u_ref[d] = jnp.sqrt(1.0 - a * a) * (i * xc)


def _ab_in(x, mods, w_in, scw, lcw, lcb, wg, bg, nsp, line):
    n = x.shape[0]
    tpb = n // mods.shape[0] // TM
    tok = lambda i: (i, 0)
    const2 = lambda i: (0, 0)
    half = jax.ShapeDtypeStruct((n, W_HALF), F32)
    both = jax.ShapeDtypeStruct((2, n, W_HALF), F32)
    return pl.pallas_call(
        functools.partial(_ab_in_kernel, line=line),
        grid=(n // TM,),
        in_specs=[pl.BlockSpec((TM, D_MODEL), tok),
                  pl.BlockSpec((1, 6, D_MODEL), lambda i: (i // tpb, 0, 0)),
                  pl.BlockSpec(w_in.shape, const2),
                  pl.BlockSpec(scw.shape, const2),
                  pl.BlockSpec(lcw.shape, const2),
                  pl.BlockSpec(lcb.shape, const2),
                  pl.BlockSpec(wg.shape, const2),
                  pl.BlockSpec(bg.shape, const2),
                  pl.BlockSpec(nsp.shape, const2)],
        out_specs=[pl.BlockSpec((TM, W_HALF), tok), pl.BlockSpec((TM, W_HALF), tok),
                   pl.BlockSpec((2, TM, W_HALF), lambda i: (0, i, 0)),
                   pl.BlockSpec((2, TM, W_HALF), lambda i: (0, i, 0))],
        out_shape=[half, half, both, both],
        compiler_params=_cparams(("parallel",)),
        name="ab_in",
    )(x, mods, w_in, scw, lcw, lcb, wg, bg, nsp)


def _lru_scan_kernel(af_ref, uf_ref, ab_ref, ub_ref, h0_ref, hf_ref, hb_ref, fin_ref, carry, *, tc, nc):
    c = pl.program_id(1)

    nb = af_ref.shape[1]

    @pl.when(c == 0)
    def _():
        carry[...] = h0_ref[...]

    def step(s, hs):
        tb = tc - 1 - s
        out = []
        for g in range(nb):
            hf, hb = hs[2 * g], hs[2 * g + 1]
            hf = af_ref[0, g, pl.ds(s, 1), :] * hf + uf_ref[0, g, pl.ds(s, 1), :]
            hb = ab_ref[0, g, pl.ds(tb, 1), :] * hb + ub_ref[0, g, pl.ds(tb, 1), :]
            hf_ref[g, pl.ds(s, 1), :] = hf
            hb_ref[g, pl.ds(tb, 1), :] = hb
            out += [hf, hb]
        return tuple(out)

    init = tuple(carry[g, d:d + 1] for g in range(nb) for d in range(2))
    hs = lax.fori_loop(0, tc, step, init, unroll=8)
    for g in range(nb):
        for d in range(2):
            carry[g, d:d + 1] = hs[2 * g + d]

    @pl.when(c == nc - 1)
    def _():
        fin_ref[...] = carry[...]


def _lru_scan(a, u, h0, bsz, seq):
    tc = min(seq, 512)
    nc = seq // tc
    nb = min(bsz, LRU_BATCH)
    a = a.reshape(2, bsz, seq, W_HALF)
    u = u.reshape(2, bsz, seq, W_HALF)
    fwd = lambda b, c: (0, b, c, 0)
    bwd = lambda b, c: (1, b, nc - 1 - c, 0)
    blk = (1, nb, tc, W_HALF)
    seq_shape = jax.ShapeDtypeStruct((bsz, seq, W_HALF), F32)
    return pl.pallas_call(
        functools.partial(_lru_scan_kernel, tc=tc, nc=nc),
        grid=(bsz // nb, nc),
        in_specs=[pl.BlockSpec(blk, fwd), pl.BlockSpec(blk, fwd),
                  pl.BlockSpec(blk, bwd), pl.BlockSpec(blk, bwd),
                  pl.BlockSpec((nb, 2, W_HALF), lambda b, c: (b, 0, 0))],
        out_specs=[pl.BlockSpec((nb, tc, W_HALF), lambda b, c: (b, c, 0)),
                   pl.BlockSpec((nb, tc, W_HALF), lambda b, c: (b, nc - 1 - c, 0)),
                   pl.BlockSpec((nb, 2, W_HALF), lambda b, c: (b, 0, 0))],
        out_shape=[seq_shape, seq_shape, jax.ShapeDtypeStruct((bsz, 2, W_HALF), F32)],
        scratch_shapes=[pltpu.VMEM((nb, 2, W_HALF), F32)],
        compiler_params=_cparams(("parallel", "arbitrary")),
        name="lru_scan",
    )(a, u, a, u, h0)


def _ab_out_kernel(x_ref, mod_ref, ya_ref, gg_ref, hf_ref, hb_ref, w_ref, g_ref, b_ref, o_ref):
    x = x_ref[...]
    m = mod_ref[0]
    yb = gg_ref[...] * (hf_ref[...] + hb_ref[...])
    y = _dot_bf16(ya_ref[...], w_ref[0:W_HALF, :]) + _dot_bf16(yb, w_ref[W_HALF:, :])
    o_ref[...] = _layer_norm(DN_ALPHA * x + m[2:3] * y, g_ref[...], b_ref[...])


def _ab_out(x, mods, ya, gg, hf, hb, w_out, ln_g, ln_b):
    n = x.shape[0]
    tpb = n // mods.shape[0] // TM
    tok = lambda i: (i, 0)
    const2 = lambda i: (0, 0)
    half = pl.BlockSpec((TM, W_HALF), tok)
    return pl.pallas_call(
        _ab_out_kernel,
        grid=(n // TM,),
        in_specs=[pl.BlockSpec((TM, D_MODEL), tok),
                  pl.BlockSpec((1, 6, D_MODEL), lambda i: (i // tpb, 0, 0)),
                  half, half, half, half,
                  pl.BlockSpec(w_out.shape, const2),
                  pl.BlockSpec((1, D_MODEL), const2), pl.BlockSpec((1, D_MODEL), const2)],
        out_specs=pl.BlockSpec((TM, D_MODEL), tok),
        out_shape=jax.ShapeDtypeStruct((n, D_MODEL), F32),
        compiler_params=_cparams(("parallel",)),
        name="ab_out",
    )(x, mods, ya, gg, hf, hb, w_out, ln_g, ln_b)


def _mlp_kernel(x_ref, mod_ref, w1_ref, w2_ref, g_ref, b_ref, o_ref):
    x = x_ref[...]
    m = mod_ref[0]
    h = (x * (1.0 + m[4:5]) + m[3:4]).astype(BF16)
    y = None
    for c in range(D_FF // FF_CHUNK):
        t = jnp.dot(h, w1_ref[:, c * FF_CHUNK:(c + 1) * FF_CHUNK], preferred_element_type=F32)
        t = jnp.maximum(t, 0.0)
        part = _dot_bf16(t * t, w2_ref[c * FF_CHUNK:(c + 1) * FF_CHUNK, :])
        y = part if y is None else y + part
    o_ref[...] = _layer_norm(DN_ALPHA * x + m[5:6] * y, g_ref[...], b_ref[...])


def _mlp(x, mods, w1, w2, ln_g, ln_b):
    n = x.shape[0]
    tpb = n // mods.shape[0] // TM
    tok = lambda i: (i, 0)
    const2 = lambda i: (0, 0)
    return pl.pallas_call(
        _mlp_kernel,
        grid=(n // TM,),
        in_specs=[pl.BlockSpec((TM, D_MODEL), tok),
                  pl.BlockSpec((1, 6, D_MODEL), lambda i: (i // tpb, 0, 0)),
                  pl.BlockSpec(w1.shape, const2, pipeline_mode=pl.Buffered(1)),
                  pl.BlockSpec(w2.shape, const2, pipeline_mode=pl.Buffered(1)),
                  pl.BlockSpec((1, D_MODEL), const2), pl.BlockSpec((1, D_MODEL), const2)],
        out_specs=pl.BlockSpec((TM, D_MODEL), tok),
        out_shape=jax.ShapeDtypeStruct((n, D_MODEL), F32),
        compiler_params=_cparams(("parallel",)),
        name="mlp",
    )(x, mods, w1, w2, ln_g, ln_b)


def _cd_in_kernel(x_ref, mod_ref, w_ref, hyc_ref, mu_ref, mux_ref, l1_ref, l2_ref, l0_ref,
                  kkp_ref, ka_ref, rk_ref, ones_ref,
                  hv_ref, hx1_ref, hx2_ref, r_ref, v_ref, kk_ref, w_out_ref, kd_ref, b_ref,
                  bonus_ref, sg_ref, *, line):
    x = x_ref[...]
    m = mod_ref[0]
    h = x * (1.0 + m[1:2]) + m[0:1]
    proj = _dot_bf16(h, w_ref[...])
    w = W_HALF
    u = _dwconv_rows(proj[:, 0:3 * w], hyc_ref[...], 1, line)
    hv_ref[...] = u[:, 0:w]
    hx1_ref[...] = u[:, w:2 * w]
    hx2_ref[...] = u[:, 2 * w:3 * w]

    mixed = []
    for n in range(4):
        t = proj[:, (3 + n) * w:(4 + n) * w]
        mixed.append(t + (_tshift_rows(t, line) - t) * mu_ref[n:n + 1])
    r, k, v, g = mixed
    dh = _tshift_rows(h, line) - h
    xw = h + dh * mux_ref[0:1]
    xa = h + dh * mux_ref[1:2]
    tw = jnp.tanh(_dot_bf16(xw, l1_ref[:, 0:128]))
    ta = _dot_bf16(xa, l1_ref[:, 128:256])
    zw = _dot_bf16(tw, l2_ref[0:128, :]) + l0_ref[0:1]
    za = _dot_bf16(ta, l2_ref[128:256, :]) + l0_ref[1:2]

    ones = ones_ref[...]
    kk = k * kkp_ref[...]
    kk = kk * lax.rsqrt(_group_sum(kk * kk, ones) + 1e-12)
    r_ref[...] = r
    v_ref[...] = v
    kk_ref[...] = kk
    kd_sum = None
    for d in range(2):
        w_raw = -_softplus(-zw[:, d * w:(d + 1) * w]) - 0.5
        w_out_ref[d] = jnp.exp(-jnp.exp(w_raw))
        a = _sigmoid(za[:, d * w:(d + 1) * w])
        kd = k * (1.0 + (a - 1.0) * ka_ref[...])
        kd_ref[d] = kd
        b_ref[d] = kk * a
        kd_sum = kd if kd_sum is None else kd_sum + kd
    bonus_ref[...] = _group_sum(r * kd_sum * rk_ref[...], ones) * v
    sg_ref[...] = _sigmoid(g)


def _cd_in(x, mods, w_in, hyc, mu, mux, l1, l2, l0, kkp, ka, rk, ones, line):
    n = x.shape[0]
    tpb = n // mods.shape[0] // TM
    tok = lambda i: (i, 0)
    const2 = lambda i: (0, 0)
    half = jax.ShapeDtypeStruct((n, W_HALF), F32)
    both = jax.ShapeDtypeStruct((2, n, W_HALF), F32)
    half_spec = pl.BlockSpec((TM, W_HALF), tok)
    both_spec = pl.BlockSpec((2, TM, W_HALF), lambda i: (0, i, 0))
    consts = [w_in, hyc, mu, mux, l1, l2, l0, kkp, ka, rk, ones]
    return pl.pallas_call(
        functools.partial(_cd_in_kernel, line=line),
        grid=(n // TM,),
        in_specs=[pl.BlockSpec((TM, D_MODEL), tok),
                  pl.BlockSpec((1, 6, D_MODEL), lambda i: (i // tpb, 0, 0))]
                 + [pl.BlockSpec(c.shape, const2) for c in consts],
        out_specs=[half_spec] * 6 + [both_spec] * 3 + [half_spec] * 2,
        out_shape=[half] * 6 + [both] * 3 + [half] * 2,
        compiler_params=_cparams(("parallel",)),
        name="cd_in",
    )(x, mods, *consts)


def _rwkv_scan_kernel(rf_ref, vf_ref, kkf_ref, wf_ref, kdf_ref, bf_ref,
                      rb_ref, vb_ref, kkb_ref, wb_ref, kdb_ref, bb_ref,
                      s0_ref, ones_ref, eye_ref,
                      of_ref, ob_ref, fin_ref, s_scr, sb_scr, o_scr, *, tc, nc, gb):
    c = pl.program_id(1)
    nsub = tc // RW_HEAD
    grp = 8
    per_sub = RW_HEAD // grp

    @pl.when(c == 0)
    def _():
        s_scr[...] = s0_ref[...]
        sb_scr[...] = s0_ref[...].astype(BF16)
        o_scr[...] = jnp.zeros_like(o_scr)

    ones = ones_ref[...]
    lane_pos = lax.broadcasted_iota(jnp.int32, (RW_HEAD, SCAN_LANES), 1) & (RW_HEAD - 1)
    chains = []
    for g in range(gb):
        chains.append((g, 0, rf_ref, vf_ref, kkf_ref, wf_ref, kdf_ref, bf_ref))
        chains.append((g, 1, rb_ref, vb_ref, kkb_ref, wb_ref, kdb_ref, bb_ref))

    hd = RW_HEAD

    tiles = [[(ch, pl.ds(j * SCAN_LANES, SCAN_LANES)) for ch in chains[c0:c0 + SCAN_CHAINS]]
             for c0 in range(0, len(chains), SCAN_CHAINS) for j in range(W_HALF // SCAN_LANES)]

    def put_outputs(streams, res_rows, step, base):
        for si, ((g, d, *_), ls) in enumerate(streams):
            pos = base + step if d == 0 else RW_HEAD - 1 - (base + step)
            o_scr[g, d, :, ls] = jnp.where(lane_pos == pos, res_rows[si * hd:(si + 1) * hd], o_scr[g, d, :, ls])

    def group(i, carry):
        sub = i // per_sub
        base = (i % per_sub) * grp
        row_f = pl.ds(pl.multiple_of(i * grp, grp), grp)
        row_b = pl.ds(pl.multiple_of(tc - grp - i * grp, grp), grp)

        def row(ref, g, d, ls, s, per_dir=False):
            q = s if d == 0 else grp - 1 - s
            tile = ref[0, g, row_b if d else row_f, ls] if per_dir else ref[g, row_b if d else row_f, ls]
            return tile[q:q + 1]

        pending = [None] * len(tiles)
        for s in range(grp):
            for ti, streams in enumerate(tiles):
                nst = len(streams)
                lhs = []
                for ((g, d, r_ref, v_ref, kk_ref, w_ref, kd_ref, b_ref), ls) in streams:
                    lhs.append(sb_scr[g, d, :, ls] * row(kk_ref, g, d, ls, s).astype(BF16))
                for ((g, d, r_ref, v_ref, kk_ref, w_ref, kd_ref, b_ref), ls) in streams:
                    lhs.append(eye_ref[:, ls] * row(v_ref, g, d, ls, s).astype(BF16))
                if pending[ti] is not None:
                    lhs += pending[ti]
                res = jnp.dot(jnp.concatenate(lhs, 0), ones, preferred_element_type=F32)
                if pending[ti] is not None:
                    put_outputs(streams, res[2 * nst * hd:], s - 1, base)
                nxt = []
                for si, ((g, d, r_ref, v_ref, kk_ref, w_ref, kd_ref, b_ref), ls) in enumerate(streams):
                    sa = res[si * hd:(si + 1) * hd]
                    vcol = res[(nst + si) * hd:(nst + si + 1) * hd]
                    st = (s_scr[g, d, :, ls] * row(w_ref, g, d, ls, s, True)
                          - sa * row(b_ref, g, d, ls, s, True) + vcol * row(kd_ref, g, d, ls, s, True))
                    s_scr[g, d, :, ls] = st
                    stb = st.astype(BF16)
                    sb_scr[g, d, :, ls] = stb
                    nxt.append(stb * row(r_ref, g, d, ls, s).astype(BF16))
                pending[ti] = nxt
        for ti, streams in enumerate(tiles):
            res = jnp.dot(jnp.concatenate(pending[ti], 0), ones, preferred_element_type=F32)
            put_outputs(streams, res, grp - 1, base)

        @pl.when(i % per_sub == per_sub - 1)
        def _():
            for g in range(gb):
                for d, (o_ref, blk) in enumerate(((of_ref, sub), (ob_ref, nsub - 1 - sub))):
                    ot = o_scr[g, d].T
                    for h in range(RW_HEADS):
                        o_ref[g, blk, :, h * hd:(h + 1) * hd] = ot[h * hd:(h + 1) * hd, :]

        return carry

    lax.fori_loop(0, tc // grp, group, 0)

    @pl.when(c == nc - 1)
    def _():
        fin_ref[...] = s_scr[...]


def _rwkv_scan(r, v, kk, w, kd, b, s0, ones2, eye, bsz, seq):
    tc = 64
    nc = seq // tc
    gb = min(bsz, SCAN_BATCH)
    r, v, kk = (t.reshape(bsz, seq, W_HALF) for t in (r, v, kk))
    w, kd, b = (t.reshape(2, bsz, seq, W_HALF) for t in (w, kd, b))
    shared = (gb, tc, W_HALF)
    perdir = (1, gb, tc, W_HALF)
    sf = lambda i, c: (i, c, 0)
    sb = lambda i, c: (i, nc - 1 - c, 0)
    pf = lambda i, c: (0, i, c, 0)
    pb = lambda i, c: (1, i, nc - 1 - c, 0)
    nsub = tc // RW_HEAD
    o_shape = jax.ShapeDtypeStruct((bsz, seq // RW_HEAD, RW_HEAD, W_HALF), F32)
    o_blk = (gb, nsub, RW_HEAD, W_HALF)
    st_blk = (gb, 2, RW_HEAD, W_HALF)
    return pl.pallas_call(
        functools.partial(_rwkv_scan_kernel, tc=tc, nc=nc, gb=gb),
        grid=(bsz // gb, nc),
        in_specs=[pl.BlockSpec(shared, sf)] * 3 + [pl.BlockSpec(perdir, pf)] * 3
                 + [pl.BlockSpec(shared, sb)] * 3 + [pl.BlockSpec(perdir, pb)] * 3
                 + [pl.BlockSpec(st_blk, lambda i, c: (i, 0, 0, 0)),
                    pl.BlockSpec(ones2.shape, lambda i, c: (0, 0)),
                    pl.BlockSpec(eye.shape, lambda i, c: (0, 0))],
        out_specs=[pl.BlockSpec(o_blk, lambda i, c: (i, c, 0, 0)),
                   pl.BlockSpec(o_blk, lambda i, c: (i, nc - 1 - c, 0, 0)),
                   pl.BlockSpec(st_blk, lambda i, c: (i, 0, 0, 0))],
        out_shape=[o_shape, o_shape, jax.ShapeDtypeStruct((bsz, 2, RW_HEAD, W_HALF), F32)],
        scratch_shapes=[pltpu.VMEM(st_blk, F32), pltpu.VMEM(st_blk, BF16), pltpu.VMEM(st_blk, F32)],
        compiler_params=_cparams(("parallel", "arbitrary")),
        name="rwkv_scan",
    )(r, v, kk, w, kd, b, r, v, kk, w, kd, b, s0, ones2, eye)


def _cd_out_kernel(x_ref, mod_ref, yc_ref, of_ref, ob_ref, bonus_ref, sg_ref, gng_ref, gnb_ref,
                   ones_ref, w_ref, g_ref, b_ref, o_ref):
    x = x_ref[...]
    m = mod_ref[0]
    ones = ones_ref[...]
    o = of_ref[...] + ob_ref[...]
    oc = o - _group_sum(o, ones) * (1.0 / RW_HEAD)
    on = oc * lax.rsqrt(_group_sum(oc * oc, ones) * (1.0 / RW_HEAD) + GN_EPS)
    yd = (on * gng_ref[...] + gnb_ref[...] + bonus_ref[...]) * sg_ref[...]
    y = _dot_bf16(yc_ref[...], w_ref[0:W_HALF, :]) + _dot_bf16(yd, w_ref[W_HALF:, :])
    o_ref[...] = _layer_norm(DN_ALPHA * x + m[2:3] * y, g_ref[...], b_ref[...])


def _cd_out(x, mods, yc, of, ob, bonus, sg, gng, gnb, ones, w_out, ln_g, ln_b):
    n = x.shape[0]
    tpb = n // mods.shape[0] // TM
    tok = lambda i: (i, 0)
    const2 = lambda i: (0, 0)
    half = pl.BlockSpec((TM, W_HALF), tok)
    vec = pl.BlockSpec((1, W_HALF), const2)
    return pl.pallas_call(
        _cd_out_kernel,
        grid=(n // TM,),
        in_specs=[pl.BlockSpec((TM, D_MODEL), tok),
                  pl.BlockSpec((1, 6, D_MODEL), lambda i: (i // tpb, 0, 0)),
                  half, half, half, half, half, vec, vec,
                  pl.BlockSpec(ones.shape, const2),
                  pl.BlockSpec(w_out.shape, const2),
                  pl.BlockSpec((1, D_MODEL), const2), pl.BlockSpec((1, D_MODEL), const2)],
        out_specs=pl.BlockSpec((TM, D_MODEL), tok),
        out_shape=jax.ShapeDtypeStruct((n, D_MODEL), F32),
        compiler_params=_cparams(("parallel",)),
        name="cd_out",
    )(x, mods, yc, of, ob, bonus, sg, gng, gnb, ones, w_out, ln_g, ln_b)


def _hyena_consts(seq):
    tn = np.linspace(0.0, 1.0, seq, dtype=np.float32)
    tr = np.arange(seq, dtype=np.float32)
    bands = np.linspace(1e-4, HY_BANDS - 1, HY_BANDS, dtype=np.float32)
    ang = np.float32(2.0 * math.pi / seq) * tr[:, None] * bands[None, :]
    feats = np.concatenate([tn[:, None], np.cos(ang), -np.sin(ang)], -1).astype(np.float32)
    feats = np.pad(feats, ((0, 0), (0, 40 - feats.shape[1])))
    deltas = np.abs(np.linspace(math.log(HY_TARGET) / HY_LONG_PCT, math.log(HY_TARGET) / HY_SHORT_PCT,
                                W_HALF, dtype=np.float32))
    decay = np.exp(-tn[:, None] * deltas[None, :]).astype(np.float32)
    lag = np.concatenate([np.arange(seq), [0], np.arange(seq - 1, 0, -1)])
    return feats[lag], decay[lag]


def _filt_kernel(f_ref, dec_ref, w1_ref, b1_ref, w2_ref, b2_ref, w3_ref, fr_ref, o_ref, ssq_ref, *, half):
    i = pl.program_id(0)
    fr = fr_ref[...]
    hid = jnp.sin(fr * (jnp.dot(f_ref[...], w1_ref[...], precision=HIGHEST,
                                preferred_element_type=F32) + b1_ref[...]))
    hid = jnp.sin(fr * (jnp.dot(hid, w2_ref[...], precision=HIGHEST,
                                preferred_element_type=F32) + b2_ref[...]))
    raw = jnp.dot(hid, w3_ref[...], precision=HIGHEST, preferred_element_type=F32)
    dec = dec_ref[...]
    rows = raw.shape[0]
    backward = i >= half
    middle = (lax.broadcasted_iota(jnp.int32, (rows, W_HALF), 0) == 0) & (i == half)
    parts = []
    for order in range(2):
        fwd = raw[:, (2 * order) * W_HALF:(2 * order + 1) * W_HALF]
        bwd = raw[:, (2 * order + 1) * W_HALF:(2 * order + 2) * W_HALF]
        f = jnp.where(middle, 0.0, jnp.where(backward, bwd, fwd) * dec)
        o_ref[order] = f
        parts.append(jnp.sum(f * f, 0, keepdims=True))
    ssq = jnp.concatenate(parts, 0)

    @pl.when(i == 0)
    def _():
        ssq_ref[...] = ssq

    @pl.when(i > 0)
    def _():
        ssq_ref[...] += ssq


def _hyena_filters(seq, w1, b1, w2, b2, w3, freq):
    feats, decay = _hyena_consts(seq)
    w1 = jnp.pad(w1, ((0, 40 - w1.shape[0]), (0, 0)))
    rows = 256
    const2 = lambda i: (0, 0)
    tok = lambda i: (i, 0)
    vec = lambda a: a.reshape(1, -1)
    args = [jnp.asarray(feats), jnp.asarray(decay), w1, vec(b1), w2, vec(b2), w3, vec(freq)]
    return pl.pallas_call(
        functools.partial(_filt_kernel, half=seq // rows),
        grid=(2 * seq // rows,),
        in_specs=[pl.BlockSpec((rows, 40), tok), pl.BlockSpec((rows, W_HALF), tok)]
                 + [pl.BlockSpec(a.shape, const2) for a in args[2:]],
        out_specs=[pl.BlockSpec((2, rows, W_HALF), lambda i: (0, i, 0)), pl.BlockSpec((2, W_HALF), const2)],
        out_shape=[jax.ShapeDtypeStruct((2, 2 * seq, W_HALF), F32),
                   jax.ShapeDtypeStruct((2, W_HALF), F32)],
        compiler_params=_cparams(("arbitrary",)),
        name="hyena_filters",
    )(*args)


def _dft_consts_two_stage(seq):
    n = 2 * seq
    p = DFT_P
    q = n // p
    a = np.arange(q)[None, :]
    k1 = np.arange(q)[:, None]
    ang1 = 2.0 * np.pi * a * k1 / q
    f1 = np.empty((2 * q, q))
    f1[0::2] = np.cos(ang1)
    f1[1::2] = -np.sin(ang1)
    f3 = np.empty((q, 2 * q))
    f3[:, 0::2] = np.cos(ang1).T / n
    f3[:, 1::2] = -np.sin(ang1).T / n
    qq = np.arange(p)[None, None, :]
    k2 = np.arange(p)[None, :, None]
    kk1 = np.arange(q)[:, None, None]
    ang = 2.0 * np.pi * (qq * k2 / p + qq * kk1 / n)
    gr, gi = np.cos(ang), -np.sin(ang)
    g = np.concatenate([np.concatenate([gr, -gi], 2), np.concatenate([gi, gr], 2)], 1)
    grt, git = np.swapaxes(gr, 1, 2), np.swapaxes(gi, 1, 2)
    ginv = np.concatenate([np.concatenate([grt, git], 2), np.concatenate([-git, grt], 2)], 1)
    half = q // 2
    return (f1.astype(np.float32), f1[:, :half].astype(np.float32), f3[:half].astype(np.float32),
            g.astype(np.float32), ginv.astype(np.float32))


def _dft_consts_one_stage(seq):
    n = 2 * seq
    t = np.arange(n)[None, :]
    f = np.arange(n)[:, None]
    ang = 2.0 * np.pi * t * f / n
    fwd = np.concatenate([np.cos(ang), -np.sin(ang)], 0)
    inv = np.concatenate([np.cos(ang), -np.sin(ang)], 1)[:seq] / n
    return fwd.astype(np.float32), fwd[:, :seq].astype(np.float32), inv.astype(np.float32)


def _expand_rows(f, rows=DFT_ROWS):
    return np.kron(f, np.eye(rows, dtype=f.dtype))


def _stage_kernel(f_ref, x_ref, o_ref, *, exact):
    k, rows, ch = x_ref.shape[1:]
    x = x_ref[0].reshape(k * rows, ch)
    if exact:
        res = jnp.dot(f_ref[...], x, precision=HIGHEST, preferred_element_type=F32)
    else:
        res = _dot_bf16(f_ref[...], x.astype(BF16))
    o_ref[0] = res.reshape(o_ref.shape[1:]).astype(o_ref.dtype)


def _stage(f, x, exact=False):
    bsz, k, p, ch = x.shape
    rr = f.shape[0]
    rows = DFT_ROWS // 2 if exact else DFT_ROWS
    fx = jnp.asarray(_expand_rows(f, rows), dtype=F32 if exact else BF16)
    return pl.pallas_call(
        functools.partial(_stage_kernel, exact=exact),
        grid=(bsz, p // rows),
        in_specs=[pl.BlockSpec(fx.shape, lambda b, j: (0, 0)),
                  pl.BlockSpec((1, k, rows, ch), lambda b, j: (b, 0, j, 0))],
        out_specs=pl.BlockSpec((1, rr, rows, ch), lambda b, j: (b, 0, j, 0)),
        out_shape=jax.ShapeDtypeStruct((bsz, rr, p, ch), F32 if exact else BF16),
        compiler_params=_cparams(("parallel", "parallel")),
        name="dft_stage",
    )(fx, x)


def _spectrum_kernel(a_ref, g_ref, ssq_ref, o_ref):
    x = jnp.dot(g_ref[0], a_ref[0, 0], precision=HIGHEST, preferred_element_type=F32)
    o_ref[0, 0] = x * lax.rsqrt(ssq_ref[0] + 1e-6)


def _spectrum(a, g, ssq):
    nb, q, k, ch = a.shape
    rr = g.shape[1]
    return pl.pallas_call(
        _spectrum_kernel,
        grid=(nb, q),
        in_specs=[pl.BlockSpec((1, 1, k, ch), lambda b, j: (b, j, 0, 0)),
                  pl.BlockSpec((1, rr, k), lambda b, j: (j, 0, 0)),
                  pl.BlockSpec((1, 1, ch), lambda b, j: (b, 0, 0))],
        out_specs=pl.BlockSpec((1, 1, rr, ch), lambda b, j: (b, j, 0, 0)),
        out_shape=jax.ShapeDtypeStruct((nb, q, rr, ch), F32),
        compiler_params=_cparams(("parallel", "parallel")),
        name="hyena_spectrum",
    )(a, g, ssq)


def _spectral_conv(a, g_ref, h_ref, gi_ref):
    x = _dot_bf16(g_ref[0], a.astype(BF16))
    half = x.shape[0] // 2
    xr, xi = x[:half], x[half:]
    hr, hi = h_ref[0, 0, :half], h_ref[0, 0, half:]
    y = jnp.concatenate([xr * hr - xi * hi, xr * hi + xi * hr], 0)
    return _dot_bf16(gi_ref[0], y.astype(BF16))


def _mid_kernel(a_ref, g_ref, h_ref, gi_ref, o_ref):
    for b in range(a_ref.shape[0]):
        o_ref[b, 0] = _spectral_conv(a_ref[b, 0], g_ref, h_ref, gi_ref).astype(o_ref.dtype)


def _mid_gate_kernel(a_ref, g_ref, h_ref, gi_ref, hx_ref, bias_ref, o_ref):
    for b in range(a_ref.shape[0]):
        a = a_ref[b, 0]
        o_ref[b, 0] = hx_ref[b, 0] * (_spectral_conv(a, g_ref, h_ref, gi_ref) + bias_ref[...] * a)


def _mid(a, g, h, gi, order, gate=None, out_dtype=F32):
    bsz, q, k, ch = a.shape
    rr = g.shape[1]
    ro = gi.shape[1]
    nb = min(bsz, MID_BATCH)
    act = lambda j, b: (b, j, 0, 0)
    in_specs = [pl.BlockSpec((nb, 1, k, ch), act),
                pl.BlockSpec((1, rr, k), lambda j, b: (j, 0, 0)),
                pl.BlockSpec((1, 1, rr, ch), lambda j, b: (order, j, 0, 0)),
                pl.BlockSpec((1, ro, rr), lambda j, b: (j, 0, 0))]
    args = [a, g, h, gi]
    kern = _mid_kernel
    if gate is not None:
        in_specs += [pl.BlockSpec((nb, 1, ro, ch), act), pl.BlockSpec((1, ch), lambda j, b: (0, 0))]
        args += list(gate)
        kern = _mid_gate_kernel
    return pl.pallas_call(
        kern,
        grid=(q, bsz // nb),
        in_specs=in_specs,
        out_specs=pl.BlockSpec((nb, 1, ro, ch), act),
        out_shape=jax.ShapeDtypeStruct((bsz, q, ro, ch), out_dtype),
        compiler_params=_cparams(("parallel", "parallel")),
        name="hyena_mid",
    )(*args)


def _last_stage_kernel(f3_ref, bt_ref, hx_ref, z_ref, bias_ref, o_ref):
    k, rows, ch = bt_ref.shape[1:]
    conv = _dot_bf16(f3_ref[...], bt_ref[0].reshape(k * rows, ch))
    flat = conv.shape
    out = hx_ref[0].reshape(flat) * (conv + bias_ref[...] * z_ref[0].reshape(flat))
    o_ref[0] = out.reshape(o_ref.shape[1:])


def _last_stage(f3, bt, hx, z, bias):
    bsz, k, p, ch = bt.shape
    rr = f3.shape[0]
    fx = jnp.asarray(_expand_rows(f3), dtype=BF16)
    act = pl.BlockSpec((1, rr, DFT_ROWS, ch), lambda b, j: (b, 0, j, 0))
    return pl.pallas_call(
        _last_stage_kernel,
        grid=(bsz, p // DFT_ROWS),
        in_specs=[pl.BlockSpec(fx.shape, lambda b, j: (0, 0)),
                  pl.BlockSpec((1, k, DFT_ROWS, ch), lambda b, j: (b, 0, j, 0)),
                  act, act,
                  pl.BlockSpec((1, ch), lambda b, j: (0, 0))],
        out_specs=act,
        out_shape=jax.ShapeDtypeStruct((bsz, rr, p, ch), F32),
        compiler_params=_cparams(("parallel", "parallel")),
        name="dft_last_stage",
    )(fx, bt, hx, z, bias)


def _hyena(hv, hx1, hx2, circ, ssq, bias, bsz, seq):
    ch = W_HALF
    ssq = ssq.reshape(2, 1, ch)
    if seq <= 512:
        fwd_full, fwd_half, inv = _dft_consts_one_stage(seq)
        spec = _spectrum(circ[:, None], jnp.asarray(fwd_full)[None], ssq)
        fwd_half = jnp.asarray(fwd_half, dtype=BF16)[None]
        inv = jnp.asarray(inv, dtype=BF16)[None]
        shape4 = (bsz, 1, seq, ch)
        z = _mid(hv.reshape(shape4), fwd_half, spec, inv, 0, gate=(hx1.reshape(shape4), bias[0:1]))
        y = _mid(z, fwd_half, spec, inv, 1, gate=(hx2.reshape(shape4), bias[1:2]))
        return y.reshape(bsz * seq, ch)
    f1_full, f1_half, f3, g, ginv = _dft_consts_two_stage(seq)
    p = DFT_P
    q = 2 * seq // p
    spec = _spectrum(_stage(f1_full, circ.reshape(2, q, p, ch), exact=True).reshape(2, q, 2 * p, ch),
                     jnp.asarray(g), ssq)
    g = jnp.asarray(g, dtype=BF16)
    ginv = jnp.asarray(ginv, dtype=BF16)
    nat = (bsz, q // 2, p, ch)
    out = hv.reshape(nat)
    for order, hx in enumerate((hx1, hx2)):
        a = _stage(f1_half, out).reshape(bsz, q, 2 * p, ch)
        bt = _mid(a, g, spec, ginv, order, out_dtype=BF16).reshape(bsz, 2 * q, p, ch)
        out = _last_stage(f3, bt, hx.reshape(nat), out, bias[order:order + 1])
    return out.reshape(bsz * seq, ch)


def _block_diag(w):
    hh, blk, _ = w.shape
    eye = jnp.eye(hh, dtype=w.dtype)
    return jnp.einsum('hij,hg->higj', w, eye).reshape(hh * blk, hh * blk)


def _group_ones(width):
    idx = np.arange(width) // RW_HEAD
    return jnp.asarray((idx[:, None] == idx[None, :]).astype(np.float32), dtype=BF16)


def _even_layer(x, mods, h0, bsz, seq, line, pw):
    ya, gg, a, u = _ab_in(x, mods, pw['w_in'], pw['sc_conv'], pw['lru_conv'], pw['lru_conv_b'],
                          pw['wg'], pw['bg'], pw['nsp'], line)
    hf, hb, fin = _lru_scan(a, u, h0, bsz, seq)
    x = _ab_out(x, mods, ya, gg, hf.reshape(-1, W_HALF), hb.reshape(-1, W_HALF),
                pw['w_out'], pw['ln1_g'], pw['ln1_b'])
    return x, fin


def _odd_layer(x, mods, s0, bsz, seq, line, pw):
    ones = _group_ones(W_HALF)
    (hv, hx1, hx2, r, v, kk, w, kd, b, bonus, sg) = _cd_in(
        x, mods, pw['w_in'], pw['hy_conv'], pw['rw_mu'], pw['rw_mu_x'], pw['l1'], pw['l2'], pw['l0'],
        pw['rw_kk'], pw['rw_ka'], pw['rw_rk'], ones, line)
    filt, ssq = _hyena_filters(seq, pw['hy_w1'], pw['hy_b1'], pw['hy_w2'], pw['hy_b2'], pw['hy_w3'],
                               pw['hy_freq'])
    yc = _hyena(hv, hx1, hx2, filt, ssq, pw['hy_bias'], bsz, seq)

    s0 = jnp.transpose(s0, (0, 1, 3, 2, 4)).reshape(bsz, 2, RW_HEAD, W_HALF)
    lane = np.arange(W_HALF)
    eye = jnp.asarray(((lane[None, :] % RW_HEAD) == np.arange(RW_HEAD)[:, None]).astype(np.float32),
                      dtype=BF16)
    of, ob, fin = _rwkv_scan(r, v, kk, w, kd, b, s0, _group_ones(SCAN_LANES), eye, bsz, seq)
    x = _cd_out(x, mods, yc, of.reshape(-1, W_HALF), ob.reshape(-1, W_HALF), bonus, sg,
                pw['rw_gn_g'], pw['rw_gn_b'], ones, pw['w_out'], pw['ln1_g'], pw['ln1_b'])
    fin = jnp.transpose(fin.reshape(bsz, 2, RW_HEAD, RW_HEADS, RW_HEAD), (0, 1, 3, 2, 4))
    return x, fin


def _layer_weights(p, l):
    j = l // 2
    row = lambda a: a.reshape(1, -1)
    pw = {'w_out': p['w_out'][l].astype(BF16), 'ln1_g': row(p['ln1_g'][l]), 'ln1_b': row(p['ln1_b'][l]),
          'ln2_g': row(p['ln2_g'][l]), 'ln2_b': row(p['ln2_b'][l]),
          'mlp_w1': p['mlp_w1'][l].astype(BF16), 'mlp_w2': p['mlp_w2'][l].astype(BF16)}
    if l % 2 == 0:
        gates = [_block_diag(p[name][j, d]) for d in range(2) for name in ('lru_wa', 'lru_wi')]
        biases = [p[name][j, d] for d in range(2) for name in ('lru_ba', 'lru_bi')]
        pw.update({'w_in': p['ab_w_in'][j].astype(BF16), 'sc_conv': p['sc_conv'][j],
                   'lru_conv': p['lru_conv'][j], 'lru_conv_b': row(p['lru_conv_b'][j]),
                   'wg': jnp.concatenate(gates, 1).astype(BF16), 'bg': row(jnp.concatenate(biases)),
                   'nsp': jax.nn.softplus(-p['lru_lambda'][j])})
    else:
        zeros = jnp.zeros((64, W_HALF), F32)
        l2 = jnp.concatenate([
            jnp.concatenate([p['rw_w2'][j, 0], zeros], 1), jnp.concatenate([zeros, p['rw_w2'][j, 1]], 1),
            jnp.concatenate([p['rw_a2'][j, 0], zeros], 1), jnp.concatenate([zeros, p['rw_a2'][j, 1]], 1)], 0)
        pw.update({'w_in': p['cd_w_in'][j].astype(BF16), 'hy_conv': p['hy_conv'][j],
                   'rw_mu': p['rw_mu'][j], 'rw_mu_x': p['rw_mu_x'][j],
                   'l1': jnp.concatenate([p['rw_w1'][j, 0], p['rw_w1'][j, 1],
                                          p['rw_a1'][j, 0], p['rw_a1'][j, 1]], 1).astype(BF16),
                   'l2': l2.astype(BF16),
                   'l0': jnp.stack([p['rw_w0'][j].reshape(-1), p['rw_a0'][j].reshape(-1)]),
                   'rw_kk': row(p['rw_kk'][j]), 'rw_ka': row(p['rw_ka'][j]), 'rw_rk': row(p['rw_rk'][j]),
                   'rw_gn_g': row(p['rw_gn_g'][j]), 'rw_gn_b': row(p['rw_gn_b'][j]),
                   'hy_w1': p['hy_w1'][j], 'hy_b1': p['hy_b1'][j], 'hy_w2': p['hy_w2'][j],
                   'hy_b2': p['hy_b2'][j], 'hy_w3': p['hy_w3'][j], 'hy_freq': p['hy_freq'][j],
                   'hy_bias': p['hy_bias'][j]})
    return pw


def _to_colmajor(x, bsz, rows):
    return jnp.transpose(x.reshape(bsz, rows, GRID_W, D_MODEL), (0, 2, 1, 3)).reshape(-1, D_MODEL)


def _from_colmajor(x, bsz, rows):
    return jnp.transpose(x.reshape(bsz, GRID_W, rows, D_MODEL), (0, 2, 1, 3)).reshape(-1, D_MODEL)


def _trunk(x, mods, init_lru, init_rwkv, rows, weights):
    bsz, seq, _ = x.shape
    x = x.reshape(bsz * seq, D_MODEL)
    new_lru, new_rwkv = [], []
    for l in range(DEPTH):
        j = l // 2
        pw = weights[l]
        if l % 2 == 0:
            line = seq if rows is None else GRID_W
            x, st = _even_layer(x, mods[l], init_lru[:, j], bsz, seq, line, pw)
            new_lru.append(st)
        else:
            if rows is None:
                x, st = _odd_layer(x, mods[l], init_rwkv[:, j], bsz, seq, seq, pw)
                x = _mlp(x, mods[l], pw['mlp_w1'], pw['mlp_w2'], pw['ln2_g'], pw['ln2_b'])
            else:
                x = _to_colmajor(x, bsz, rows)
                x, st = _odd_layer(x, mods[l], init_rwkv[:, j], bsz, seq, rows, pw)
                x = _mlp(x, mods[l], pw['mlp_w1'], pw['mlp_w2'], pw['ln2_g'], pw['ln2_b'])
                x = _from_colmajor(x, bsz, rows)
            new_rwkv.append(st)
            continue
        x = _mlp(x, mods[l], pw['mlp_w1'], pw['mlp_w2'], pw['ln2_g'], pw['ln2_b'])
    return x.reshape(bsz, seq, D_MODEL), jnp.stack(new_lru, 1), jnp.stack(new_rwkv, 1)


def kernel(x_prompt, x_sample, state_lru, state_rwkv, c, c_ctx, w_mod, b_mod, ln1_g, ln1_b, ln2_g, ln2_b, mlp_w1, mlp_w2, w_out, ab_w_in, sc_conv, lru_conv, lru_conv_b, lru_wa, lru_ba, lru_wi, lru_bi, lru_lambda, cd_w_in, hy_conv, hy_w1, hy_b1, hy_w2, hy_b2, hy_w3, hy_freq, hy_bias, rw_mu, rw_mu_x, rw_w0, rw_w1, rw_w2, rw_a0, rw_a1, rw_a2, rw_kk, rw_ka, rw_rk, rw_gn_g, rw_gn_b):
    p = dict(ln1_g=ln1_g, ln1_b=ln1_b, ln2_g=ln2_g, ln2_b=ln2_b,
             mlp_w1=mlp_w1, mlp_w2=mlp_w2, w_out=w_out, ab_w_in=ab_w_in, sc_conv=sc_conv,
             lru_conv=lru_conv, lru_conv_b=lru_conv_b, lru_wa=lru_wa, lru_ba=lru_ba, lru_wi=lru_wi,
             lru_bi=lru_bi, lru_lambda=lru_lambda, cd_w_in=cd_w_in, hy_conv=hy_conv, hy_w1=hy_w1,
             hy_b1=hy_b1, hy_w2=hy_w2, hy_b2=hy_b2, hy_w3=hy_w3, hy_freq=hy_freq, hy_bias=hy_bias,
             rw_mu=rw_mu, rw_mu_x=rw_mu_x, rw_w0=rw_w0, rw_w1=rw_w1, rw_w2=rw_w2, rw_a0=rw_a0,
             rw_a1=rw_a1, rw_a2=rw_a2, rw_kk=rw_kk, rw_ka=rw_ka, rw_rk=rw_rk, rw_gn_g=rw_gn_g,
             rw_gn_b=rw_gn_b)
    weights = [_layer_weights(p, l) for l in range(DEPTH)]
    nb, dec = x_prompt.shape[0], x_sample.shape[0]
    rpad = -(1 + dec) % 8
    cvec = jnp.concatenate([c_ctx[None, :], c, jnp.zeros((rpad, D_MODEL), F32)], 0)
    mods = _mods(cvec, w_mod, b_mod)
    mods = jnp.transpose(mods, (0, 2, 1, 3))
    zero_lru = jnp.zeros((nb, (DEPTH + 1) // 2, 2, W_HALF), F32)
    zero_rwkv = jnp.zeros((nb, DEPTH // 2, 2, RW_HEADS, RW_HEAD, RW_HEAD), F32)
    y_prompt, new_lru, new_rwkv = _trunk(x_prompt, mods[:, 0:1], zero_lru, zero_rwkv, None, weights)
    rows = x_sample.shape[1] // GRID_W
    y_sample, _, _ = _trunk(x_sample, mods[:, 1:1 + dec], state_lru, state_rwkv, rows, weights)
    return (y_prompt, y_sample, new_lru, new_rwkv)
```

```python
import functools
import math

import jax
import jax.numpy as jnp
import numpy as np
from jax import lax
from jax.experimental import pallas as pl
from jax.experimental.pallas import tpu as pltpu

F32 = jnp.float32
BF16 = jnp.bfloat16
HIGHEST = lax.Precision.HIGHEST

D_MODEL = 1024
DEPTH = 4
GRID_W = 64
W_HALF = D_MODEL // 2
LRU_HEADS = 8
RG_C = 8.0
HY_BANDS = 16
HY_TARGET = 1e-2
HY_SHORT_PCT = 0.3
HY_LONG_PCT = 1.5
RW_HEAD = 64
RW_HEADS = W_HALF // RW_HEAD
D_FF = 4 * D_MODEL
DN_ALPHA = (2 * DEPTH) ** 0.25
LN_EPS = 1e-5
GN_EPS = 64e-5

LANES = 128
TM = 256
VMEM_LIMIT = 56 * 1024 * 1024
DFT_P = 128
DFT_ROWS = 16
MID_BATCH = 8
FF_CHUNK = 1024
LRU_BATCH = 4
SCAN_LANES = 256
SCAN_BATCH = 8
SCAN_CHAINS = 8


def _cparams(sem):
    return pltpu.CompilerParams(dimension_semantics=sem, vmem_limit_bytes=VMEM_LIMIT)


def _shift_rows(x, off, line):
    if off == 0:
        return x
    n, width = x.shape
    rolled = pltpu.roll(x, (-off) % n, 0)
    pos = lax.broadcasted_iota(jnp.int32, (n, LANES), 0) & (line - 1)
    valid = (pos + off >= 0) if off < 0 else (pos + off < line)
    return jnp.concatenate([jnp.where(valid, rolled[:, j:j + LANES], 0.0) for j in range(0, width, LANES)], 1)


def _dwconv_rows(x, w, pad_left, line):
    out = None
    for k in range(w.shape[0]):
        term = _shift_rows(x, k - pad_left, line) * w[k:k + 1]
        out = term if out is None else out + term
    return out


def _tshift_rows(x, line):
    return 0.5 * (_shift_rows(x, -1, line) + _shift_rows(x, 1, line))


def _layer_norm(v, g, b):
    mu = jnp.mean(v, -1, keepdims=True)
    vc = v - mu
    var = jnp.mean(vc * vc, -1, keepdims=True)
    return vc * lax.rsqrt(var + LN_EPS) * g + b


def _dot_bf16(a, b):
    return jnp.dot(a.astype(BF16), b, preferred_element_type=F32)


def _dot_f32_lhs(a, b):
    return lax.dot_general(a, b, (((1,), (0,)), ((), ())), preferred_element_type=F32)


def _group_sum(x, ones):
    hi = x.astype(BF16)
    lo = (x - hi.astype(F32)).astype(BF16)
    return (jnp.dot(hi, ones, preferred_element_type=F32)
            + jnp.dot(lo, ones, preferred_element_type=F32))


def _softplus(x):
    return jnp.maximum(x, 0.0) + jnp.log(1.0 + jnp.exp(-jnp.abs(x)))


def _sigmoid(x):
    return 1.0 / (1.0 + jnp.exp(-x))


def _mods_kernel(c_ref, w_ref, b_ref, o_ref):
    c = c_ref[...]
    s = c * _sigmoid(c)
    o_ref[0, 0] = jnp.dot(s, w_ref[0], precision=HIGHEST, preferred_element_type=F32) + b_ref[0, 0]


def _mods(cvec, w_mod, b_mod):
    r = cvec.shape[0]
    return pl.pallas_call(
        _mods_kernel,
        grid=(DEPTH, 6),
        in_specs=[pl.BlockSpec((r, D_MODEL), lambda l, n: (0, 0)),
                  pl.BlockSpec((1, D_MODEL, D_MODEL), lambda l, n: (l, 0, n)),
                  pl.BlockSpec((1, 1, 1, D_MODEL), lambda l, n: (l, n, 0, 0))],
        out_specs=pl.BlockSpec((1, 1, r, D_MODEL), lambda l, n: (l, n, 0, 0)),
        out_shape=jax.ShapeDtypeStruct((DEPTH, 6, r, D_MODEL), F32),
        compiler_params=_cparams(("parallel", "parallel")),
        name="mods",
    )(cvec, w_mod, b_mod.reshape(DEPTH, 6, 1, D_MODEL))


def _ab_in_kernel(x_ref, mod_ref, w_ref, scw_ref, lcw_ref, lcb_ref, wg_ref, bg_ref, nsp_ref,
                  ya_ref, gg_ref, a_ref, u_ref, *, line):
    x = x_ref[...]
    m = mod_ref[0]
    h = x * (1.0 + m[1:2]) + m[0:1]
    proj = _dot_bf16(h, w_ref[...])
    w = W_HALF
    s_b, s_c, s_v = proj[:, 0:w], proj[:, w:2 * w], proj[:, 2 * w:3 * w]
    g_lru, x_lru = proj[:, 3 * w:4 * w], proj[:, 4 * w:5 * w]
    ya_ref[...] = s_b * _dwconv_rows(s_c * s_v, scw_ref[...], 1, line)
    gg_ref[...] = 0.5 * g_lru * (1.0 + jnp.tanh(
        math.sqrt(2.0 / math.pi) * (g_lru + 0.044715 * (g_lru * g_lru * g_lru))))
    xc = _dwconv_rows(x_lru, lcw_ref[...], 2, line) + lcb_ref[...]
    gates = _dot_bf16(xc, wg_ref[...]) + bg_ref[...]
    for d in range(2):
        r = _sigmoid(gates[:, (2 * d) * w:(2 * d + 1) * w])
        i = _sigmoid(gates[:, (2 * d + 1) * w:(2 * d + 2) * w])
        a = jnp.exp(-RG_C * r * nsp_ref[d:d + 1])
        a_ref[d] = a
        u_ref[d] = jnp.sqrt(1.0 - a * a) * (i * xc)


def _ab_in(x, mods, w_in, scw, lcw, lcb, wg, bg, nsp, line):
    n = x.shape[0]
    tpb = n // mods.shape[0] // TM
    tok = lambda i: (i, 0)
    const2 = lambda i: (0, 0)
    half = jax.ShapeDtypeStruct((n, W_HALF), F32)
    both = jax.ShapeDtypeStruct((2, n, W_HALF), F32)
    return pl.pallas_call(
        functools.partial(_ab_in_kernel, line=line),
        grid=(n // TM,),
        in_specs=[pl.BlockSpec((TM, D_MODEL), tok),
                  pl.BlockSpec((1, 6, D_MODEL), lambda i: (i // tpb, 0, 0)),
                  pl.BlockSpec(w_in.shape, const2),
                  pl.BlockSpec(scw.shape, const2),
                  pl.BlockSpec(lcw.shape, const2),
                  pl.BlockSpec(lcb.shape, const2),
                  pl.BlockSpec(wg.shape, const2),
                  pl.BlockSpec(bg.shape, const2),
                  pl.BlockSpec(nsp.shape, const2)],
        out_specs=[pl.BlockSpec((TM, W_HALF), tok), pl.BlockSpec((TM, W_HALF), tok),
                   pl.BlockSpec((2, TM, W_HALF), lambda i: (0, i, 0)),
                   pl.BlockSpec((2, TM, W_HALF), lambda i: (0, i, 0))],
        out_shape=[half, half, both, both],
        compiler_params=_cparams(("parallel",)),
        name="ab_in",
    )(x, mods, w_in, scw, lcw, lcb, wg, bg, nsp)


def _lru_scan_kernel(af_ref, uf_ref, ab_ref, ub_ref, h0_ref, hf_ref, hb_ref, fin_ref, carry, *, tc, nc):
    c = pl.program_id(1)

    nb = af_ref.shape[1]

    @pl.when(c == 0)
    def _():
        carry[...] = h0_ref[...]

    def step(s, hs):
        tb = tc - 1 - s
        out = []
        for g in range(nb):
            hf, hb = hs[2 * g], hs[2 * g + 1]
            hf = af_ref[0, g, pl.ds(s, 1), :] * hf + uf_ref[0, g, pl.ds(s, 1), :]
            hb = ab_ref[0, g, pl.ds(tb, 1), :] * hb + ub_ref[0, g, pl.ds(tb, 1), :]
            hf_ref[g, pl.ds(s, 1), :] = hf
            hb_ref[g, pl.ds(tb, 1), :] = hb
            out += [hf, hb]
        return tuple(out)

    init = tuple(carry[g, d:d + 1] for g in range(nb) for d in range(2))
    hs = lax.fori_loop(0, tc, step, init, unroll=8)
    for g in range(nb):
        for d in range(2):
            carry[g, d:d + 1] = hs[2 * g + d]

    @pl.when(c == nc - 1)
    def _():
        fin_ref[...] = carry[...]


def _lru_scan(a, u, h0, bsz, seq):
    tc = min(seq, 512)
    nc = seq // tc
    nb = min(bsz, LRU_BATCH)
    a = a.reshape(2, bsz, seq, W_HALF)
    u = u.reshape(2, bsz, seq, W_HALF)
    fwd = lambda b, c: (0, b, c, 0)
    bwd = lambda b, c: (1, b, nc - 1 - c, 0)
    blk = (1, nb, tc, W_HALF)
    seq_shape = jax.ShapeDtypeStruct((bsz, seq, W_HALF), F32)
    return pl.pallas_call(
        functools.partial(_lru_scan_kernel, tc=tc, nc=nc),
        grid=(bsz // nb, nc),
        in_specs=[pl.BlockSpec(blk, fwd), pl.BlockSpec(blk, fwd),
                  pl.BlockSpec(blk, bwd), pl.BlockSpec(blk, bwd),
                  pl.BlockSpec((nb, 2, W_HALF), lambda b, c: (b, 0, 0))],
        out_specs=[pl.BlockSpec((nb, tc, W_HALF), lambda b, c: (b, c, 0)),
                   pl.BlockSpec((nb, tc, W_HALF), lambda b, c: (b, nc - 1 - c, 0)),
                   pl.BlockSpec((nb, 2, W_HALF), lambda b, c: (b, 0, 0))],
        out_shape=[seq_shape, seq_shape, jax.ShapeDtypeStruct((bsz, 2, W_HALF), F32)],
        scratch_shapes=[pltpu.VMEM((nb, 2, W_HALF), F32)],
        compiler_params=_cparams(("parallel", "arbitrary")),
        name="lru_scan",
    )(a, u, a, u, h0)


def _sublayers_tail(x, m, y, tail_refs):
    g1_ref, b1_ref, w1_ref, w2_ref, g2_ref, b2_ref = tail_refs
    x = _layer_norm(DN_ALPHA * x + m[2:3] * y, g1_ref[...], b1_ref[...])
    h = (x * (1.0 + m[4:5]) + m[3:4]).astype(BF16)
    y = None
    for c in range(D_FF // FF_CHUNK):
        t = jnp.dot(h, w1_ref[:, c * FF_CHUNK:(c + 1) * FF_CHUNK], preferred_element_type=F32)
        t = jnp.maximum(t, 0.0)
        part = _dot_bf16(t * t, w2_ref[c * FF_CHUNK:(c + 1) * FF_CHUNK, :])
        y = part if y is None else y + part
    return _layer_norm(DN_ALPHA * x + m[5:6] * y, g2_ref[...], b2_ref[...])


def _tail_specs(pw):
    const2 = lambda i: (0, 0)
    vec = pl.BlockSpec((1, D_MODEL), const2)
    args = [pw['ln1_g'], pw['ln1_b'], pw['mlp_w1'], pw['mlp_w2'], pw['ln2_g'], pw['ln2_b']]
    specs = [vec, vec,
             pl.BlockSpec(pw['mlp_w1'].shape, const2, pipeline_mode=pl.Buffered(1)),
             pl.BlockSpec(pw['mlp_w2'].shape, const2, pipeline_mode=pl.Buffered(1)),
             vec, vec]
    return args, specs


def _ab_out_kernel(x_ref, mod_ref, ya_ref, gg_ref, hf_ref, hb_ref, w_ref, *rest):
    *tail_refs, o_ref = rest
    yb = gg_ref[...] * (hf_ref[...] + hb_ref[...])
    y = _dot_bf16(ya_ref[...], w_ref[0:W_HALF, :]) + _dot_bf16(yb, w_ref[W_HALF:, :])
    o_ref[...] = _sublayers_tail(x_ref[...], mod_ref[0], y, tail_refs)


def _ab_out(x, mods, ya, gg, hf, hb, pw):
    n = x.shape[0]
    tpb = n // mods.shape[0] // TM
    tok = lambda i: (i, 0)
    half = pl.BlockSpec((TM, W_HALF), tok)
    tail_args, tail_specs = _tail_specs(pw)
    return pl.pallas_call(
        _ab_out_kernel,
        grid=(n // TM,),
        in_specs=[pl.BlockSpec((TM, D_MODEL), tok),
                  pl.BlockSpec((1, 6, D_MODEL), lambda i: (i // tpb, 0, 0)),
                  half, half, half, half,
                  pl.BlockSpec(pw['w_out'].shape, lambda i: (0, 0))] + tail_specs,
        out_specs=pl.BlockSpec((TM, D_MODEL), tok),
        out_shape=jax.ShapeDtypeStruct((n, D_MODEL), F32),
        compiler_params=_cparams(("parallel",)),
        name="ab_out_mlp",
    )(x, mods, ya, gg, hf, hb, pw['w_out'], *tail_args)


def _cd_in_kernel(x_ref, mod_ref, w_ref, hyc_ref, mu_ref, mux_ref, l1_ref, l2_ref, l0_ref,
                  kkp_ref, ka_ref, rk_ref, ones_ref,
                  hv_ref, hx1_ref, hx2_ref, r_ref, v_ref, kk_ref, w_out_ref, kd_ref, b_ref,
                  bonus_ref, sg_ref, *, line):
    x = x_ref[...]
    m = mod_ref[0]
    h = x * (1.0 + m[1:2]) + m[0:1]
    proj = _dot_bf16(h, w_ref[...])
    w = W_HALF
    u = _dwconv_rows(proj[:, 0:3 * w], hyc_ref[...], 1, line)
    hv_ref[...] = u[:, 0:w]
    hx1_ref[...] = u[:, w:2 * w]
    hx2_ref[...] = u[:, 2 * w:3 * w]

    mixed = []
    for n in range(4):
        t = proj[:, (3 + n) * w:(4 + n) * w]
        mixed.append(t + (_tshift_rows(t, line) - t) * mu_ref[n:n + 1])
    r, k, v, g = mixed
    dh = _tshift_rows(h, line) - h
    xw = h + dh * mux_ref[0:1]
    xa = h + dh * mux_ref[1:2]
    tw = jnp.tanh(_dot_bf16(xw, l1_ref[:, 0:128]))
    ta = _dot_bf16(xa, l1_ref[:, 128:256])
    zw = _dot_bf16(tw, l2_ref[0:128, :]) + l0_ref[0:1]
    za = _dot_bf16(ta, l2_ref[128:256, :]) + l0_ref[1:2]

    ones = ones_ref[...]
    kk = k * kkp_ref[...]
    kk = kk * lax.rsqrt(_group_sum(kk * kk, ones) + 1e-12)
    r_ref[...] = r
    v_ref[...] = v
    kk_ref[...] = kk
    kd_sum = None
    for d in range(2):
        w_raw = -_softplus(-zw[:, d * w:(d + 1) * w]) - 0.5
        w_out_ref[d] = jnp.exp(-jnp.exp(w_raw))
        a = _sigmoid(za[:, d * w:(d + 1) * w])
        kd = k * (1.0 + (a - 1.0) * ka_ref[...])
        kd_ref[d] = kd
        b_ref[d] = kk * a
        kd_sum = kd if kd_sum is None else kd_sum + kd
    bonus_ref[...] = _group_sum(r * kd_sum * rk_ref[...], ones) * v
    sg_ref[...] = _sigmoid(g)


def _cd_in(x, mods, w_in, hyc, mu, mux, l1, l2, l0, kkp, ka, rk, ones, line):
    n = x.shape[0]
    tpb = n // mods.shape[0] // TM
    tok = lambda i: (i, 0)
    const2 = lambda i: (0, 0)
    half = jax.ShapeDtypeStruct((n, W_HALF), F32)
    both = jax.ShapeDtypeStruct((2, n, W_HALF), F32)
    half_spec = pl.BlockSpec((TM, W_HALF), tok)
    both_spec = pl.BlockSpec((2, TM, W_HALF), lambda i: (0, i, 0))
    consts = [w_in, hyc, mu, mux, l1, l2, l0, kkp, ka, rk, ones]
    return pl.pallas_call(
        functools.partial(_cd_in_kernel, line=line),
        grid=(n // TM,),
        in_specs=[pl.BlockSpec((TM, D_MODEL), tok),
                  pl.BlockSpec((1, 6, D_MODEL), lambda i: (i // tpb, 0, 0))]
                 + [pl.BlockSpec(c.shape, const2) for c in consts],
        out_specs=[half_spec] * 6 + [both_spec] * 3 + [half_spec] * 2,
        out_shape=[half] * 6 + [both] * 3 + [half] * 2,
        compiler_params=_cparams(("parallel",)),
        name="cd_in",
    )(x, mods, *consts)


def _rwkv_scan_kernel(rf_ref, vf_ref, kkf_ref, wf_ref, kdf_ref, bf_ref,
                      rb_ref, vb_ref, kkb_ref, wb_ref, kdb_ref, bb_ref,
                      s0_ref, ones_ref, eye_ref,
                      of_ref, ob_ref, fin_ref, s_scr, sb_scr, o_scr, *, tc, nc, gb):
    c = pl.program_id(1)
    nsub = tc // RW_HEAD
    grp = 8
    per_sub = RW_HEAD // grp

    @pl.when(c == 0)
    def _():
        s_scr[...] = s0_ref[...]
        sb_scr[...] = s0_ref[...].astype(BF16)
        o_scr[...] = jnp.zeros_like(o_scr)

    ones = ones_ref[...]
    lane_pos = lax.broadcasted_iota(jnp.int32, (RW_HEAD, SCAN_LANES), 1) & (RW_HEAD - 1)
    chains = []
    for g in range(gb):
        chains.append((g, 0, rf_ref, vf_ref, kkf_ref, wf_ref, kdf_ref, bf_ref))
        chains.append((g, 1, rb_ref, vb_ref, kkb_ref, wb_ref, kdb_ref, bb_ref))

    hd = RW_HEAD

    tiles = [[(ch, pl.ds(j * SCAN_LANES, SCAN_LANES)) for ch in chains[c0:c0 + SCAN_CHAINS]]
             for c0 in range(0, len(chains), SCAN_CHAINS) for j in range(W_HALF // SCAN_LANES)]

    def put_outputs(streams, res_rows, step, base):
        for si, ((g, d, *_), ls) in enumerate(streams):
            pos = base + step if d == 0 else RW_HEAD - 1 - (base + step)
            o_scr[g, d, :, ls] = jnp.where(lane_pos == pos, res_rows[si * hd:(si + 1) * hd], o_scr[g, d, :, ls])

    def group(i, carry):
        sub = i // per_sub
        base = (i % per_sub) * grp
        row_f = pl.ds(pl.multiple_of(i * grp, grp), grp)
        row_b = pl.ds(pl.multiple_of(tc - grp - i * grp, grp), grp)

        def row(ref, g, d, ls, s, per_dir=False):
            q = s if d == 0 else grp - 1 - s
            tile = ref[0, g, row_b if d else row_f, ls] if per_dir else ref[g, row_b if d else row_f, ls]
            return tile[q:q + 1]

        pending = [None] * len(tiles)
        for s in range(grp):
            for ti, streams in enumerate(tiles):
                nst = len(streams)
                lhs = []
                for ((g, d, r_ref, v_ref, kk_ref, w_ref, kd_ref, b_ref), ls) in streams:
                    lhs.append(sb_scr[g, d, :, ls] * row(kk_ref, g, d, ls, s).astype(BF16))
                for ((g, d, r_ref, v_ref, kk_ref, w_ref, kd_ref, b_ref), ls) in streams:
                    lhs.append(eye_ref[:, ls] * row(v_ref, g, d, ls, s).astype(BF16))
                if pending[ti] is not None:
                    lhs += pending[ti]
                res = jnp.dot(jnp.concatenate(lhs, 0), ones, preferred_element_type=F32)
                if pending[ti] is not None:
                    put_outputs(streams, res[2 * nst * hd:], s - 1, base)
                nxt = []
                for si, ((g, d, r_ref, v_ref, kk_ref, w_ref, kd_ref, b_ref), ls) in enumerate(streams):
                    sa = res[si * hd:(si + 1) * hd]
                    vcol = res[(nst + si) * hd:(nst + si + 1) * hd]
                    st = (s_scr[g, d, :, ls] * row(w_ref, g, d, ls, s, True)
                          - sa * row(b_ref, g, d, ls, s, True) + vcol * row(kd_ref, g, d, ls, s, True))
                    s_scr[g, d, :, ls] = st
                    stb = st.astype(BF16)
                    sb_scr[g, d, :, ls] = stb
                    nxt.append(stb * row(r_ref, g, d, ls, s).astype(BF16))
                pending[ti] = nxt
        for ti, streams in enumerate(tiles):
            res = jnp.dot(jnp.concatenate(pending[ti], 0), ones, preferred_element_type=F32)
            put_outputs(streams, res, grp - 1, base)

        @pl.when(i % per_sub == per_sub - 1)
        def _():
            for g in range(gb):
                for d, (o_ref, blk) in enumerate(((of_ref, sub), (ob_ref, nsub - 1 - sub))):
                    ot = o_scr[g, d].T
                    for h in range(RW_HEADS):
                        o_ref[g, blk, :, h * hd:(h + 1) * hd] = ot[h * hd:(h + 1) * hd, :]

        return carry

    lax.fori_loop(0, tc // grp, group, 0)

    @pl.when(c == nc - 1)
    def _():
        fin_ref[...] = s_scr[...]


def _rwkv_scan(r, v, kk, w, kd, b, s0, ones2, eye, bsz, seq):
    tc = 64
    nc = seq // tc
    gb = min(bsz, SCAN_BATCH)
    r, v, kk = (t.reshape(bsz, seq, W_HALF) for t in (r, v, kk))
    w, kd, b = (t.reshape(2, bsz, seq, W_HALF) for t in (w, kd, b))
    shared = (gb, tc, W_HALF)
    perdir = (1, gb, tc, W_HALF)
    sf = lambda i, c: (i, c, 0)
    sb = lambda i, c: (i, nc - 1 - c, 0)
    pf = lambda i, c: (0, i, c, 0)
    pb = lambda i, c: (1, i, nc - 1 - c, 0)
    nsub = tc // RW_HEAD
    o_shape = jax.ShapeDtypeStruct((bsz, seq // RW_HEAD, RW_HEAD, W_HALF), F32)
    o_blk = (gb, nsub, RW_HEAD, W_HALF)
    st_blk = (gb, 2, RW_HEAD, W_HALF)
    return pl.pallas_call(
        functools.partial(_rwkv_scan_kernel, tc=tc, nc=nc, gb=gb),
        grid=(bsz // gb, nc),
        in_specs=[pl.BlockSpec(shared, sf)] * 3 + [pl.BlockSpec(perdir, pf)] * 3
                 + [pl.BlockSpec(shared, sb)] * 3 + [pl.BlockSpec(perdir, pb)] * 3
                 + [pl.BlockSpec(st_blk, lambda i, c: (i, 0, 0, 0)),
                    pl.BlockSpec(ones2.shape, lambda i, c: (0, 0)),
                    pl.BlockSpec(eye.shape, lambda i, c: (0, 0))],
        out_specs=[pl.BlockSpec(o_blk, lambda i, c: (i, c, 0, 0)),
                   pl.BlockSpec(o_blk, lambda i, c: (i, nc - 1 - c, 0, 0)),
                   pl.BlockSpec(st_blk, lambda i, c: (i, 0, 0, 0))],
        out_shape=[o_shape, o_shape, jax.ShapeDtypeStruct((bsz, 2, RW_HEAD, W_HALF), F32)],
        scratch_shapes=[pltpu.VMEM(st_blk, F32), pltpu.VMEM(st_blk, BF16), pltpu.VMEM(st_blk, F32)],
        compiler_params=_cparams(("parallel", "arbitrary")),
        name="rwkv_scan",
    )(r, v, kk, w, kd, b, r, v, kk, w, kd, b, s0, ones2, eye)


def _cd_out_kernel(x_ref, mod_ref, yc_ref, of_ref, ob_ref, bonus_ref, sg_ref, gng_ref, gnb_ref,
                   ones_ref, w_ref, *rest):
    *tail_refs, o_ref = rest
    ones = ones_ref[...]
    o = of_ref[...] + ob_ref[...]
    oc = o - _group_sum(o, ones) * (1.0 / RW_HEAD)
    on = oc * lax.rsqrt(_group_sum(oc * oc, ones) * (1.0 / RW_HEAD) + GN_EPS)
    yd = (on * gng_ref[...] + gnb_ref[...] + bonus_ref[...]) * sg_ref[...]
    y = _dot_bf16(yc_ref[...], w_ref[0:W_HALF, :]) + _dot_bf16(yd, w_ref[W_HALF:, :])
    o_ref[...] = _sublayers_tail(x_ref[...], mod_ref[0], y, tail_refs)


def _cd_out(x, mods, yc, of, ob, bonus, sg, ones, pw):
    n = x.shape[0]
    tpb = n // mods.shape[0] // TM
    tok = lambda i: (i, 0)
    const2 = lambda i: (0, 0)
    half = pl.BlockSpec((TM, W_HALF), tok)
    vec = pl.BlockSpec((1, W_HALF), const2)
    tail_args, tail_specs = _tail_specs(pw)
    return pl.pallas_call(
        _cd_out_kernel,
        grid=(n // TM,),
        in_specs=[pl.BlockSpec((TM, D_MODEL), tok),
                  pl.BlockSpec((1, 6, D_MODEL), lambda i: (i // tpb, 0, 0)),
                  half, half, half, half, half, vec, vec,
                  pl.BlockSpec(ones.shape, const2),
                  pl.BlockSpec(pw['w_out'].shape, const2)] + tail_specs,
        out_specs=pl.BlockSpec((TM, D_MODEL), tok),
        out_shape=jax.ShapeDtypeStruct((n, D_MODEL), F32),
        compiler_params=_cparams(("parallel",)),
        name="cd_out_mlp",
    )(x, mods, yc, of, ob, bonus, sg, pw['rw_gn_g'], pw['rw_gn_b'], ones, pw['w_out'], *tail_args)


def _hyena_consts(seq):
    tn = np.linspace(0.0, 1.0, seq, dtype=np.float32)
    tr = np.arange(seq, dtype=np.float32)
    bands = np.linspace(1e-4, HY_BANDS - 1, HY_BANDS, dtype=np.float32)
    ang = np.float32(2.0 * math.pi / seq) * tr[:, None] * bands[None, :]
    feats = np.concatenate([tn[:, None], np.cos(ang), -np.sin(ang)], -1).astype(np.float32)
    feats = np.pad(feats, ((0, 0), (0, 40 - feats.shape[1])))
    deltas = np.abs(np.linspace(math.log(HY_TARGET) / HY_LONG_PCT, math.log(HY_TARGET) / HY_SHORT_PCT,
                                W_HALF, dtype=np.float32))
    decay = np.exp(-tn[:, None] * deltas[None, :]).astype(np.float32)
    lag = np.concatenate([np.arange(seq), [0], np.arange(seq - 1, 0, -1)])
    return feats[lag], decay[lag]


def _filt_kernel(f_ref, dec_ref, w1_ref, b1_ref, w2_ref, b2_ref, w3_ref, fr_ref, o_ref, ssq_ref, *, half):
    i = pl.program_id(0)
    fr = fr_ref[...]
    hid = jnp.sin(fr * (jnp.dot(f_ref[...], w1_ref[...], precision=HIGHEST,
                                preferred_element_type=F32) + b1_ref[...]))
    hid = jnp.sin(fr * (jnp.dot(hid, w2_ref[...], precision=HIGHEST,
                                preferred_element_type=F32) + b2_ref[...]))
    raw = jnp.dot(hid, w3_ref[...], precision=HIGHEST, preferred_element_type=F32)
    dec = dec_ref[...]
    rows = raw.shape[0]
    backward = i >= half
    middle = (lax.broadcasted_iota(jnp.int32, (rows, W_HALF), 0) == 0) & (i == half)
    parts = []
    for order in range(2):
        fwd = raw[:, (2 * order) * W_HALF:(2 * order + 1) * W_HALF]
        bwd = raw[:, (2 * order + 1) * W_HALF:(2 * order + 2) * W_HALF]
        f = jnp.where(middle, 0.0, jnp.where(backward, bwd, fwd) * dec)
        o_ref[order] = f
        parts.append(jnp.sum(f * f, 0, keepdims=True))
    ssq = jnp.concatenate(parts, 0)

    @pl.when(i == 0)
    def _():
        ssq_ref[...] = ssq

    @pl.when(i > 0)
    def _():
        ssq_ref[...] += ssq


def _hyena_filters(seq, w1, b1, w2, b2, w3, freq):
    feats, decay = _hyena_consts(seq)
    w1 = jnp.pad(w1, ((0, 40 - w1.shape[0]), (0, 0)))
    rows = 256
    const2 = lambda i: (0, 0)
    tok = lambda i: (i, 0)
    vec = lambda a: a.reshape(1, -1)
    args = [jnp.asarray(feats), jnp.asarray(decay), w1, vec(b1), w2, vec(b2), w3, vec(freq)]
    return pl.pallas_call(
        functools.partial(_filt_kernel, half=seq // rows),
        grid=(2 * seq // rows,),
        in_specs=[pl.BlockSpec((rows, 40), tok), pl.BlockSpec((rows, W_HALF), tok)]
                 + [pl.BlockSpec(a.shape, const2) for a in args[2:]],
        out_specs=[pl.BlockSpec((2, rows, W_HALF), lambda i: (0, i, 0)), pl.BlockSpec((2, W_HALF), const2)],
        out_shape=[jax.ShapeDtypeStruct((2, 2 * seq, W_HALF), F32),
                   jax.ShapeDtypeStruct((2, W_HALF), F32)],
        compiler_params=_cparams(("arbitrary",)),
        name="hyena_filters",
    )(*args)


def _dft_consts_two_stage(seq):
    n = 2 * seq
    p = DFT_P
    q = n // p
    a = np.arange(q)[None, :]
    k1 = np.arange(q)[:, None]
    ang1 = 2.0 * np.pi * a * k1 / q
    f1 = np.empty((2 * q, q))
    f1[0::2] = np.cos(ang1)
    f1[1::2] = -np.sin(ang1)
    f3 = np.empty((q, 2 * q))
    f3[:, 0::2] = np.cos(ang1).T / n
    f3[:, 1::2] = -np.sin(ang1).T / n
    qq = np.arange(p)[None, None, :]
    k2 = np.arange(p)[None, :, None]
    kk1 = np.arange(q)[:, None, None]
    ang = 2.0 * np.pi * (qq * k2 / p + qq * kk1 / n)
    gr, gi = np.cos(ang), -np.sin(ang)
    g = np.concatenate([np.concatenate([gr, -gi], 2), np.concatenate([gi, gr], 2)], 1)
    grt, git = np.swapaxes(gr, 1, 2), np.swapaxes(gi, 1, 2)
    ginv = np.concatenate([np.concatenate([grt, git], 2), np.concatenate([-git, grt], 2)], 1)
    half = q // 2
    return (f1.astype(np.float32), f1[:, :half].astype(np.float32), f3[:half].astype(np.float32),
            g.astype(np.float32), ginv.astype(np.float32))


def _dft_consts_one_stage(seq):
    n = 2 * seq
    t = np.arange(n)[None, :]
    f = np.arange(n)[:, None]
    ang = 2.0 * np.pi * t * f / n
    fwd = np.concatenate([np.cos(ang), -np.sin(ang)], 0)
    inv = np.concatenate([np.cos(ang), -np.sin(ang)], 1)[:seq] / n
    return fwd.astype(np.float32), fwd[:, :seq].astype(np.float32), inv.astype(np.float32)


def _expand_rows(f, rows=DFT_ROWS):
    return np.kron(f, np.eye(rows, dtype=f.dtype))


def _stage_kernel(f_ref, x_ref, o_ref, *, exact):
    k, rows, ch = x_ref.shape[1:]
    x = x_ref[0].reshape(k * rows, ch)
    if exact:
        res = jnp.dot(f_ref[...], x, precision=HIGHEST, preferred_element_type=F32)
    else:
        res = _dot_bf16(f_ref[...], x.astype(BF16))
    o_ref[0] = res.reshape(o_ref.shape[1:]).astype(o_ref.dtype)


def _stage(f, x, exact=False):
    bsz, k, p, ch = x.shape
    rr = f.shape[0]
    rows = DFT_ROWS // 2 if exact else DFT_ROWS
    fx = jnp.asarray(_expand_rows(f, rows), dtype=F32 if exact else BF16)
    return pl.pallas_call(
        functools.partial(_stage_kernel, exact=exact),
        grid=(bsz, p // rows),
        in_specs=[pl.BlockSpec(fx.shape, lambda b, j: (0, 0)),
                  pl.BlockSpec((1, k, rows, ch), lambda b, j: (b, 0, j, 0))],
        out_specs=pl.BlockSpec((1, rr, rows, ch), lambda b, j: (b, 0, j, 0)),
        out_shape=jax.ShapeDtypeStruct((bsz, rr, p, ch), F32 if exact else BF16),
        compiler_params=_cparams(("parallel", "parallel")),
        name="dft_stage",
    )(fx, x)


def _spectrum_kernel(a_ref, g_ref, ssq_ref, o_ref):
    x = jnp.dot(g_ref[0], a_ref[0, 0], precision=HIGHEST, preferred_element_type=F32)
    o_ref[0, 0] = x * lax.rsqrt(ssq_ref[0] + 1e-6)


def _spectrum(a, g, ssq):
    nb, q, k, ch = a.shape
    rr = g.shape[1]
    return pl.pallas_call(
        _spectrum_kernel,
        grid=(nb, q),
        in_specs=[pl.BlockSpec((1, 1, k, ch), lambda b, j: (b, j, 0, 0)),
                  pl.BlockSpec((1, rr, k), lambda b, j: (j, 0, 0)),
                  pl.BlockSpec((1, 1, ch), lambda b, j: (b, 0, 0))],
        out_specs=pl.BlockSpec((1, 1, rr, ch), lambda b, j: (b, j, 0, 0)),
        out_shape=jax.ShapeDtypeStruct((nb, q, rr, ch), F32),
        compiler_params=_cparams(("parallel", "parallel")),
        name="hyena_spectrum",
    )(a, g, ssq)


def _spectral_conv(a, g_ref, h_ref, gi_ref):
    x = _dot_bf16(g_ref[0], a.astype(BF16))
    half = x.shape[0] // 2
    xr, xi = x[:half], x[half:]
    hr, hi = h_ref[0, 0, :half], h_ref[0, 0, half:]
    y = jnp.concatenate([xr * hr - xi * hi, xr * hi + xi * hr], 0)
    return _dot_bf16(gi_ref[0], y.astype(BF16))


def _mid_kernel(a_ref, g_ref, h_ref, gi_ref, o_ref):
    for b in range(a_ref.shape[0]):
        o_ref[b, 0] = _spectral_conv(a_ref[b, 0], g_ref, h_ref, gi_ref).astype(o_ref.dtype)


def _mid_gate_kernel(a_ref, g_ref, h_ref, gi_ref, hx_ref, bias_ref, o_ref):
    for b in range(a_ref.shape[0]):
        a = a_ref[b, 0]
        o_ref[b, 0] = hx_ref[b, 0] * (_spectral_conv(a, g_ref, h_ref, gi_ref) + bias_ref[...] * a)


def _mid(a, g, h, gi, order, gate=None, out_dtype=F32):
    bsz, q, k, ch = a.shape
    rr = g.shape[1]
    ro = gi.shape[1]
    nb = min(bsz, MID_BATCH)
    act = lambda j, b: (b, j, 0, 0)
    in_specs = [pl.BlockSpec((nb, 1, k, ch), act),
                pl.BlockSpec((1, rr, k), lambda j, b: (j, 0, 0)),
                pl.BlockSpec((1, 1, rr, ch), lambda j, b: (order, j, 0, 0)),
                pl.BlockSpec((1, ro, rr), lambda j, b: (j, 0, 0))]
    args = [a, g, h, gi]
    kern = _mid_kernel
    if gate is not None:
        in_specs += [pl.BlockSpec((nb, 1, ro, ch), act), pl.BlockSpec((1, ch), lambda j, b: (0, 0))]
        args += list(gate)
        kern = _mid_gate_kernel
    return pl.pallas_call(
        kern,
        grid=(q, bsz // nb),
        in_specs=in_specs,
        out_specs=pl.BlockSpec((nb, 1, ro, ch), act),
        out_shape=jax.ShapeDtypeStruct((bsz, q, ro, ch), out_dtype),
        compiler_params=_cparams(("parallel", "parallel")),
        name="hyena_mid",
    )(*args)


def _last_stage_kernel(f3_ref, bt_ref, hx_ref, z_ref, bias_ref, o_ref):
    k, rows, ch = bt_ref.shape[1:]
    conv = _dot_bf16(f3_ref[...], bt_ref[0].reshape(k * rows, ch))
    flat = conv.shape
    out = hx_ref[0].reshape(flat) * (conv + bias_ref[...] * z_ref[0].reshape(flat))
    o_ref[0] = out.reshape(o_ref.shape[1:])


def _last_stage(f3, bt, hx, z, bias):
    bsz, k, p, ch = bt.shape
    rr = f3.shape[0]
    fx = jnp.asarray(_expand_rows(f3), dtype=BF16)
    act = pl.BlockSpec((1, rr, DFT_ROWS, ch), lambda b, j: (b, 0, j, 0))
    return pl.pallas_call(
        _last_stage_kernel,
        grid=(bsz, p // DFT_ROWS),
        in_specs=[pl.BlockSpec(fx.shape, lambda b, j: (0, 0)),
                  pl.BlockSpec((1, k, DFT_ROWS, ch), lambda b, j: (b, 0, j, 0)),
                  act, act,
                  pl.BlockSpec((1, ch), lambda b, j: (0, 0))],
        out_specs=act,
        out_shape=jax.ShapeDtypeStruct((bsz, rr, p, ch), F32),
        compiler_params=_cparams(("parallel", "parallel")),
        name="dft_last_stage",
    )(fx, bt, hx, z, bias)


def _hyena(hv, hx1, hx2, circ, ssq, bias, bsz, seq):
    ch = W_HALF
    ssq = ssq.reshape(2, 1, ch)
    if seq <= 512:
        fwd_full, fwd_half, inv = _dft_consts_one_stage(seq)
        spec = _spectrum(circ[:, None], jnp.asarray(fwd_full)[None], ssq)
        fwd_half = jnp.asarray(fwd_half, dtype=BF16)[None]
        inv = jnp.asarray(inv, dtype=BF16)[None]
        shape4 = (bsz, 1, seq, ch)
        z = _mid(hv.reshape(shape4), fwd_half, spec, inv, 0, gate=(hx1.reshape(shape4), bias[0:1]))
        y = _mid(z, fwd_half, spec, inv, 1, gate=(hx2.reshape(shape4), bias[1:2]))
        return y.reshape(bsz * seq, ch)
    f1_full, f1_half, f3, g, ginv = _dft_consts_two_stage(seq)
    p = DFT_P
    q = 2 * seq // p
    spec = _spectrum(_stage(f1_full, circ.reshape(2, q, p, ch), exact=True).reshape(2, q, 2 * p, ch),
                     jnp.asarray(g), ssq)
    g = jnp.asarray(g, dtype=BF16)
    ginv = jnp.asarray(ginv, dtype=BF16)
    nat = (bsz, q // 2, p, ch)
    out = hv.reshape(nat)
    for order, hx in enumerate((hx1, hx2)):
        a = _stage(f1_half, out).reshape(bsz, q, 2 * p, ch)
        bt = _mid(a, g, spec, ginv, order, out_dtype=BF16).reshape(bsz, 2 * q, p, ch)
        out = _last_stage(f3, bt, hx.reshape(nat), out, bias[order:order + 1])
    return out.reshape(bsz * seq, ch)


def _block_diag(w):
    hh, blk, _ = w.shape
    eye = jnp.eye(hh, dtype=w.dtype)
    return jnp.einsum('hij,hg->higj', w, eye).reshape(hh * blk, hh * blk)


def _group_ones(width):
    idx = np.arange(width) // RW_HEAD
    return jnp.asarray((idx[:, None] == idx[None, :]).astype(np.float32), dtype=BF16)


def _even_layer(x, mods, h0, bsz, seq, line, pw):
    ya, gg, a, u = _ab_in(x, mods, pw['w_in'], pw['sc_conv'], pw['lru_conv'], pw['lru_conv_b'],
                          pw['wg'], pw['bg'], pw['nsp'], line)
    hf, hb, fin = _lru_scan(a, u, h0, bsz, seq)
    x = _ab_out(x, mods, ya, gg, hf.reshape(-1, W_HALF), hb.reshape(-1, W_HALF), pw)
    return x, fin


def _odd_layer(x, mods, s0, bsz, seq, line, pw):
    ones = _group_ones(W_HALF)
    (hv, hx1, hx2, r, v, kk, w, kd, b, bonus, sg) = _cd_in(
        x, mods, pw['w_in'], pw['hy_conv'], pw['rw_mu'], pw['rw_mu_x'], pw['l1'], pw['l2'], pw['l0'],
        pw['rw_kk'], pw['rw_ka'], pw['rw_rk'], ones, line)
    filt, ssq = _hyena_filters(seq, pw['hy_w1'], pw['hy_b1'], pw['hy_w2'], pw['hy_b2'], pw['hy_w3'],
                               pw['hy_freq'])
    yc = _hyena(hv, hx1, hx2, filt, ssq, pw['hy_bias'], bsz, seq)

    s0 = jnp.transpose(s0, (0, 1, 3, 2, 4)).reshape(bsz, 2, RW_HEAD, W_HALF)
    lane = np.arange(W_HALF)
    eye = jnp.asarray(((lane[None, :] % RW_HEAD) == np.arange(RW_HEAD)[:, None]).astype(np.float32),
                      dtype=BF16)
    of, ob, fin = _rwkv_scan(r, v, kk, w, kd, b, s0, _group_ones(SCAN_LANES), eye, bsz, seq)
    x = _cd_out(x, mods, yc, of.reshape(-1, W_HALF), ob.reshape(-1, W_HALF), bonus, sg, ones, pw)
    fin = jnp.transpose(fin.reshape(bsz, 2, RW_HEAD, RW_HEADS, RW_HEAD), (0, 1, 3, 2, 4))
    return x, fin


def _layer_weights(p, l):
    j = l // 2
    row = lambda a: a.reshape(1, -1)
    pw = {'w_out': p['w_out'][l].astype(BF16), 'ln1_g': row(p['ln1_g'][l]), 'ln1_b': row(p['ln1_b'][l]),
          'ln2_g': row(p['ln2_g'][l]), 'ln2_b': row(p['ln2_b'][l]),
          'mlp_w1': p['mlp_w1'][l].astype(BF16), 'mlp_w2': p['mlp_w2'][l].astype(BF16)}
    if l % 2 == 0:
        gates = [_block_diag(p[name][j, d]) for d in range(2) for name in ('lru_wa', 'lru_wi')]
        biases = [p[name][j, d] for d in range(2) for name in ('lru_ba', 'lru_bi')]
        pw.update({'w_in': p['ab_w_in'][j].astype(BF16), 'sc_conv': p['sc_conv'][j],
                   'lru_conv': p['lru_conv'][j], 'lru_conv_b': row(p['lru_conv_b'][j]),
                   'wg': jnp.concatenate(gates, 1).astype(BF16), 'bg': row(jnp.concatenate(biases)),
                   'nsp': jax.nn.softplus(-p['lru_lambda'][j])})
    else:
        zeros = jnp.zeros((64, W_HALF), F32)
        l2 = jnp.concatenate([
            jnp.concatenate([p['rw_w2'][j, 0], zeros], 1), jnp.concatenate([zeros, p['rw_w2'][j, 1]], 1),
            jnp.concatenate([p['rw_a2'][j, 0], zeros], 1), jnp.concatenate([zeros, p['rw_a2'][j, 1]], 1)], 0)
        pw.update({'w_in': p['cd_w_in'][j].astype(BF16), 'hy_conv': p['hy_conv'][j],
                   'rw_mu': p['rw_mu'][j], 'rw_mu_x': p['rw_mu_x'][j],
                   'l1': jnp.concatenate([p['rw_w1'][j, 0], p['rw_w1'][j, 1],
                                          p['rw_a1'][j, 0], p['rw_a1'][j, 1]], 1).astype(BF16),
                   'l2': l2.astype(BF16),
                   'l0': jnp.stack([p['rw_w0'][j].reshape(-1), p['rw_a0'][j].reshape(-1)]),
                   'rw_kk': row(p['rw_kk'][j]), 'rw_ka': row(p['rw_ka'][j]), 'rw_rk': row(p['rw_rk'][j]),
                   'rw_gn_g': row(p['rw_gn_g'][j]), 'rw_gn_b': row(p['rw_gn_b'][j]),
                   'hy_w1': p['hy_w1'][j], 'hy_b1': p['hy_b1'][j], 'hy_w2': p['hy_w2'][j],
                   'hy_b2': p['hy_b2'][j], 'hy_w3': p['hy_w3'][j], 'hy_freq': p['hy_freq'][j],
                   'hy_bias': p['hy_bias'][j]})
    return pw


def _to_colmajor(x, bsz, rows):
    return jnp.transpose(x.reshape(bsz, rows, GRID_W, D_MODEL), (0, 2, 1, 3)).reshape(-1, D_MODEL)


def _from_colmajor(x, bsz, rows):
    return jnp.transpose(x.reshape(bsz, GRID_W, rows, D_MODEL), (0, 2, 1, 3)).reshape(-1, D_MODEL)


def _trunk(x, mods, init_lru, init_rwkv, rows, weights):
    bsz, seq, _ = x.shape
    x = x.reshape(bsz * seq, D_MODEL)
    new_lru, new_rwkv = [], []
    for l in range(DEPTH):
        j = l // 2
        pw = weights[l]
        if l % 2 == 0:
            line = seq if rows is None else GRID_W
            x, st = _even_layer(x, mods[l], init_lru[:, j], bsz, seq, line, pw)
            new_lru.append(st)
        elif rows is None:
            x, st = _odd_layer(x, mods[l], init_rwkv[:, j], bsz, seq, seq, pw)
            new_rwkv.append(st)
        else:
            x = _to_colmajor(x, bsz, rows)
            x, st = _odd_layer(x, mods[l], init_rwkv[:, j], bsz, seq, rows, pw)
            x = _from_colmajor(x, bsz, rows)
            new_rwkv.append(st)
    return x.reshape(bsz, seq, D_MODEL), jnp.stack(new_lru, 1), jnp.stack(new_rwkv, 1)


def kernel(x_prompt, x_sample, state_lru, state_rwkv, c, c_ctx, w_mod, b_mod, ln1_g, ln1_b, ln2_g, ln2_b, mlp_w1, mlp_w2, w_out, ab_w_in, sc_conv, lru_conv, lru_conv_b, lru_wa, lru_ba, lru_wi, lru_bi, lru_lambda, cd_w_in, hy_conv, hy_w1, hy_b1, hy_w2, hy_b2, hy_w3, hy_freq, hy_bias, rw_mu, rw_mu_x, rw_w0, rw_w1, rw_w2, rw_a0, rw_a1, rw_a2, rw_kk, rw_ka, rw_rk, rw_gn_g, rw_gn_b):
    p = dict(ln1_g=ln1_g, ln1_b=ln1_b, ln2_g=ln2_g, ln2_b=ln2_b,
             mlp_w1=mlp_w1, mlp_w2=mlp_w2, w_out=w_out, ab_w_in=ab_w_in, sc_conv=sc_conv,
             lru_conv=lru_conv, lru_conv_b=lru_conv_b, lru_wa=lru_wa, lru_ba=lru_ba, lru_wi=lru_wi,
             lru_bi=lru_bi, lru_lambda=lru_lambda, cd_w_in=cd_w_in, hy_conv=hy_conv, hy_w1=hy_w1,
             hy_b1=hy_b1, hy_w2=hy_w2, hy_b2=hy_b2, hy_w3=hy_w3, hy_freq=hy_freq, hy_bias=hy_bias,
             rw_mu=rw_mu, rw_mu_x=rw_mu_x, rw_w0=rw_w0, rw_w1=rw_w1, rw_w2=rw_w2, rw_a0=rw_a0,
             rw_a1=rw_a1, rw_a2=rw_a2, rw_kk=rw_kk, rw_ka=rw_ka, rw_rk=rw_rk, rw_gn_g=rw_gn_g,
             rw_gn_b=rw_gn_b)
    weights = [_layer_weights(p, l) for l in range(DEPTH)]
    nb, dec = x_prompt.shape[0], x_sample.shape[0]
    rpad = -(1 + dec) % 8
    cvec = jnp.concatenate([c_ctx[None, :], c, jnp.zeros((rpad, D_MODEL), F32)], 0)
    mods = _mods(cvec, w_mod, b_mod)
    mods = jnp.transpose(mods, (0, 2, 1, 3))
    zero_lru = jnp.zeros((nb, (DEPTH + 1) // 2, 2, W_HALF), F32)
    zero_rwkv = jnp.zeros((nb, DEPTH // 2, 2, RW_HEADS, RW_HEAD, RW_HEAD), F32)
    y_prompt, new_lru, new_rwkv = _trunk(x_prompt, mods[:, 0:1], zero_lru, zero_rwkv, None, weights)
    rows = x_sample.shape[1] // GRID_W
    y_sample, _, _ = _trunk(x_sample, mods[:, 1:1 + dec], state_lru, state_rwkv, rows, weights)
    return (y_prompt, y_sample, new_lru, new_rwkv)
```

```python
import functools
import math

import jax
import jax.numpy as jnp
import numpy as np
from jax import lax
from jax.experimental import pallas as pl
from jax.experimental.pallas import tpu as pltpu

F32 = jnp.float32
BF16 = jnp.bfloat16
HIGHEST = lax.Precision.HIGHEST

D_MODEL = 1024
DEPTH = 4
GRID_W = 64
W_HALF = D_MODEL // 2
LRU_HEADS = 8
RG_C = 8.0
HY_BANDS = 16
HY_TARGET = 1e-2
HY_SHORT_PCT = 0.3
HY_LONG_PCT = 1.5
RW_HEAD = 64
RW_HEADS = W_HALF // RW_HEAD
D_FF = 4 * D_MODEL
DN_ALPHA = (2 * DEPTH) ** 0.25
LN_EPS = 1e-5
GN_EPS = 64e-5

LANES = 128
TM = 256
VMEM_LIMIT = 56 * 1024 * 1024
DFT_P = 128
DFT_ROWS = 16
MID_BATCH = 8
FF_CHUNK = 1024
LRU_BATCH = 4
SCAN_LANES = 256
SCAN_BATCH = 8
SCAN_CHAINS = 8


def _cparams(sem):
    return pltpu.CompilerParams(dimension_semantics=sem, vmem_limit_bytes=VMEM_LIMIT)


def _shift_rows(x, off, line):
    if off == 0:
        return x
    n, width = x.shape
    rolled = pltpu.roll(x, (-off) % n, 0)
    pos = lax.broadcasted_iota(jnp.int32, (n, LANES), 0) & (line - 1)
    valid = (pos + off >= 0) if off < 0 else (pos + off < line)
    return jnp.concatenate([jnp.where(valid, rolled[:, j:j + LANES], 0.0) for j in range(0, width, LANES)], 1)


def _dwconv_rows(x, w, pad_left, line):
    out = None
    for k in range(w.shape[0]):
        term = _shift_rows(x, k - pad_left, line) * w[k:k + 1]
        out = term if out is None else out + term
    return out


def _tshift_rows(x, line):
    return 0.5 * (_shift_rows(x, -1, line) + _shift_rows(x, 1, line))


def _layer_norm(v, g, b):
    mu = jnp.mean(v, -1, keepdims=True)
    vc = v - mu
    var = jnp.mean(vc * vc, -1, keepdims=True)
    return vc * lax.rsqrt(var + LN_EPS) * g + b


def _dot_bf16(a, b):
    return jnp.dot(a.astype(BF16), b, preferred_element_type=F32)


def _dot_f32_lhs(a, b):
    return lax.dot_general(a, b, (((1,), (0,)), ((), ())), preferred_element_type=F32)


def _group_sum(x, ones):
    hi = x.astype(BF16)
    lo = (x - hi.astype(F32)).astype(BF16)
    return (jnp.dot(hi, ones, preferred_element_type=F32)
            + jnp.dot(lo, ones, preferred_element_type=F32))


def _softplus(x):
    return jnp.maximum(x, 0.0) + jnp.log(1.0 + jnp.exp(-jnp.abs(x)))


def _sigmoid(x):
    return 1.0 / (1.0 + jnp.exp(-x))


def _mods_kernel(c_ref, w_ref, b_ref, o_ref):
    c = c_ref[...]
    s = c * _sigmoid(c)
    o_ref[0, 0] = jnp.dot(s, w_ref[0], precision=HIGHEST, preferred_element_type=F32) + b_ref[0, 0]


def _mods(cvec, w_mod, b_mod):
    r = cvec.shape[0]
    return pl.pallas_call(
        _mods_kernel,
        grid=(DEPTH, 6),
        in_specs=[pl.BlockSpec((r, D_MODEL), lambda l, n: (0, 0)),
                  pl.BlockSpec((1, D_MODEL, D_MODEL), lambda l, n: (l, 0, n)),
                  pl.BlockSpec((1, 1, 1, D_MODEL), lambda l, n: (l, n, 0, 0))],
        out_specs=pl.BlockSpec((1, 1, r, D_MODEL), lambda l, n: (l, n, 0, 0)),
        out_shape=jax.ShapeDtypeStruct((DEPTH, 6, r, D_MODEL), F32),
        compiler_params=_cparams(("parallel", "parallel")),
        name="mods",
    )(cvec, w_mod, b_mod.reshape(DEPTH, 6, 1, D_MODEL))


def _ab_in_kernel(x_ref, mod_ref, w_ref, scw_ref, lcw_ref, lcb_ref, wg_ref, bg_ref, nsp_ref,
                  ya_ref, gg_ref, a_ref, u_ref, *, line):
    x = x_ref[...]
    m = mod_ref[0]
    h = x * (1.0 + m[1:2]) + m[0:1]
    proj = _dot_bf16(h, w_ref[...])
    w = W_HALF
    s_b, s_c, s_v = proj[:, 0:w], proj[:, w:2 * w], proj[:, 2 * w:3 * w]
    g_lru, x_lru = proj[:, 3 * w:4 * w], proj[:, 4 * w:5 * w]
    ya_ref[...] = s_b * _dwconv_rows(s_c * s_v, scw_ref[...], 1, line)
    gg_ref[...] = 0.5 * g_lru * (1.0 + jnp.tanh(
        math.sqrt(2.0 / math.pi) * (g_lru + 0.044715 * (g_lru * g_lru * g_lru))))
    xc = _dwconv_rows(x_lru, lcw_ref[...], 2, line) + lcb_ref[...]
    gates = _dot_bf16(xc, wg_ref[...]) + bg_ref[...]
    for d in range(2):
        r = _sigmoid(gates[:, (2 * d) * w:(2 * d + 1) * w])
        i = _sigmoid(gates[:, (2 * d + 1) * w:(2 * d + 2) * w])
        a = jnp.exp(-RG_C * r * nsp_ref[d:d + 1])
        a_ref[d] = a
        u_ref[d] = jnp.sqrt(1.0 - a * a) * (i * xc)


def _ab_in(x, mods, w_in, scw, lcw, lcb, wg, bg, nsp, line):
    n = x.shape[0]
    tpb = n // mods.shape[0] // TM
    tok = lambda i: (i, 0)
    const2 = lambda i: (0, 0)
    half = jax.ShapeDtypeStruct((n, W_HALF), F32)
    both = jax.ShapeDtypeStruct((2, n, W_HALF), F32)
    return pl.pallas_call(
        functools.partial(_ab_in_kernel, line=line),
        grid=(n // TM,),
        in_specs=[pl.BlockSpec((TM, D_MODEL), tok),
                  pl.BlockSpec((1, 6, D_MODEL), lambda i: (i // tpb, 0, 0)),
                  pl.BlockSpec(w_in.shape, const2),
                  pl.BlockSpec(scw.shape, const2),
                  pl.BlockSpec(lcw.shape, const2),
                  pl.BlockSpec(lcb.shape, const2),
                  pl.BlockSpec(wg.shape, const2),
                  pl.BlockSpec(bg.shape, const2),
                  pl.BlockSpec(nsp.shape, const2)],
        out_specs=[pl.BlockSpec((TM, W_HALF), tok), pl.BlockSpec((TM, W_HALF), tok),
                   pl.BlockSpec((2, TM, W_HALF), lambda i: (0, i, 0)),
                   pl.BlockSpec((2, TM, W_HALF), lambda i: (0, i, 0))],
        out_shape=[half, half, both, both],
        compiler_params=_cparams(("parallel",)),
        name="ab_in",
    )(x, mods, w_in, scw, lcw, lcb, wg, bg, nsp)


def _lru_scan_kernel(af_ref, uf_ref, ab_ref, ub_ref, h0_ref, hf_ref, hb_ref, fin_ref, carry, *, tc, nc):
    c = pl.program_id(1)

    nb = af_ref.shape[1]

    @pl.when(c == 0)
    def _():
        carry[...] = h0_ref[...]

    def step(s, hs):
        tb = tc - 1 - s
        out = []
        for g in range(nb):
            hf, hb = hs[2 * g], hs[2 * g + 1]
            hf = af_ref[0, g, pl.ds(s, 1), :] * hf + uf_ref[0, g, pl.ds(s, 1), :]
            hb = ab_ref[0, g, pl.ds(tb, 1), :] * hb + ub_ref[0, g, pl.ds(tb, 1), :]
            hf_ref[g, pl.ds(s, 1), :] = hf
            hb_ref[g, pl.ds(tb, 1), :] = hb
            out += [hf, hb]
        return tuple(out)

    init = tuple(carry[g, d:d + 1] for g in range(nb) for d in range(2))
    hs = lax.fori_loop(0, tc, step, init, unroll=8)
    for g in range(nb):
        for d in range(2):
            carry[g, d:d + 1] = hs[2 * g + d]

    @pl.when(c == nc - 1)
    def _():
        fin_ref[...] = carry[...]


def _lru_scan(a, u, h0, bsz, seq):
    tc = min(seq, 512)
    nc = seq // tc
    nb = min(bsz, LRU_BATCH)
    a = a.reshape(2, bsz, seq, W_HALF)
    u = u.reshape(2, bsz, seq, W_HALF)
    fwd = lambda b, c: (0, b, c, 0)
    bwd = lambda b, c: (1, b, nc - 1 - c, 0)
    blk = (1, nb, tc, W_HALF)
    seq_shape = jax.ShapeDtypeStruct((bsz, seq, W_HALF), F32)
    return pl.pallas_call(
        functools.partial(_lru_scan_kernel, tc=tc, nc=nc),
        grid=(bsz // nb, nc),
        in_specs=[pl.BlockSpec(blk, fwd), pl.BlockSpec(blk, fwd),
                  pl.BlockSpec(blk, bwd), pl.BlockSpec(blk, bwd),
                  pl.BlockSpec((nb, 2, W_HALF), lambda b, c: (b, 0, 0))],
        out_specs=[pl.BlockSpec((nb, tc, W_HALF), lambda b, c: (b, c, 0)),
                   pl.BlockSpec((nb, tc, W_HALF), lambda b, c: (b, nc - 1 - c, 0)),
                   pl.BlockSpec((nb, 2, W_HALF), lambda b, c: (b, 0, 0))],
        out_shape=[seq_shape, seq_shape, jax.ShapeDtypeStruct((bsz, 2, W_HALF), F32)],
        scratch_shapes=[pltpu.VMEM((nb, 2, W_HALF), F32)],
        compiler_params=_cparams(("parallel", "arbitrary")),
        name="lru_scan",
    )(a, u, a, u, h0)


def _sublayers_tail(x, m, y, tail_refs):
    g1_ref, b1_ref, w1_ref, w2_ref, g2_ref, b2_ref = tail_refs
    x = _layer_norm(DN_ALPHA * x + m[2:3] * y, g1_ref[...], b1_ref[...])
    h = (x * (1.0 + m[4:5]) + m[3:4]).astype(BF16)
    y = None
    for c in range(D_FF // FF_CHUNK):
        t = jnp.dot(h, w1_ref[:, c * FF_CHUNK:(c + 1) * FF_CHUNK], preferred_element_type=F32)
        t = jnp.maximum(t, 0.0)
        part = _dot_bf16(t * t, w2_ref[c * FF_CHUNK:(c + 1) * FF_CHUNK, :])
        y = part if y is None else y + part
    return _layer_norm(DN_ALPHA * x + m[5:6] * y, g2_ref[...], b2_ref[...])


def _tail_specs(pw):
    const2 = lambda i: (0, 0)
    vec = pl.BlockSpec((1, D_MODEL), const2)
    args = [pw['ln1_g'], pw['ln1_b'], pw['mlp_w1'], pw['mlp_w2'], pw['ln2_g'], pw['ln2_b']]
    specs = [vec, vec,
             pl.BlockSpec(pw['mlp_w1'].shape, const2, pipeline_mode=pl.Buffered(1)),
             pl.BlockSpec(pw['mlp_w2'].shape, const2, pipeline_mode=pl.Buffered(1)),
             vec, vec]
    return args, specs


def _ab_out_kernel(x_ref, mod_ref, ya_ref, gg_ref, hf_ref, hb_ref, w_ref, *rest):
    *tail_refs, o_ref = rest
    yb = gg_ref[...] * (hf_ref[...] + hb_ref[...])
    y = _dot_bf16(ya_ref[...], w_ref[0:W_HALF, :]) + _dot_bf16(yb, w_ref[W_HALF:, :])
    o_ref[...] = _sublayers_tail(x_ref[...], mod_ref[0], y, tail_refs)


def _ab_out(x, mods, ya, gg, hf, hb, pw):
    n = x.shape[0]
    tpb = n // mods.shape[0] // TM
    tok = lambda i: (i, 0)
    half = pl.BlockSpec((TM, W_HALF), tok)
    tail_args, tail_specs = _tail_specs(pw)
    return pl.pallas_call(
        _ab_out_kernel,
        grid=(n // TM,),
        in_specs=[pl.BlockSpec((TM, D_MODEL), tok),
                  pl.BlockSpec((1, 6, D_MODEL), lambda i: (i // tpb, 0, 0)),
                  half, half, half, half,
                  pl.BlockSpec(pw['w_out'].shape, lambda i: (0, 0))] + tail_specs,
        out_specs=pl.BlockSpec((TM, D_MODEL), tok),
        out_shape=jax.ShapeDtypeStruct((n, D_MODEL), F32),
        compiler_params=_cparams(("parallel",)),
        name="ab_out_mlp",
    )(x, mods, ya, gg, hf, hb, pw['w_out'], *tail_args)


def _cd_in_kernel(x_ref, mod_ref, w_ref, hyc_ref, mu_ref, mux_ref, l1_ref, l2_ref, l0_ref,
                  kkp_ref, ka_ref, rk_ref, ones_ref,
                  hv_ref, hx1_ref, hx2_ref, r_ref, v_ref, kk_ref, w_out_ref, kd_ref, b_ref,
                  bonus_ref, sg_ref, *, line):
    x = x_ref[...]
    m = mod_ref[0]
    h = x * (1.0 + m[1:2]) + m[0:1]
    proj = _dot_bf16(h, w_ref[...])
    w = W_HALF
    u = _dwconv_rows(proj[:, 0:3 * w], hyc_ref[...], 1, line)
    hv_ref[...] = u[:, 0:w]
    hx1_ref[...] = u[:, w:2 * w]
    hx2_ref[...] = u[:, 2 * w:3 * w]

    mixed = []
    for n in range(4):
        t = proj[:, (3 + n) * w:(4 + n) * w]
        mixed.append(t + (_tshift_rows(t, line) - t) * mu_ref[n:n + 1])
    r, k, v, g = mixed
    dh = _tshift_rows(h, line) - h
    xw = h + dh * mux_ref[0:1]
    xa = h + dh * mux_ref[1:2]
    tw = jnp.tanh(_dot_bf16(xw, l1_ref[:, 0:128]))
    ta = _dot_bf16(xa, l1_ref[:, 128:256])
    zw = _dot_bf16(tw, l2_ref[0:128, :]) + l0_ref[0:1]
    za = _dot_bf16(ta, l2_ref[128:256, :]) + l0_ref[1:2]

    ones = ones_ref[...]
    kk = k * kkp_ref[...]
    kk = kk * lax.rsqrt(_group_sum(kk * kk, ones) + 1e-12)
    r_ref[...] = r
    v_ref[...] = v
    kk_ref[...] = kk
    kd_sum = None
    for d in range(2):
        w_raw = -_softplus(-zw[:, d * w:(d + 1) * w]) - 0.5
        w_out_ref[d] = jnp.exp(-jnp.exp(w_raw))
        a = _sigmoid(za[:, d * w:(d + 1) * w])
        kd = k * (1.0 + (a - 1.0) * ka_ref[...])
        kd_ref[d] = kd
        b_ref[d] = kk * a
        kd_sum = kd if kd_sum is None else kd_sum + kd
    bonus_ref[...] = _group_sum(r * kd_sum * rk_ref[...], ones) * v
    sg_ref[...] = _sigmoid(g)


def _cd_in(x, mods, w_in, hyc, mu, mux, l1, l2, l0, kkp, ka, rk, ones, line):
    n = x.shape[0]
    tpb = n // mods.shape[0] // TM
    tok = lambda i: (i, 0)
    const2 = lambda i: (0, 0)
    half = jax.ShapeDtypeStruct((n, W_HALF), F32)
    both = jax.ShapeDtypeStruct((2, n, W_HALF), F32)
    half_spec = pl.BlockSpec((TM, W_HALF), tok)
    both_spec = pl.BlockSpec((2, TM, W_HALF), lambda i: (0, i, 0))
    consts = [w_in, hyc, mu, mux, l1, l2, l0, kkp, ka, rk, ones]
    return pl.pallas_call(
        functools.partial(_cd_in_kernel, line=line),
        grid=(n // TM,),
        in_specs=[pl.BlockSpec((TM, D_MODEL), tok),
                  pl.BlockSpec((1, 6, D_MODEL), lambda i: (i // tpb, 0, 0))]
                 + [pl.BlockSpec(c.shape, const2) for c in consts],
        out_specs=[half_spec] * 6 + [both_spec] * 3 + [half_spec] * 2,
        out_shape=[half] * 6 + [both] * 3 + [half] * 2,
        compiler_params=_cparams(("parallel",)),
        name="cd_in",
    )(x, mods, *consts)


def _rwkv_scan_kernel(rf_ref, vf_ref, kkf_ref, wf_ref, kdf_ref, bf_ref,
                      rb_ref, vb_ref, kkb_ref, wb_ref, kdb_ref, bb_ref,
                      s0_ref, ones_ref, eye_ref,
                      of_ref, ob_ref, fin_ref, s_scr, sb_scr, o_scr, *, tc, nc, gb):
    c = pl.program_id(1)
    nsub = tc // RW_HEAD
    grp = 8
    per_sub = RW_HEAD // grp

    @pl.when(c == 0)
    def _():
        s_scr[...] = s0_ref[...]
        sb_scr[...] = s0_ref[...].astype(BF16)
        o_scr[...] = jnp.zeros_like(o_scr)

    ones = ones_ref[...]
    lane_pos = lax.broadcasted_iota(jnp.int32, (RW_HEAD, SCAN_LANES), 1) & (RW_HEAD - 1)
    chains = []
    for g in range(gb):
        chains.append((g, 0, rf_ref, vf_ref, kkf_ref, wf_ref, kdf_ref, bf_ref))
        chains.append((g, 1, rb_ref, vb_ref, kkb_ref, wb_ref, kdb_ref, bb_ref))

    hd = RW_HEAD

    tiles = [[(ch, pl.ds(j * SCAN_LANES, SCAN_LANES)) for ch in chains[c0:c0 + SCAN_CHAINS]]
             for c0 in range(0, len(chains), SCAN_CHAINS) for j in range(W_HALF // SCAN_LANES)]

    def put_outputs(streams, res_rows, step, base):
        for si, ((g, d, *_), ls) in enumerate(streams):
            pos = base + step if d == 0 else RW_HEAD - 1 - (base + step)
            o_scr[g, d, :, ls] = jnp.where(lane_pos == pos, res_rows[si * hd:(si + 1) * hd], o_scr[g, d, :, ls])

    def group(i, carry):
        sub = i // per_sub
        base = (i % per_sub) * grp
        row_f = pl.ds(pl.multiple_of(i * grp, grp), grp)
        row_b = pl.ds(pl.multiple_of(tc - grp - i * grp, grp), grp)

        def row(ref, g, d, ls, s, per_dir=False):
            q = s if d == 0 else grp - 1 - s
            tile = ref[0, g, row_b if d else row_f, ls] if per_dir else ref[g, row_b if d else row_f, ls]
            return tile[q:q + 1]

        def output_rows(streams, s):
            return [sb_scr[g, d, :, ls] * row(r_ref, g, d, ls, s).astype(BF16)
                    for ((g, d, r_ref, *_), ls) in streams]

        for s in range(grp):
            for streams in tiles:
                nst = len(streams)
                lhs = []
                for ((g, d, r_ref, v_ref, kk_ref, w_ref, kd_ref, b_ref), ls) in streams:
                    lhs.append(sb_scr[g, d, :, ls] * row(kk_ref, g, d, ls, s).astype(BF16))
                    lhs.append(eye_ref[:, ls] * row(v_ref, g, d, ls, s).astype(BF16))
                if s > 0:
                    lhs += output_rows(streams, s - 1)
                res = jnp.dot(jnp.concatenate(lhs, 0), ones, preferred_element_type=F32)
                for si, ((g, d, r_ref, v_ref, kk_ref, w_ref, kd_ref, b_ref), ls) in enumerate(streams):
                    sa = res[2 * si * hd:(2 * si + 1) * hd]
                    vcol = res[(2 * si + 1) * hd:(2 * si + 2) * hd]
                    st = (s_scr[g, d, :, ls] * row(w_ref, g, d, ls, s, True)
                          - sa * row(b_ref, g, d, ls, s, True) + vcol * row(kd_ref, g, d, ls, s, True))
                    s_scr[g, d, :, ls] = st
                    sb_scr[g, d, :, ls] = st.astype(BF16)
                if s > 0:
                    put_outputs(streams, res[2 * nst * hd:], s - 1, base)
        for streams in tiles:
            res = jnp.dot(jnp.concatenate(output_rows(streams, grp - 1), 0), ones, preferred_element_type=F32)
            put_outputs(streams, res, grp - 1, base)

        @pl.when(i % per_sub == per_sub - 1)
        def _():
            for g in range(gb):
                for d, (o_ref, blk) in enumerate(((of_ref, sub), (ob_ref, nsub - 1 - sub))):
                    ot = o_scr[g, d].T
                    for h in range(RW_HEADS):
                        o_ref[g, blk, :, h * hd:(h + 1) * hd] = ot[h * hd:(h + 1) * hd, :]

        return carry

    lax.fori_loop(0, tc // grp, group, 0)

    @pl.when(c == nc - 1)
    def _():
        fin_ref[...] = s_scr[...]


def _rwkv_scan(r, v, kk, w, kd, b, s0, ones2, eye, bsz, seq):
    tc = 64
    nc = seq // tc
    gb = min(bsz, SCAN_BATCH)
    r, v, kk = (t.reshape(bsz, seq, W_HALF) for t in (r, v, kk))
    w, kd, b = (t.reshape(2, bsz, seq, W_HALF) for t in (w, kd, b))
    shared = (gb, tc, W_HALF)
    perdir = (1, gb, tc, W_HALF)
    sf = lambda i, c: (i, c, 0)
    sb = lambda i, c: (i, nc - 1 - c, 0)
    pf = lambda i, c: (0, i, c, 0)
    pb = lambda i, c: (1, i, nc - 1 - c, 0)
    nsub = tc // RW_HEAD
    o_shape = jax.ShapeDtypeStruct((bsz, seq // RW_HEAD, RW_HEAD, W_HALF), F32)
    o_blk = (gb, nsub, RW_HEAD, W_HALF)
    st_blk = (gb, 2, RW_HEAD, W_HALF)
    return pl.pallas_call(
        functools.partial(_rwkv_scan_kernel, tc=tc, nc=nc, gb=gb),
        grid=(bsz // gb, nc),
        in_specs=[pl.BlockSpec(shared, sf)] * 3 + [pl.BlockSpec(perdir, pf)] * 3
                 + [pl.BlockSpec(shared, sb)] * 3 + [pl.BlockSpec(perdir, pb)] * 3
                 + [pl.BlockSpec(st_blk, lambda i, c: (i, 0, 0, 0)),
                    pl.BlockSpec(ones2.shape, lambda i, c: (0, 0)),
                    pl.BlockSpec(eye.shape, lambda i, c: (0, 0))],
        out_specs=[pl.BlockSpec(o_blk, lambda i, c: (i, c, 0, 0)),
                   pl.BlockSpec(o_blk, lambda i, c: (i, nc - 1 - c, 0, 0)),
                   pl.BlockSpec(st_blk, lambda i, c: (i, 0, 0, 0))],
        out_shape=[o_shape, o_shape, jax.ShapeDtypeStruct((bsz, 2, RW_HEAD, W_HALF), F32)],
        scratch_shapes=[pltpu.VMEM(st_blk, F32), pltpu.VMEM(st_blk, BF16), pltpu.VMEM(st_blk, F32)],
        compiler_params=_cparams(("parallel", "arbitrary")),
        name="rwkv_scan",
    )(r, v, kk, w, kd, b, r, v, kk, w, kd, b, s0, ones2, eye)


def _cd_out_kernel(x_ref, mod_ref, yc_ref, of_ref, ob_ref, bonus_ref, sg_ref, gng_ref, gnb_ref,
                   ones_ref, w_ref, *rest):
    *tail_refs, o_ref = rest
    ones = ones_ref[...]
    o = of_ref[...] + ob_ref[...]
    oc = o - _group_sum(o, ones) * (1.0 / RW_HEAD)
    on = oc * lax.rsqrt(_group_sum(oc * oc, ones) * (1.0 / RW_HEAD) + GN_EPS)
    yd = (on * gng_ref[...] + gnb_ref[...] + bonus_ref[...]) * sg_ref[...]
    y = _dot_bf16(yc_ref[...], w_ref[0:W_HALF, :]) + _dot_bf16(yd, w_ref[W_HALF:, :])
    o_ref[...] = _sublayers_tail(x_ref[...], mod_ref[0], y, tail_refs)


def _cd_out(x, mods, yc, of, ob, bonus, sg, ones, pw):
    n = x.shape[0]
    tpb = n // mods.shape[0] // TM
    tok = lambda i: (i, 0)
    const2 = lambda i: (0, 0)
    half = pl.BlockSpec((TM, W_HALF), tok)
    vec = pl.BlockSpec((1, W_HALF), const2)
    tail_args, tail_specs = _tail_specs(pw)
    return pl.pallas_call(
        _cd_out_kernel,
        grid=(n // TM,),
        in_specs=[pl.BlockSpec((TM, D_MODEL), tok),
                  pl.BlockSpec((1, 6, D_MODEL), lambda i: (i // tpb, 0, 0)),
                  half, half, half, half, half, vec, vec,
                  pl.BlockSpec(ones.shape, const2),
                  pl.BlockSpec(pw['w_out'].shape, const2)] + tail_specs,
        out_specs=pl.BlockSpec((TM, D_MODEL), tok),
        out_shape=jax.ShapeDtypeStruct((n, D_MODEL), F32),
        compiler_params=_cparams(("parallel",)),
        name="cd_out_mlp",
    )(x, mods, yc, of, ob, bonus, sg, pw['rw_gn_g'], pw['rw_gn_b'], ones, pw['w_out'], *tail_args)


def _hyena_consts(seq):
    tn = np.linspace(0.0, 1.0, seq, dtype=np.float32)
    tr = np.arange(seq, dtype=np.float32)
    bands = np.linspace(1e-4, HY_BANDS - 1, HY_BANDS, dtype=np.float32)
    ang = np.float32(2.0 * math.pi / seq) * tr[:, None] * bands[None, :]
    feats = np.concatenate([tn[:, None], np.cos(ang), -np.sin(ang)], -1).astype(np.float32)
    feats = np.pad(feats, ((0, 0), (0, 40 - feats.shape[1])))
    deltas = np.abs(np.linspace(math.log(HY_TARGET) / HY_LONG_PCT, math.log(HY_TARGET) / HY_SHORT_PCT,
                                W_HALF, dtype=np.float32))
    decay = np.exp(-tn[:, None] * deltas[None, :]).astype(np.float32)
    lag = np.concatenate([np.arange(seq), [0], np.arange(seq - 1, 0, -1)])
    return feats[lag], decay[lag]


def _filt_kernel(f_ref, dec_ref, w1_ref, b1_ref, w2_ref, b2_ref, w3_ref, fr_ref, o_ref, ssq_ref, *, half):
    i = pl.program_id(0)
    fr = fr_ref[...]
    hid = jnp.sin(fr * (jnp.dot(f_ref[...], w1_ref[...], precision=HIGHEST,
                                preferred_element_type=F32) + b1_ref[...]))
    hid = jnp.sin(fr * (jnp.dot(hid, w2_ref[...], precision=HIGHEST,
                                preferred_element_type=F32) + b2_ref[...]))
    raw = jnp.dot(hid, w3_ref[...], precision=HIGHEST, preferred_element_type=F32)
    dec = dec_ref[...]
    rows = raw.shape[0]
    backward = i >= half
    middle = (lax.broadcasted_iota(jnp.int32, (rows, W_HALF), 0) == 0) & (i == half)
    parts = []
    for order in range(2):
        fwd = raw[:, (2 * order) * W_HALF:(2 * order + 1) * W_HALF]
        bwd = raw[:, (2 * order + 1) * W_HALF:(2 * order + 2) * W_HALF]
        f = jnp.where(middle, 0.0, jnp.where(backward, bwd, fwd) * dec)
        o_ref[order] = f
        parts.append(jnp.sum(f * f, 0, keepdims=True))
    ssq = jnp.concatenate(parts, 0)

    @pl.when(i == 0)
    def _():
        ssq_ref[...] = ssq

    @pl.when(i > 0)
    def _():
        ssq_ref[...] += ssq


def _hyena_filters(seq, w1, b1, w2, b2, w3, freq):
    feats, decay = _hyena_consts(seq)
    w1 = jnp.pad(w1, ((0, 40 - w1.shape[0]), (0, 0)))
    rows = 256
    const2 = lambda i: (0, 0)
    tok = lambda i: (i, 0)
    vec = lambda a: a.reshape(1, -1)
    args = [jnp.asarray(feats), jnp.asarray(decay), w1, vec(b1), w2, vec(b2), w3, vec(freq)]
    return pl.pallas_call(
        functools.partial(_filt_kernel, half=seq // rows),
        grid=(2 * seq // rows,),
        in_specs=[pl.BlockSpec((rows, 40), tok), pl.BlockSpec((rows, W_HALF), tok)]
                 + [pl.BlockSpec(a.shape, const2) for a in args[2:]],
        out_specs=[pl.BlockSpec((2, rows, W_HALF), lambda i: (0, i, 0)), pl.BlockSpec((2, W_HALF), const2)],
        out_shape=[jax.ShapeDtypeStruct((2, 2 * seq, W_HALF), F32),
                   jax.ShapeDtypeStruct((2, W_HALF), F32)],
        compiler_params=_cparams(("arbitrary",)),
        name="hyena_filters",
    )(*args)


def _dft_consts_two_stage(seq):
    n = 2 * seq
    p = DFT_P
    q = n // p
    a = np.arange(q)[None, :]
    k1 = np.arange(q)[:, None]
    ang1 = 2.0 * np.pi * a * k1 / q
    f1 = np.empty((2 * q, q))
    f1[0::2] = np.cos(ang1)
    f1[1::2] = -np.sin(ang1)
    f3 = np.empty((q, 2 * q))
    f3[:, 0::2] = np.cos(ang1).T / n
    f3[:, 1::2] = -np.sin(ang1).T / n
    qq = np.arange(p)[None, None, :]
    k2 = np.arange(p)[None, :, None]
    kk1 = np.arange(q)[:, None, None]
    ang = 2.0 * np.pi * (qq * k2 / p + qq * kk1 / n)
    gr, gi = np.cos(ang), -np.sin(ang)
    g = np.concatenate([np.concatenate([gr, -gi], 2), np.concatenate([gi, gr], 2)], 1)
    grt, git = np.swapaxes(gr, 1, 2), np.swapaxes(gi, 1, 2)
    ginv = np.concatenate([np.concatenate([grt, git], 2), np.concatenate([-git, grt], 2)], 1)
    half = q // 2
    return (f1.astype(np.float32), f1[:, :half].astype(np.float32), f3[:half].astype(np.float32),
            g.astype(np.float32), ginv.astype(np.float32))


def _dft_consts_one_stage(seq):
    n = 2 * seq
    t = np.arange(n)[None, :]
    f = np.arange(n)[:, None]
    ang = 2.0 * np.pi * t * f / n
    fwd = np.concatenate([np.cos(ang), -np.sin(ang)], 0)
    inv = np.concatenate([np.cos(ang), -np.sin(ang)], 1)[:seq] / n
    return fwd.astype(np.float32), fwd[:, :seq].astype(np.float32), inv.astype(np.float32)


def _expand_rows(f, rows=DFT_ROWS):
    return np.kron(f, np.eye(rows, dtype=f.dtype))


def _stage_kernel(f_ref, x_ref, o_ref, *, exact):
    k, rows, ch = x_ref.shape[1:]
    x = x_ref[0].reshape(k * rows, ch)
    if exact:
        res = jnp.dot(f_ref[...], x, precision=HIGHEST, preferred_element_type=F32)
    else:
        res = _dot_bf16(f_ref[...], x.astype(BF16))
    o_ref[0] = res.reshape(o_ref.shape[1:]).astype(o_ref.dtype)


def _stage(f, x, exact=False):
    bsz, k, p, ch = x.shape
    rr = f.shape[0]
    rows = DFT_ROWS // 2 if exact else DFT_ROWS
    fx = jnp.asarray(_expand_rows(f, rows), dtype=F32 if exact else BF16)
    return pl.pallas_call(
        functools.partial(_stage_kernel, exact=exact),
        grid=(bsz, p // rows),
        in_specs=[pl.BlockSpec(fx.shape, lambda b, j: (0, 0)),
                  pl.BlockSpec((1, k, rows, ch), lambda b, j: (b, 0, j, 0))],
        out_specs=pl.BlockSpec((1, rr, rows, ch), lambda b, j: (b, 0, j, 0)),
        out_shape=jax.ShapeDtypeStruct((bsz, rr, p, ch), F32 if exact else BF16),
        compiler_params=_cparams(("parallel", "parallel")),
        name="dft_stage",
    )(fx, x)


def _spectrum_kernel(a_ref, g_ref, ssq_ref, o_ref):
    x = jnp.dot(g_ref[0], a_ref[0, 0], precision=HIGHEST, preferred_element_type=F32)
    o_ref[0, 0] = x * lax.rsqrt(ssq_ref[0] + 1e-6)


def _spectrum(a, g, ssq):
    nb, q, k, ch = a.shape
    rr = g.shape[1]
    return pl.pallas_call(
        _spectrum_kernel,
        grid=(nb, q),
        in_specs=[pl.BlockSpec((1, 1, k, ch), lambda b, j: (b, j, 0, 0)),
                  pl.BlockSpec((1, rr, k), lambda b, j: (j, 0, 0)),
                  pl.BlockSpec((1, 1, ch), lambda b, j: (b, 0, 0))],
        out_specs=pl.BlockSpec((1, 1, rr, ch), lambda b, j: (b, j, 0, 0)),
        out_shape=jax.ShapeDtypeStruct((nb, q, rr, ch), F32),
        compiler_params=_cparams(("parallel", "parallel")),
        name="hyena_spectrum",
    )(a, g, ssq)


def _spectral_conv(a, g_ref, h_ref, gi_ref):
    x = _dot_bf16(g_ref[0], a.astype(BF16))
    half = x.shape[0] // 2
    xr, xi = x[:half], x[half:]
    hr, hi = h_ref[0, 0, :half], h_ref[0, 0, half:]
    y = jnp.concatenate([xr * hr - xi * hi, xr * hi + xi * hr], 0)
    return _dot_bf16(gi_ref[0], y.astype(BF16))


def _mid_kernel(a_ref, g_ref, h_ref, gi_ref, o_ref):
    for b in range(a_ref.shape[0]):
        o_ref[b, 0] = _spectral_conv(a_ref[b, 0], g_ref, h_ref, gi_ref).astype(o_ref.dtype)


def _mid_gate_kernel(a_ref, g_ref, h_ref, gi_ref, hx_ref, bias_ref, o_ref):
    for b in range(a_ref.shape[0]):
        a = a_ref[b, 0]
        o_ref[b, 0] = hx_ref[b, 0] * (_spectral_conv(a, g_ref, h_ref, gi_ref) + bias_ref[...] * a)


def _mid(a, g, h, gi, order, gate=None, out_dtype=F32):
    bsz, q, k, ch = a.shape
    rr = g.shape[1]
    ro = gi.shape[1]
    nb = min(bsz, MID_BATCH)
    act = lambda j, b: (b, j, 0, 0)
    in_specs = [pl.BlockSpec((nb, 1, k, ch), act),
                pl.BlockSpec((1, rr, k), lambda j, b: (j, 0, 0)),
                pl.BlockSpec((1, 1, rr, ch), lambda j, b: (order, j, 0, 0)),
                pl.BlockSpec((1, ro, rr), lambda j, b: (j, 0, 0))]
    args = [a, g, h, gi]
    kern = _mid_kernel
    if gate is not None:
        in_specs += [pl.BlockSpec((nb, 1, ro, ch), act), pl.BlockSpec((1, ch), lambda j, b: (0, 0))]
        args += list(gate)
        kern = _mid_gate_kernel
    return pl.pallas_call(
        kern,
        grid=(q, bsz // nb),
        in_specs=in_specs,
        out_specs=pl.BlockSpec((nb, 1, ro, ch), act),
        out_shape=jax.ShapeDtypeStruct((bsz, q, ro, ch), out_dtype),
        compiler_params=_cparams(("parallel", "parallel")),
        name="hyena_mid",
    )(*args)


def _last_stage_kernel(f3_ref, bt_ref, hx_ref, z_ref, bias_ref, o_ref):
    k, rows, ch = bt_ref.shape[1:]
    conv = _dot_bf16(f3_ref[...], bt_ref[0].reshape(k * rows, ch))
    flat = conv.shape
    out = hx_ref[0].reshape(flat) * (conv + bias_ref[...] * z_ref[0].reshape(flat))
    o_ref[0] = out.reshape(o_ref.shape[1:])


def _last_stage(f3, bt, hx, z, bias):
    bsz, k, p, ch = bt.shape
    rr = f3.shape[0]
    fx = jnp.asarray(_expand_rows(f3), dtype=BF16)
    act = pl.BlockSpec((1, rr, DFT_ROWS, ch), lambda b, j: (b, 0, j, 0))
    return pl.pallas_call(
        _last_stage_kernel,
        grid=(bsz, p // DFT_ROWS),
        in_specs=[pl.BlockSpec(fx.shape, lambda b, j: (0, 0)),
                  pl.BlockSpec((1, k, DFT_ROWS, ch), lambda b, j: (b, 0, j, 0)),
                  act, act,
                  pl.BlockSpec((1, ch), lambda b, j: (0, 0))],
        out_specs=act,
        out_shape=jax.ShapeDtypeStruct((bsz, rr, p, ch), F32),
        compiler_params=_cparams(("parallel", "parallel")),
        name="dft_last_stage",
    )(fx, bt, hx, z, bias)


def _hyena(hv, hx1, hx2, circ, ssq, bias, bsz, seq):
    ch = W_HALF
    ssq = ssq.reshape(2, 1, ch)
    if seq <= 512:
        fwd_full, fwd_half, inv = _dft_consts_one_stage(seq)
        spec = _spectrum(circ[:, None], jnp.asarray(fwd_full)[None], ssq)
        fwd_half = jnp.asarray(fwd_half, dtype=BF16)[None]
        inv = jnp.asarray(inv, dtype=BF16)[None]
        shape4 = (bsz, 1, seq, ch)
        z = _mid(hv.reshape(shape4), fwd_half, spec, inv, 0, gate=(hx1.reshape(shape4), bias[0:1]))
        y = _mid(z, fwd_half, spec, inv, 1, gate=(hx2.reshape(shape4), bias[1:2]))
        return y.reshape(bsz * seq, ch)
    f1_full, f1_half, f3, g, ginv = _dft_consts_two_stage(seq)
    p = DFT_P
    q = 2 * seq // p
    spec = _spectrum(_stage(f1_full, circ.reshape(2, q, p, ch), exact=True).reshape(2, q, 2 * p, ch),
                     jnp.asarray(g), ssq)
    g = jnp.asarray(g, dtype=BF16)
    ginv = jnp.asarray(ginv, dtype=BF16)
    nat = (bsz, q // 2, p, ch)
    out = hv.reshape(nat)
    for order, hx in enumerate((hx1, hx2)):
        a = _stage(f1_half, out).reshape(bsz, q, 2 * p, ch)
        bt = _mid(a, g, spec, ginv, order, out_dtype=BF16).reshape(bsz, 2 * q, p, ch)
        out = _last_stage(f3, bt, hx.reshape(nat), out, bias[order:order + 1])
    return out.reshape(bsz * seq, ch)


def _block_diag(w):
    hh, blk, _ = w.shape
    eye = jnp.eye(hh, dtype=w.dtype)
    return jnp.einsum('hij,hg->higj', w, eye).reshape(hh * blk, hh * blk)


def _group_ones(width):
    idx = np.arange(width) // RW_HEAD
    return jnp.asarray((idx[:, None] == idx[None, :]).astype(np.float32), dtype=BF16)


def _even_layer(x, mods, h0, bsz, seq, line, pw):
    ya, gg, a, u = _ab_in(x, mods, pw['w_in'], pw['sc_conv'], pw['lru_conv'], pw['lru_conv_b'],
                          pw['wg'], pw['bg'], pw['nsp'], line)
    hf, hb, fin = _lru_scan(a, u, h0, bsz, seq)
    x = _ab_out(x, mods, ya, gg, hf.reshape(-1, W_HALF), hb.reshape(-1, W_HALF), pw)
    return x, fin


def _odd_layer(x, mods, s0, bsz, seq, line, pw):
    ones = _group_ones(W_HALF)
    (hv, hx1, hx2, r, v, kk, w, kd, b, bonus, sg) = _cd_in(
        x, mods, pw['w_in'], pw['hy_conv'], pw['rw_mu'], pw['rw_mu_x'], pw['l1'], pw['l2'], pw['l0'],
        pw['rw_kk'], pw['rw_ka'], pw['rw_rk'], ones, line)
    filt, ssq = _hyena_filters(seq, pw['hy_w1'], pw['hy_b1'], pw['hy_w2'], pw['hy_b2'], pw['hy_w3'],
                               pw['hy_freq'])
    yc = _hyena(hv, hx1, hx2, filt, ssq, pw['hy_bias'], bsz, seq)

    s0 = jnp.transpose(s0, (0, 1, 3, 2, 4)).reshape(bsz, 2, RW_HEAD, W_HALF)
    lane = np.arange(W_HALF)
    eye = jnp.asarray(((lane[None, :] % RW_HEAD) == np.arange(RW_HEAD)[:, None]).astype(np.float32),
                      dtype=BF16)
    of, ob, fin = _rwkv_scan(r, v, kk, w, kd, b, s0, _group_ones(SCAN_LANES), eye, bsz, seq)
    x = _cd_out(x, mods, yc, of.reshape(-1, W_HALF), ob.reshape(-1, W_HALF), bonus, sg, ones, pw)
    fin = jnp.transpose(fin.reshape(bsz, 2, RW_HEAD, RW_HEADS, RW_HEAD), (0, 1, 3, 2, 4))
    return x, fin


def _layer_weights(p, l):
    j = l // 2
    row = lambda a: a.reshape(1, -1)
    pw = {'w_out': p['w_out'][l].astype(BF16), 'ln1_g': row(p['ln1_g'][l]), 'ln1_b': row(p['ln1_b'][l]),
          'ln2_g': row(p['ln2_g'][l]), 'ln2_b': row(p['ln2_b'][l]),
          'mlp_w1': p['mlp_w1'][l].astype(BF16), 'mlp_w2': p['mlp_w2'][l].astype(BF16)}
    if l % 2 == 0:
        gates = [_block_diag(p[name][j, d]) for d in range(2) for name in ('lru_wa', 'lru_wi')]
        biases = [p[name][j, d] for d in range(2) for name in ('lru_ba', 'lru_bi')]
        pw.update({'w_in': p['ab_w_in'][j].astype(BF16), 'sc_conv': p['sc_conv'][j],
                   'lru_conv': p['lru_conv'][j], 'lru_conv_b': row(p['lru_conv_b'][j]),
                   'wg': jnp.concatenate(gates, 1).astype(BF16), 'bg': row(jnp.concatenate(biases)),
                   'nsp': jax.nn.softplus(-p['lru_lambda'][j])})
    else:
        zeros = jnp.zeros((64, W_HALF), F32)
        l2 = jnp.concatenate([
            jnp.concatenate([p['rw_w2'][j, 0], zeros], 1), jnp.concatenate([zeros, p['rw_w2'][j, 1]], 1),
            jnp.concatenate([p['rw_a2'][j, 0], zeros], 1), jnp.concatenate([zeros, p['rw_a2'][j, 1]], 1)], 0)
        pw.update({'w_in': p['cd_w_in'][j].astype(BF16), 'hy_conv': p['hy_conv'][j],
                   'rw_mu': p['rw_mu'][j], 'rw_mu_x': p['rw_mu_x'][j],
                   'l1': jnp.concatenate([p['rw_w1'][j, 0], p['rw_w1'][j, 1],
                                          p['rw_a1'][j, 0], p['rw_a1'][j, 1]], 1).astype(BF16),
                   'l2': l2.astype(BF16),
                   'l0': jnp.stack([p['rw_w0'][j].reshape(-1), p['rw_a0'][j].reshape(-1)]),
                   'rw_kk': row(p['rw_kk'][j]), 'rw_ka': row(p['rw_ka'][j]), 'rw_rk': row(p['rw_rk'][j]),
                   'rw_gn_g': row(p['rw_gn_g'][j]), 'rw_gn_b': row(p['rw_gn_b'][j]),
                   'hy_w1': p['hy_w1'][j], 'hy_b1': p['hy_b1'][j], 'hy_w2': p['hy_w2'][j],
                   'hy_b2': p['hy_b2'][j], 'hy_w3': p['hy_w3'][j], 'hy_freq': p['hy_freq'][j],
                   'hy_bias': p['hy_bias'][j]})
    return pw


def _to_colmajor(x, bsz, rows):
    return jnp.transpose(x.reshape(bsz, rows, GRID_W, D_MODEL), (0, 2, 1, 3)).reshape(-1, D_MODEL)


def _from_colmajor(x, bsz, rows):
    return jnp.transpose(x.reshape(bsz, GRID_W, rows, D_MODEL), (0, 2, 1, 3)).reshape(-1, D_MODEL)


def _trunk(x, mods, init_lru, init_rwkv, rows, weights):
    bsz, seq, _ = x.shape
    x = x.reshape(bsz * seq, D_MODEL)
    new_lru, new_rwkv = [], []
    for l in range(DEPTH):
        j = l // 2
        pw = weights[l]
        if l % 2 == 0:
            line = seq if rows is None else GRID_W
            x, st = _even_layer(x, mods[l], init_lru[:, j], bsz, seq, line, pw)
            new_lru.append(st)
        elif rows is None:
            x, st = _odd_layer(x, mods[l], init_rwkv[:, j], bsz, seq, seq, pw)
            new_rwkv.append(st)
        else:
            x = _to_colmajor(x, bsz, rows)
            x, st = _odd_layer(x, mods[l], init_rwkv[:, j], bsz, seq, rows, pw)
            x = _from_colmajor(x, bsz, rows)
            new_rwkv.append(st)
    return x.reshape(bsz, seq, D_MODEL), jnp.stack(new_lru, 1), jnp.stack(new_rwkv, 1)


def kernel(x_prompt, x_sample, state_lru, state_rwkv, c, c_ctx, w_mod, b_mod, ln1_g, ln1_b, ln2_g, ln2_b, mlp_w1, mlp_w2, w_out, ab_w_in, sc_conv, lru_conv, lru_conv_b, lru_wa, lru_ba, lru_wi, lru_bi, lru_lambda, cd_w_in, hy_conv, hy_w1, hy_b1, hy_w2, hy_b2, hy_w3, hy_freq, hy_bias, rw_mu, rw_mu_x, rw_w0, rw_w1, rw_w2, rw_a0, rw_a1, rw_a2, rw_kk, rw_ka, rw_rk, rw_gn_g, rw_gn_b):
    p = dict(ln1_g=ln1_g, ln1_b=ln1_b, ln2_g=ln2_g, ln2_b=ln2_b,
             mlp_w1=mlp_w1, mlp_w2=mlp_w2, w_out=w_out, ab_w_in=ab_w_in, sc_conv=sc_conv,
             lru_conv=lru_conv, lru_conv_b=lru_conv_b, lru_wa=lru_wa, lru_ba=lru_ba, lru_wi=lru_wi,
             lru_bi=lru_bi, lru_lambda=lru_lambda, cd_w_in=cd_w_in, hy_conv=hy_conv, hy_w1=hy_w1,
             hy_b1=hy_b1, hy_w2=hy_w2, hy_b2=hy_b2, hy_w3=hy_w3, hy_freq=hy_freq, hy_bias=hy_bias,
             rw_mu=rw_mu, rw_mu_x=rw_mu_x, rw_w0=rw_w0, rw_w1=rw_w1, rw_w2=rw_w2, rw_a0=rw_a0,
             rw_a1=rw_a1, rw_a2=rw_a2, rw_kk=rw_kk, rw_ka=rw_ka, rw_rk=rw_rk, rw_gn_g=rw_gn_g,
             rw_gn_b=rw_gn_b)
    weights = [_layer_weights(p, l) for l in range(DEPTH)]
    nb, dec = x_prompt.shape[0], x_sample.shape[0]
    rpad = -(1 + dec) % 8
    cvec = jnp.concatenate([c_ctx[None, :], c, jnp.zeros((rpad, D_MODEL), F32)], 0)
    mods = _mods(cvec, w_mod, b_mod)
    mods = jnp.transpose(mods, (0, 2, 1, 3))
    zero_lru = jnp.zeros((nb, (DEPTH + 1) // 2, 2, W_HALF), F32)
    zero_rwkv = jnp.zeros((nb, DEPTH // 2, 2, RW_HEADS, RW_HEAD, RW_HEAD), F32)
    y_prompt, new_lru, new_rwkv = _trunk(x_prompt, mods[:, 0:1], zero_lru, zero_rwkv, None, weights)
    rows = x_sample.shape[1] // GRID_W
    y_sample, _, _ = _trunk(x_sample, mods[:, 1:1 + dec], state_lru, state_rwkv, rows, weights)
    return (y_prompt, y_sample, new_lru, new_rwkv)
```

```python
import functools
import math

import jax
import jax.numpy as jnp
import numpy as np
from jax import lax
from jax.experimental import pallas as pl
from jax.experimental.pallas import tpu as pltpu

F32 = jnp.float32
BF16 = jnp.bfloat16
HIGHEST = lax.Precision.HIGHEST

D_MODEL = 1024
DEPTH = 4
GRID_W = 64
W_HALF = D_MODEL // 2
RG_C = 8.0
HY_BANDS = 16
HY_TARGET = 1e-2
HY_SHORT_PCT = 0.3
HY_LONG_PCT = 1.5
RW_HEAD = 64
RW_HEADS = W_HALF // RW_HEAD
D_FF = 4 * D_MODEL
DN_ALPHA = (2 * DEPTH) ** 0.25
LN_EPS = 1e-5
GN_EPS = 64e-5

LANES = 128
TM = 256
TM_TAIL = 512
VMEM_LIMIT = 56 * 1024 * 1024
DFT_P = 128
DFT_ROWS = 16
MID_BATCH = 8
FF_CHUNK = 1024
LRU_BATCH = 4
SCAN_LANES = 256
SCAN_BATCH = 8
SCAN_CHAINS = 8


def _cparams(sem):
    return pltpu.CompilerParams(dimension_semantics=sem, vmem_limit_bytes=VMEM_LIMIT)


def _shift_rows(x, off, line):
    if off == 0:
        return x
    n, width = x.shape
    rolled = pltpu.roll(x, (-off) % n, 0)
    pos = lax.broadcasted_iota(jnp.int32, (n, LANES), 0) & (line - 1)
    valid = (pos + off >= 0) if off < 0 else (pos + off < line)
    return jnp.concatenate([jnp.where(valid, rolled[:, j:j + LANES], 0.0) for j in range(0, width, LANES)], 1)


def _dwconv_rows(x, w, pad_left, line):
    out = None
    for k in range(w.shape[0]):
        term = _shift_rows(x, k - pad_left, line) * w[k:k + 1]
        out = term if out is None else out + term
    return out


def _tshift_rows(x, line):
    return 0.5 * (_shift_rows(x, -1, line) + _shift_rows(x, 1, line))


def _layer_norm(v, g, b):
    mu = jnp.mean(v, -1, keepdims=True)
    vc = v - mu
    var = jnp.mean(vc * vc, -1, keepdims=True)
    return vc * lax.rsqrt(var + LN_EPS) * g + b


def _dot_bf16(a, b):
    return jnp.dot(a.astype(BF16), b, preferred_element_type=F32)


def _group_sum(x, ones):
    hi = x.astype(BF16)
    lo = (x - hi.astype(F32)).astype(BF16)
    return (jnp.dot(hi, ones, preferred_element_type=F32)
            + jnp.dot(lo, ones, preferred_element_type=F32))


def _softplus(x):
    return jnp.maximum(x, 0.0) + jnp.log(1.0 + jnp.exp(-jnp.abs(x)))


def _sigmoid(x):
    return 1.0 / (1.0 + jnp.exp(-x))


def _mods_kernel(c_ref, w_ref, b_ref, o_ref):
    c = c_ref[...]
    s = c * _sigmoid(c)
    o_ref[0, 0] = jnp.dot(s, w_ref[0], precision=HIGHEST, preferred_element_type=F32) + b_ref[0, 0]


def _mods(cvec, w_mod, b_mod):
    r = cvec.shape[0]
    return pl.pallas_call(
        _mods_kernel,
        grid=(DEPTH, 6),
        in_specs=[pl.BlockSpec((r, D_MODEL), lambda l, n: (0, 0)),
                  pl.BlockSpec((1, D_MODEL, D_MODEL), lambda l, n: (l, 0, n)),
                  pl.BlockSpec((1, 1, 1, D_MODEL), lambda l, n: (l, n, 0, 0))],
        out_specs=pl.BlockSpec((1, 1, r, D_MODEL), lambda l, n: (l, n, 0, 0)),
        out_shape=jax.ShapeDtypeStruct((DEPTH, 6, r, D_MODEL), F32),
        compiler_params=_cparams(("parallel", "parallel")),
        name="mods",
    )(cvec, w_mod, b_mod.reshape(DEPTH, 6, 1, D_MODEL))


def _ab_in_kernel(x_ref, mod_ref, w_ref, scw_ref, lcw_ref, lcb_ref, wg_ref, bg_ref, nsp_ref,
                  ya_ref, gg_ref, a_ref, u_ref, *, line):
    x = x_ref[...]
    m = mod_ref[0]
    h = x * (1.0 + m[1:2]) + m[0:1]
    proj = _dot_bf16(h, w_ref[...])
    w = W_HALF
    s_b, s_c, s_v = proj[:, 0:w], proj[:, w:2 * w], proj[:, 2 * w:3 * w]
    g_lru, x_lru = proj[:, 3 * w:4 * w], proj[:, 4 * w:5 * w]
    ya_ref[...] = s_b * _dwconv_rows(s_c * s_v, scw_ref[...], 1, line)
    gg_ref[...] = 0.5 * g_lru * (1.0 + jnp.tanh(
        math.sqrt(2.0 / math.pi) * (g_lru + 0.044715 * (g_lru * g_lru * g_lru))))
    xc = _dwconv_rows(x_lru, lcw_ref[...], 2, line) + lcb_ref[...]
    gates = _dot_bf16(xc, wg_ref[...]) + bg_ref[...]
    for d in range(2):
        r = _sigmoid(gates[:, (2 * d) * w:(2 * d + 1) * w])
        i = _sigmoid(gates[:, (2 * d + 1) * w:(2 * d + 2) * w])
        a = jnp.exp(-RG_C * r * nsp_ref[d:d + 1])
        a_ref[d] = a
        u_ref[d] = jnp.sqrt(1.0 - a * a) * (i * xc)


def _ab_in(x, mods, w_in, scw, lcw, lcb, wg, bg, nsp, line):
    n = x.shape[0]
    tpb = n // mods.shape[0] // TM
    tok = lambda i: (i, 0)
    const2 = lambda i: (0, 0)
    half = jax.ShapeDtypeStruct((n, W_HALF), F32)
    both = jax.ShapeDtypeStruct((2, n, W_HALF), F32)
    return pl.pallas_call(
        functools.partial(_ab_in_kernel, line=line),
        grid=(n // TM,),
        in_specs=[pl.BlockSpec((TM, D_MODEL), tok),
                  pl.BlockSpec((1, 6, D_MODEL), lambda i: (i // tpb, 0, 0)),
                  pl.BlockSpec(w_in.shape, const2),
                  pl.BlockSpec(scw.shape, const2),
                  pl.BlockSpec(lcw.shape, const2),
                  pl.BlockSpec(lcb.shape, const2),
                  pl.BlockSpec(wg.shape, const2),
                  pl.BlockSpec(bg.shape, const2),
                  pl.BlockSpec(nsp.shape, const2)],
        out_specs=[pl.BlockSpec((TM, W_HALF), tok), pl.BlockSpec((TM, W_HALF), tok),
                   pl.BlockSpec((2, TM, W_HALF), lambda i: (0, i, 0)),
                   pl.BlockSpec((2, TM, W_HALF), lambda i: (0, i, 0))],
        out_shape=[half, half, both, both],
        compiler_params=_cparams(("parallel",)),
        name="ab_in",
    )(x, mods, w_in, scw, lcw, lcb, wg, bg, nsp)


def _lru_scan_kernel(af_ref, uf_ref, ab_ref, ub_ref, h0_ref, hf_ref, hb_ref, fin_ref, carry, *, tc, nc):
    c = pl.program_id(1)

    nb = af_ref.shape[1]

    @pl.when(c == 0)
    def _():
        carry[...] = h0_ref[...]

    def step(s, hs):
        tb = tc - 1 - s
        out = []
        for g in range(nb):
            hf, hb = hs[2 * g], hs[2 * g + 1]
            hf = af_ref[0, g, pl.ds(s, 1), :] * hf + uf_ref[0, g, pl.ds(s, 1), :]
            hb = ab_ref[0, g, pl.ds(tb, 1), :] * hb + ub_ref[0, g, pl.ds(tb, 1), :]
            hf_ref[g, pl.ds(s, 1), :] = hf
            hb_ref[g, pl.ds(tb, 1), :] = hb
            out += [hf, hb]
        return tuple(out)

    init = tuple(carry[g, d:d + 1] for g in range(nb) for d in range(2))
    hs = lax.fori_loop(0, tc, step, init, unroll=8)
    for g in range(nb):
        for d in range(2):
            carry[g, d:d + 1] = hs[2 * g + d]

    @pl.when(c == nc - 1)
    def _():
        fin_ref[...] = carry[...]


def _lru_scan(a, u, h0, bsz, seq):
    tc = min(seq, 512)
    nc = seq // tc
    nb = min(bsz, LRU_BATCH)
    a = a.reshape(2, bsz, seq, W_HALF)
    u = u.reshape(2, bsz, seq, W_HALF)
    fwd = lambda b, c: (0, b, c, 0)
    bwd = lambda b, c: (1, b, nc - 1 - c, 0)
    blk = (1, nb, tc, W_HALF)
    seq_shape = jax.ShapeDtypeStruct((bsz, seq, W_HALF), F32)
    return pl.pallas_call(
        functools.partial(_lru_scan_kernel, tc=tc, nc=nc),
        grid=(bsz // nb, nc),
        in_specs=[pl.BlockSpec(blk, fwd), pl.BlockSpec(blk, fwd),
                  pl.BlockSpec(blk, bwd), pl.BlockSpec(blk, bwd),
                  pl.BlockSpec((nb, 2, W_HALF), lambda b, c: (b, 0, 0))],
        out_specs=[pl.BlockSpec((nb, tc, W_HALF), lambda b, c: (b, c, 0)),
                   pl.BlockSpec((nb, tc, W_HALF), lambda b, c: (b, nc - 1 - c, 0)),
                   pl.BlockSpec((nb, 2, W_HALF), lambda b, c: (b, 0, 0))],
        out_shape=[seq_shape, seq_shape, jax.ShapeDtypeStruct((bsz, 2, W_HALF), F32)],
        scratch_shapes=[pltpu.VMEM((nb, 2, W_HALF), F32)],
        compiler_params=_cparams(("parallel", "arbitrary")),
        name="lru_scan",
    )(a, u, a, u, h0)


def _sublayers_tail(x, m, y, tail_refs):
    g1_ref, b1_ref, w1_ref, w2_ref, g2_ref, b2_ref = tail_refs
    x = _layer_norm(DN_ALPHA * x + m[2:3] * y, g1_ref[...], b1_ref[...])
    h = (x * (1.0 + m[4:5]) + m[3:4]).astype(BF16)
    y = None
    for c in range(D_FF // FF_CHUNK):
        t = jnp.dot(h, w1_ref[:, c * FF_CHUNK:(c + 1) * FF_CHUNK], preferred_element_type=F32)
        t = jnp.maximum(t, 0.0)
        part = _dot_bf16(t * t, w2_ref[c * FF_CHUNK:(c + 1) * FF_CHUNK, :])
        y = part if y is None else y + part
    return _layer_norm(DN_ALPHA * x + m[5:6] * y, g2_ref[...], b2_ref[...])


def _tail_specs(pw, n, mods):
    assert n % TM_TAIL == 0 and (mods.shape[0] == 1 or (n // mods.shape[0]) % TM_TAIL == 0), (n, mods.shape)
    const2 = lambda i: (0, 0)
    vec = pl.BlockSpec((1, D_MODEL), const2)
    args = [pw['ln1_g'], pw['ln1_b'], pw['mlp_w1'], pw['mlp_w2'], pw['ln2_g'], pw['ln2_b']]
    specs = [vec, vec,
             pl.BlockSpec(pw['mlp_w1'].shape, const2, pipeline_mode=pl.Buffered(1)),
             pl.BlockSpec(pw['mlp_w2'].shape, const2, pipeline_mode=pl.Buffered(1)),
             vec, vec]
    return args, specs


def _ab_out_kernel(x_ref, mod_ref, ya_ref, gg_ref, hf_ref, hb_ref, w_ref, *rest):
    *tail_refs, o_ref = rest
    yb = gg_ref[...] * (hf_ref[...] + hb_ref[...])
    y = _dot_bf16(ya_ref[...], w_ref[0:W_HALF, :]) + _dot_bf16(yb, w_ref[W_HALF:, :])
    o_ref[...] = _sublayers_tail(x_ref[...], mod_ref[0], y, tail_refs)


def _ab_out(x, mods, ya, gg, hf, hb, pw):
    n = x.shape[0]
    tpb = n // mods.shape[0] // TM_TAIL
    tok = lambda i: (i, 0)
    half = pl.BlockSpec((TM_TAIL, W_HALF), tok)
    tail_args, tail_specs = _tail_specs(pw, n, mods)
    return pl.pallas_call(
        _ab_out_kernel,
        grid=(n // TM_TAIL,),
        in_specs=[pl.BlockSpec((TM_TAIL, D_MODEL), tok),
                  pl.BlockSpec((1, 6, D_MODEL), lambda i: (i // tpb, 0, 0)),
                  half, half, half, half,
                  pl.BlockSpec(pw['w_out'].shape, lambda i: (0, 0))] + tail_specs,
        out_specs=pl.BlockSpec((TM_TAIL, D_MODEL), tok),
        out_shape=jax.ShapeDtypeStruct((n, D_MODEL), F32),
        compiler_params=_cparams(("parallel",)),
        name="ab_out_mlp",
    )(x, mods, ya, gg, hf, hb, pw['w_out'], *tail_args)


def _cd_in_kernel(x_ref, mod_ref, w_ref, hyc_ref, mu_ref, mux_ref, l1_ref, l2_ref, l0_ref,
                  kkp_ref, ka_ref, rk_ref, ones_ref,
                  hv_ref, hx1_ref, hx2_ref, r_ref, v_ref, kk_ref, w_out_ref, kd_ref, b_ref,
                  bonus_ref, sg_ref, *, line):
    x = x_ref[...]
    m = mod_ref[0]
    h = x * (1.0 + m[1:2]) + m[0:1]
    proj = _dot_bf16(h, w_ref[...])
    w = W_HALF
    u = _dwconv_rows(proj[:, 0:3 * w], hyc_ref[...], 1, line)
    hv_ref[...] = u[:, 0:w]
    hx1_ref[...] = u[:, w:2 * w]
    hx2_ref[...] = u[:, 2 * w:3 * w]

    mixed = []
    for n in range(4):
        t = proj[:, (3 + n) * w:(4 + n) * w]
        mixed.append(t + (_tshift_rows(t, line) - t) * mu_ref[n:n + 1])
    r, k, v, g = mixed
    dh = _tshift_rows(h, line) - h
    xw = h + dh * mux_ref[0:1]
    xa = h + dh * mux_ref[1:2]
    tw = jnp.tanh(_dot_bf16(xw, l1_ref[:, 0:128]))
    ta = _dot_bf16(xa, l1_ref[:, 128:256])
    zw = _dot_bf16(tw, l2_ref[0:128, :]) + l0_ref[0:1]
    za = _dot_bf16(ta, l2_ref[128:256, :]) + l0_ref[1:2]

    ones = ones_ref[...]
    kk = k * kkp_ref[...]
    kk = kk * lax.rsqrt(_group_sum(kk * kk, ones) + 1e-12)
    r_ref[...] = r
    v_ref[...] = v
    kk_ref[...] = kk
    kd_sum = None
    for d in range(2):
        w_raw = -_softplus(-zw[:, d * w:(d + 1) * w]) - 0.5
        w_out_ref[d] = jnp.exp(-jnp.exp(w_raw))
        a = _sigmoid(za[:, d * w:(d + 1) * w])
        kd = k * (1.0 + (a - 1.0) * ka_ref[...])
        kd_ref[d] = kd
        b_ref[d] = kk * a
        kd_sum = kd if kd_sum is None else kd_sum + kd
    bonus_ref[...] = _group_sum(r * kd_sum * rk_ref[...], ones) * v
    sg_ref[...] = _sigmoid(g)


def _cd_in(x, mods, w_in, hyc, mu, mux, l1, l2, l0, kkp, ka, rk, ones, line):
    n = x.shape[0]
    tpb = n // mods.shape[0] // TM
    tok = lambda i: (i, 0)
    const2 = lambda i: (0, 0)
    half = jax.ShapeDtypeStruct((n, W_HALF), F32)
    both = jax.ShapeDtypeStruct((2, n, W_HALF), F32)
    half_spec = pl.BlockSpec((TM, W_HALF), tok)
    both_spec = pl.BlockSpec((2, TM, W_HALF), lambda i: (0, i, 0))
    consts = [w_in, hyc, mu, mux, l1, l2, l0, kkp, ka, rk, ones]
    return pl.pallas_call(
        functools.partial(_cd_in_kernel, line=line),
        grid=(n // TM,),
        in_specs=[pl.BlockSpec((TM, D_MODEL), tok),
                  pl.BlockSpec((1, 6, D_MODEL), lambda i: (i // tpb, 0, 0))]
                 + [pl.BlockSpec(c.shape, const2) for c in consts],
        out_specs=[half_spec] * 6 + [both_spec] * 3 + [half_spec] * 2,
        out_shape=[half] * 6 + [both] * 3 + [half] * 2,
        compiler_params=_cparams(("parallel",)),
        name="cd_in",
    )(x, mods, *consts)


def _rwkv_scan_kernel(rf_ref, vf_ref, kkf_ref, wf_ref, kdf_ref, bf_ref,
                      rb_ref, vb_ref, kkb_ref, wb_ref, kdb_ref, bb_ref,
                      s0_ref, ones_ref, eye_ref,
                      of_ref, ob_ref, fin_ref, s_scr, sb_scr, o_scr, *, tc, nc, gb):
    c = pl.program_id(1)
    nsub = tc // RW_HEAD
    grp = 8
    per_sub = RW_HEAD // grp

    @pl.when(c == 0)
    def _():
        s_scr[...] = s0_ref[...]
        sb_scr[...] = s0_ref[...].astype(BF16)
        o_scr[...] = jnp.zeros_like(o_scr)

    ones = ones_ref[...]
    lane_pos = lax.broadcasted_iota(jnp.int32, (RW_HEAD, SCAN_LANES), 1) & (RW_HEAD - 1)
    chains = []
    for g in range(gb):
        chains.append((g, 0, rf_ref, vf_ref, kkf_ref, wf_ref, kdf_ref, bf_ref))
        chains.append((g, 1, rb_ref, vb_ref, kkb_ref, wb_ref, kdb_ref, bb_ref))

    hd = RW_HEAD

    tiles = [[(ch, pl.ds(j * SCAN_LANES, SCAN_LANES)) for ch in chains[c0:c0 + SCAN_CHAINS]]
             for c0 in range(0, len(chains), SCAN_CHAINS) for j in range(W_HALF // SCAN_LANES)]

    def put_outputs(streams, res_rows, step, base):
        for si, ((g, d, *_), ls) in enumerate(streams):
            pos = base + step if d == 0 else RW_HEAD - 1 - (base + step)
            o_scr[g, d, :, ls] = jnp.where(lane_pos == pos, res_rows[si * hd:(si + 1) * hd], o_scr[g, d, :, ls])

    def group(i, carry):
        sub = i // per_sub
        base = (i % per_sub) * grp
        row_f = pl.ds(pl.multiple_of(i * grp, grp), grp)
        row_b = pl.ds(pl.multiple_of(tc - grp - i * grp, grp), grp)

        def row(ref, g, d, ls, s, per_dir=False):
            q = s if d == 0 else grp - 1 - s
            tile = ref[0, g, row_b if d else row_f, ls] if per_dir else ref[g, row_b if d else row_f, ls]
            return tile[q:q + 1]

        def output_rows(streams, s):
            return [sb_scr[g, d, :, ls] * row(r_ref, g, d, ls, s).astype(BF16)
                    for ((g, d, r_ref, *_), ls) in streams]

        for s in range(grp):
            for streams in tiles:
                nst = len(streams)
                lhs = []
                for ((g, d, r_ref, v_ref, kk_ref, w_ref, kd_ref, b_ref), ls) in streams:
                    lhs.append(sb_scr[g, d, :, ls] * row(kk_ref, g, d, ls, s).astype(BF16))
                    lhs.append(eye_ref[:, ls] * row(v_ref, g, d, ls, s).astype(BF16))
                if s > 0:
                    lhs += output_rows(streams, s - 1)
                res = jnp.dot(jnp.concatenate(lhs, 0), ones, preferred_element_type=F32)
                for si, ((g, d, r_ref, v_ref, kk_ref, w_ref, kd_ref, b_ref), ls) in enumerate(streams):
                    sa = res[2 * si * hd:(2 * si + 1) * hd]
                    vcol = res[(2 * si + 1) * hd:(2 * si + 2) * hd]
                    st = (s_scr[g, d, :, ls] * row(w_ref, g, d, ls, s, True)
                          - sa * row(b_ref, g, d, ls, s, True) + vcol * row(kd_ref, g, d, ls, s, True))
                    s_scr[g, d, :, ls] = st
                    sb_scr[g, d, :, ls] = st.astype(BF16)
                if s > 0:
                    put_outputs(streams, res[2 * nst * hd:], s - 1, base)
        for streams in tiles:
            res = jnp.dot(jnp.concatenate(output_rows(streams, grp - 1), 0), ones, preferred_element_type=F32)
            put_outputs(streams, res, grp - 1, base)

        @pl.when(i % per_sub == per_sub - 1)
        def _():
            for g in range(gb):
                for d, (o_ref, blk) in enumerate(((of_ref, sub), (ob_ref, nsub - 1 - sub))):
                    ot = o_scr[g, d].T
                    for h in range(RW_HEADS):
                        o_ref[g, blk, :, h * hd:(h + 1) * hd] = ot[h * hd:(h + 1) * hd, :]

        return carry

    lax.fori_loop(0, tc // grp, group, 0)

    @pl.when(c == nc - 1)
    def _():
        fin_ref[...] = s_scr[...]


def _rwkv_scan(r, v, kk, w, kd, b, s0, ones2, eye, bsz, seq):
    tc = 64
    nc = seq // tc
    gb = min(bsz, SCAN_BATCH)
    r, v, kk = (t.reshape(bsz, seq, W_HALF) for t in (r, v, kk))
    w, kd, b = (t.reshape(2, bsz, seq, W_HALF) for t in (w, kd, b))
    shared = (gb, tc, W_HALF)
    perdir = (1, gb, tc, W_HALF)
    sf = lambda i, c: (i, c, 0)
    sb = lambda i, c: (i, nc - 1 - c, 0)
    pf = lambda i, c: (0, i, c, 0)
    pb = lambda i, c: (1, i, nc - 1 - c, 0)
    nsub = tc // RW_HEAD
    o_shape = jax.ShapeDtypeStruct((bsz, seq // RW_HEAD, RW_HEAD, W_HALF), F32)
    o_blk = (gb, nsub, RW_HEAD, W_HALF)
    st_blk = (gb, 2, RW_HEAD, W_HALF)
    return pl.pallas_call(
        functools.partial(_rwkv_scan_kernel, tc=tc, nc=nc, gb=gb),
        grid=(bsz // gb, nc),
        in_specs=[pl.BlockSpec(shared, sf)] * 3 + [pl.BlockSpec(perdir, pf)] * 3
                 + [pl.BlockSpec(shared, sb)] * 3 + [pl.BlockSpec(perdir, pb)] * 3
                 + [pl.BlockSpec(st_blk, lambda i, c: (i, 0, 0, 0)),
                    pl.BlockSpec(ones2.shape, lambda i, c: (0, 0)),
                    pl.BlockSpec(eye.shape, lambda i, c: (0, 0))],
        out_specs=[pl.BlockSpec(o_blk, lambda i, c: (i, c, 0, 0)),
                   pl.BlockSpec(o_blk, lambda i, c: (i, nc - 1 - c, 0, 0)),
                   pl.BlockSpec(st_blk, lambda i, c: (i, 0, 0, 0))],
        out_shape=[o_shape, o_shape, jax.ShapeDtypeStruct((bsz, 2, RW_HEAD, W_HALF), F32)],
        scratch_shapes=[pltpu.VMEM(st_blk, F32), pltpu.VMEM(st_blk, BF16), pltpu.VMEM(st_blk, F32)],
        compiler_params=_cparams(("parallel", "arbitrary")),
        name="rwkv_scan",
    )(r, v, kk, w, kd, b, r, v, kk, w, kd, b, s0, ones2, eye)


def _cd_out_kernel(x_ref, mod_ref, yc_ref, of_ref, ob_ref, bonus_ref, sg_ref, gng_ref, gnb_ref,
                   ones_ref, w_ref, *rest):
    *tail_refs, o_ref = rest
    ones = ones_ref[...]
    o = of_ref[...] + ob_ref[...]
    oc = o - _group_sum(o, ones) * (1.0 / RW_HEAD)
    on = oc * lax.rsqrt(_group_sum(oc * oc, ones) * (1.0 / RW_HEAD) + GN_EPS)
    yd = (on * gng_ref[...] + gnb_ref[...] + bonus_ref[...]) * sg_ref[...]
    y = _dot_bf16(yc_ref[...], w_ref[0:W_HALF, :]) + _dot_bf16(yd, w_ref[W_HALF:, :])
    o_ref[...] = _sublayers_tail(x_ref[...], mod_ref[0], y, tail_refs)


def _cd_out(x, mods, yc, of, ob, bonus, sg, ones, pw):
    n = x.shape[0]
    tpb = n // mods.shape[0] // TM_TAIL
    tok = lambda i: (i, 0)
    const2 = lambda i: (0, 0)
    half = pl.BlockSpec((TM_TAIL, W_HALF), tok)
    vec = pl.BlockSpec((1, W_HALF), const2)
    tail_args, tail_specs = _tail_specs(pw, n, mods)
    return pl.pallas_call(
        _cd_out_kernel,
        grid=(n // TM_TAIL,),
        in_specs=[pl.BlockSpec((TM_TAIL, D_MODEL), tok),
                  pl.BlockSpec((1, 6, D_MODEL), lambda i: (i // tpb, 0, 0)),
                  half, half, half, half, half, vec, vec,
                  pl.BlockSpec(ones.shape, const2),
                  pl.BlockSpec(pw['w_out'].shape, const2)] + tail_specs,
        out_specs=pl.BlockSpec((TM_TAIL, D_MODEL), tok),
        out_shape=jax.ShapeDtypeStruct((n, D_MODEL), F32),
        compiler_params=_cparams(("parallel",)),
        name="cd_out_mlp",
    )(x, mods, yc, of, ob, bonus, sg, pw['rw_gn_g'], pw['rw_gn_b'], ones, pw['w_out'], *tail_args)


def _hyena_consts(seq):
    tn = np.linspace(0.0, 1.0, seq, dtype=np.float32)
    tr = np.arange(seq, dtype=np.float32)
    bands = np.linspace(1e-4, HY_BANDS - 1, HY_BANDS, dtype=np.float32)
    ang = np.float32(2.0 * math.pi / seq) * tr[:, None] * bands[None, :]
    feats = np.concatenate([tn[:, None], np.cos(ang), -np.sin(ang)], -1).astype(np.float32)
    feats = np.pad(feats, ((0, 0), (0, 40 - feats.shape[1])))
    deltas = np.abs(np.linspace(math.log(HY_TARGET) / HY_LONG_PCT, math.log(HY_TARGET) / HY_SHORT_PCT,
                                W_HALF, dtype=np.float32))
    decay = np.exp(-tn[:, None] * deltas[None, :]).astype(np.float32)
    lag = np.concatenate([np.arange(seq), [0], np.arange(seq - 1, 0, -1)])
    return feats[lag], decay[lag]


def _filt_kernel(f_ref, dec_ref, w1_ref, b1_ref, w2_ref, b2_ref, w3_ref, fr_ref, o_ref, ssq_ref, *, half):
    i = pl.program_id(0)
    fr = fr_ref[...]
    hid = jnp.sin(fr * (jnp.dot(f_ref[...], w1_ref[...], precision=HIGHEST,
                                preferred_element_type=F32) + b1_ref[...]))
    hid = jnp.sin(fr * (jnp.dot(hid, w2_ref[...], precision=HIGHEST,
                                preferred_element_type=F32) + b2_ref[...]))
    raw = jnp.dot(hid, w3_ref[...], precision=HIGHEST, preferred_element_type=F32)
    dec = dec_ref[...]
    rows = raw.shape[0]
    backward = i >= half
    middle = (lax.broadcasted_iota(jnp.int32, (rows, W_HALF), 0) == 0) & (i == half)
    parts = []
    for order in range(2):
        fwd = raw[:, (2 * order) * W_HALF:(2 * order + 1) * W_HALF]
        bwd = raw[:, (2 * order + 1) * W_HALF:(2 * order + 2) * W_HALF]
        f = jnp.where(middle, 0.0, jnp.where(backward, bwd, fwd) * dec)
        o_ref[order] = f
        parts.append(jnp.sum(f * f, 0, keepdims=True))
    ssq = jnp.concatenate(parts, 0)

    @pl.when(i == 0)
    def _():
        ssq_ref[...] = ssq

    @pl.when(i > 0)
    def _():
        ssq_ref[...] += ssq


def _hyena_filters(seq, w1, b1, w2, b2, w3, freq):
    feats, decay = _hyena_consts(seq)
    w1 = jnp.pad(w1, ((0, 40 - w1.shape[0]), (0, 0)))
    rows = 256
    const2 = lambda i: (0, 0)
    tok = lambda i: (i, 0)
    vec = lambda a: a.reshape(1, -1)
    args = [jnp.asarray(feats), jnp.asarray(decay), w1, vec(b1), w2, vec(b2), w3, vec(freq)]
    return pl.pallas_call(
        functools.partial(_filt_kernel, half=seq // rows),
        grid=(2 * seq // rows,),
        in_specs=[pl.BlockSpec((rows, 40), tok), pl.BlockSpec((rows, W_HALF), tok)]
                 + [pl.BlockSpec(a.shape, const2) for a in args[2:]],
        out_specs=[pl.BlockSpec((2, rows, W_HALF), lambda i: (0, i, 0)), pl.BlockSpec((2, W_HALF), const2)],
        out_shape=[jax.ShapeDtypeStruct((2, 2 * seq, W_HALF), F32),
                   jax.ShapeDtypeStruct((2, W_HALF), F32)],
        compiler_params=_cparams(("arbitrary",)),
        name="hyena_filters",
    )(*args)


def _dft_consts_two_stage(seq):
    n = 2 * seq
    p = DFT_P
    q = n // p
    a = np.arange(q)[None, :]
    k1 = np.arange(q)[:, None]
    ang1 = 2.0 * np.pi * a * k1 / q
    f1 = np.empty((2 * q, q))
    f1[0::2] = np.cos(ang1)
    f1[1::2] = -np.sin(ang1)
    f3 = np.empty((q, 2 * q))
    f3[:, 0::2] = np.cos(ang1).T / n
    f3[:, 1::2] = -np.sin(ang1).T / n
    qq = np.arange(p)[None, None, :]
    k2 = np.arange(p)[None, :, None]
    kk1 = np.arange(q)[:, None, None]
    ang = 2.0 * np.pi * (qq * k2 / p + qq * kk1 / n)
    gr, gi = np.cos(ang), -np.sin(ang)
    g = np.concatenate([np.concatenate([gr, -gi], 2), np.concatenate([gi, gr], 2)], 1)
    grt, git = np.swapaxes(gr, 1, 2), np.swapaxes(gi, 1, 2)
    ginv = np.concatenate([np.concatenate([grt, git], 2), np.concatenate([-git, grt], 2)], 1)
    half = q // 2
    return (f1.astype(np.float32), f1[:, :half].astype(np.float32), f3[:half].astype(np.float32),
            g.astype(np.float32), ginv.astype(np.float32))


def _dft_consts_one_stage(seq):
    n = 2 * seq
    t = np.arange(n)[None, :]
    f = np.arange(n)[:, None]
    ang = 2.0 * np.pi * t * f / n
    fwd = np.concatenate([np.cos(ang), -np.sin(ang)], 0)
    inv = np.concatenate([np.cos(ang), -np.sin(ang)], 1)[:seq] / n
    return fwd.astype(np.float32), fwd[:, :seq].astype(np.float32), inv.astype(np.float32)


def _expand_rows(f, rows=DFT_ROWS):
    return np.kron(f, np.eye(rows, dtype=f.dtype))


def _stage_kernel(f_ref, x_ref, o_ref, *, exact):
    k, rows, ch = x_ref.shape[1:]
    x = x_ref[0].reshape(k * rows, ch)
    if exact:
        res = jnp.dot(f_ref[...], x, precision=HIGHEST, preferred_element_type=F32)
    else:
        res = _dot_bf16(f_ref[...], x.astype(BF16))
    o_ref[0] = res.reshape(o_ref.shape[1:]).astype(o_ref.dtype)


def _stage(f, x, exact=False):
    bsz, k, p, ch = x.shape
    rr = f.shape[0]
    rows = DFT_ROWS // 2 if exact else DFT_ROWS
    fx = jnp.asarray(_expand_rows(f, rows), dtype=F32 if exact else BF16)
    return pl.pallas_call(
        functools.partial(_stage_kernel, exact=exact),
        grid=(bsz, p // rows),
        in_specs=[pl.BlockSpec(fx.shape, lambda b, j: (0, 0)),
                  pl.BlockSpec((1, k, rows, ch), lambda b, j: (b, 0, j, 0))],
        out_specs=pl.BlockSpec((1, rr, rows, ch), lambda b, j: (b, 0, j, 0)),
        out_shape=jax.ShapeDtypeStruct((bsz, rr, p, ch), F32 if exact else BF16),
        compiler_params=_cparams(("parallel", "parallel")),
        name="dft_stage",
    )(fx, x)


def _spectrum_kernel(a_ref, g_ref, ssq_ref, o_ref):
    x = jnp.dot(g_ref[0], a_ref[0, 0], precision=HIGHEST, preferred_element_type=F32)
    o_ref[0, 0] = x * lax.rsqrt(ssq_ref[0] + 1e-6)


def _spectrum(a, g, ssq):
    nb, q, k, ch = a.shape
    rr = g.shape[1]
    return pl.pallas_call(
        _spectrum_kernel,
        grid=(nb, q),
        in_specs=[pl.BlockSpec((1, 1, k, ch), lambda b, j: (b, j, 0, 0)),
                  pl.BlockSpec((1, rr, k), lambda b, j: (j, 0, 0)),
                  pl.BlockSpec((1, 1, ch), lambda b, j: (b, 0, 0))],
        out_specs=pl.BlockSpec((1, 1, rr, ch), lambda b, j: (b, j, 0, 0)),
        out_shape=jax.ShapeDtypeStruct((nb, q, rr, ch), F32),
        compiler_params=_cparams(("parallel", "parallel")),
        name="hyena_spectrum",
    )(a, g, ssq)


def _spectral_conv(a, g_ref, h_ref, gi_ref):
    x = _dot_bf16(g_ref[0], a.astype(BF16))
    half = x.shape[0] // 2
    xr, xi = x[:half], x[half:]
    hr, hi = h_ref[0, 0, :half], h_ref[0, 0, half:]
    y = jnp.concatenate([xr * hr - xi * hi, xr * hi + xi * hr], 0)
    return _dot_bf16(gi_ref[0], y.astype(BF16))


def _mid_kernel(a_ref, g_ref, h_ref, gi_ref, o_ref):
    for b in range(a_ref.shape[0]):
        o_ref[b, 0] = _spectral_conv(a_ref[b, 0], g_ref, h_ref, gi_ref).astype(o_ref.dtype)


def _mid_gate_kernel(a_ref, g_ref, h_ref, gi_ref, hx_ref, bias_ref, o_ref):
    for b in range(a_ref.shape[0]):
        a = a_ref[b, 0]
        o_ref[b, 0] = hx_ref[b, 0] * (_spectral_conv(a, g_ref, h_ref, gi_ref) + bias_ref[...] * a)


def _mid(a, g, h, gi, order, gate=None, out_dtype=F32):
    bsz, q, k, ch = a.shape
    rr = g.shape[1]
    ro = gi.shape[1]
    nb = min(bsz, MID_BATCH)
    act = lambda j, b: (b, j, 0, 0)
    in_specs = [pl.BlockSpec((nb, 1, k, ch), act),
                pl.BlockSpec((1, rr, k), lambda j, b: (j, 0, 0)),
                pl.BlockSpec((1, 1, rr, ch), lambda j, b: (order, j, 0, 0)),
                pl.BlockSpec((1, ro, rr), lambda j, b: (j, 0, 0))]
    args = [a, g, h, gi]
    kern = _mid_kernel
    if gate is not None:
        in_specs += [pl.BlockSpec((nb, 1, ro, ch), act), pl.BlockSpec((1, ch), lambda j, b: (0, 0))]
        args += list(gate)
        kern = _mid_gate_kernel
    return pl.pallas_call(
        kern,
        grid=(q, bsz // nb),
        in_specs=in_specs,
        out_specs=pl.BlockSpec((nb, 1, ro, ch), act),
        out_shape=jax.ShapeDtypeStruct((bsz, q, ro, ch), out_dtype),
        compiler_params=_cparams(("parallel", "parallel")),
        name="hyena_mid",
    )(*args)


def _last_stage_kernel(f3_ref, bt_ref, hx_ref, z_ref, bias_ref, o_ref):
    k, rows, ch = bt_ref.shape[1:]
    conv = _dot_bf16(f3_ref[...], bt_ref[0].reshape(k * rows, ch))
    flat = conv.shape
    out = hx_ref[0].reshape(flat) * (conv + bias_ref[...] * z_ref[0].reshape(flat))
    o_ref[0] = out.reshape(o_ref.shape[1:])


def _last_stage(f3, bt, hx, z, bias):
    bsz, k, p, ch = bt.shape
    rr = f3.shape[0]
    fx = jnp.asarray(_expand_rows(f3), dtype=BF16)
    act = pl.BlockSpec((1, rr, DFT_ROWS, ch), lambda b, j: (b, 0, j, 0))
    return pl.pallas_call(
        _last_stage_kernel,
        grid=(bsz, p // DFT_ROWS),
        in_specs=[pl.BlockSpec(fx.shape, lambda b, j: (0, 0)),
                  pl.BlockSpec((1, k, DFT_ROWS, ch), lambda b, j: (b, 0, j, 0)),
                  act, act,
                  pl.BlockSpec((1, ch), lambda b, j: (0, 0))],
        out_specs=act,
        out_shape=jax.ShapeDtypeStruct((bsz, rr, p, ch), F32),
        compiler_params=_cparams(("parallel", "parallel")),
        name="dft_last_stage",
    )(fx, bt, hx, z, bias)


def _hyena(hv, hx1, hx2, circ, ssq, bias, bsz, seq):
    ch = W_HALF
    ssq = ssq.reshape(2, 1, ch)
    if seq <= 512:
        fwd_full, fwd_half, inv = _dft_consts_one_stage(seq)
        spec = _spectrum(circ[:, None], jnp.asarray(fwd_full)[None], ssq)
        fwd_half = jnp.asarray(fwd_half, dtype=BF16)[None]
        inv = jnp.asarray(inv, dtype=BF16)[None]
        shape4 = (bsz, 1, seq, ch)
        z = _mid(hv.reshape(shape4), fwd_half, spec, inv, 0, gate=(hx1.reshape(shape4), bias[0:1]))
        y = _mid(z, fwd_half, spec, inv, 1, gate=(hx2.reshape(shape4), bias[1:2]))
        return y.reshape(bsz * seq, ch)
    f1_full, f1_half, f3, g, ginv = _dft_consts_two_stage(seq)
    p = DFT_P
    q = 2 * seq // p
    spec = _spectrum(_stage(f1_full, circ.reshape(2, q, p, ch), exact=True).reshape(2, q, 2 * p, ch),
                     jnp.asarray(g), ssq)
    g = jnp.asarray(g, dtype=BF16)
    ginv = jnp.asarray(ginv, dtype=BF16)
    nat = (bsz, q // 2, p, ch)
    out = hv.reshape(nat)
    for order, hx in enumerate((hx1, hx2)):
        a = _stage(f1_half, out).reshape(bsz, q, 2 * p, ch)
        bt = _mid(a, g, spec, ginv, order, out_dtype=BF16).reshape(bsz, 2 * q, p, ch)
        out = _last_stage(f3, bt, hx.reshape(nat), out, bias[order:order + 1])
    return out.reshape(bsz * seq, ch)


def _block_diag(w):
    hh, blk, _ = w.shape
    eye = jnp.eye(hh, dtype=w.dtype)
    return jnp.einsum('hij,hg->higj', w, eye).reshape(hh * blk, hh * blk)


def _group_ones(width):
    idx = np.arange(width) // RW_HEAD
    return jnp.asarray((idx[:, None] == idx[None, :]).astype(np.float32), dtype=BF16)


def _even_layer(x, mods, h0, bsz, seq, line, pw):
    ya, gg, a, u = _ab_in(x, mods, pw['w_in'], pw['sc_conv'], pw['lru_conv'], pw['lru_conv_b'],
                          pw['wg'], pw['bg'], pw['nsp'], line)
    hf, hb, fin = _lru_scan(a, u, h0, bsz, seq)
    x = _ab_out(x, mods, ya, gg, hf.reshape(-1, W_HALF), hb.reshape(-1, W_HALF), pw)
    return x, fin


def _odd_layer(x, mods, s0, bsz, seq, line, pw):
    ones = _group_ones(W_HALF)
    (hv, hx1, hx2, r, v, kk, w, kd, b, bonus, sg) = _cd_in(
        x, mods, pw['w_in'], pw['hy_conv'], pw['rw_mu'], pw['rw_mu_x'], pw['l1'], pw['l2'], pw['l0'],
        pw['rw_kk'], pw['rw_ka'], pw['rw_rk'], ones, line)
    filt, ssq = _hyena_filters(seq, pw['hy_w1'], pw['hy_b1'], pw['hy_w2'], pw['hy_b2'], pw['hy_w3'],
                               pw['hy_freq'])
    yc = _hyena(hv, hx1, hx2, filt, ssq, pw['hy_bias'], bsz, seq)

    s0 = jnp.transpose(s0, (0, 1, 3, 2, 4)).reshape(bsz, 2, RW_HEAD, W_HALF)
    lane = np.arange(W_HALF)
    eye = jnp.asarray(((lane[None, :] % RW_HEAD) == np.arange(RW_HEAD)[:, None]).astype(np.float32),
                      dtype=BF16)
    of, ob, fin = _rwkv_scan(r, v, kk, w, kd, b, s0, _group_ones(SCAN_LANES), eye, bsz, seq)
    x = _cd_out(x, mods, yc, of.reshape(-1, W_HALF), ob.reshape(-1, W_HALF), bonus, sg, ones, pw)
    fin = jnp.transpose(fin.reshape(bsz, 2, RW_HEAD, RW_HEADS, RW_HEAD), (0, 1, 3, 2, 4))
    return x, fin


def _layer_weights(p, l):
    j = l // 2
    row = lambda a: a.reshape(1, -1)
    pw = {'w_out': p['w_out'][l].astype(BF16), 'ln1_g': row(p['ln1_g'][l]), 'ln1_b': row(p['ln1_b'][l]),
          'ln2_g': row(p['ln2_g'][l]), 'ln2_b': row(p['ln2_b'][l]),
          'mlp_w1': p['mlp_w1'][l].astype(BF16), 'mlp_w2': p['mlp_w2'][l].astype(BF16)}
    if l % 2 == 0:
        gates = [_block_diag(p[name][j, d]) for d in range(2) for name in ('lru_wa', 'lru_wi')]
        biases = [p[name][j, d] for d in range(2) for name in ('lru_ba', 'lru_bi')]
        pw.update({'w_in': p['ab_w_in'][j].astype(BF16), 'sc_conv': p['sc_conv'][j],
                   'lru_conv': p['lru_conv'][j], 'lru_conv_b': row(p['lru_conv_b'][j]),
                   'wg': jnp.concatenate(gates, 1).astype(BF16), 'bg': row(jnp.concatenate(biases)),
                   'nsp': jax.nn.softplus(-p['lru_lambda'][j])})
    else:
        zeros = jnp.zeros((64, W_HALF), F32)
        l2 = jnp.concatenate([
            jnp.concatenate([p['rw_w2'][j, 0], zeros], 1), jnp.concatenate([zeros, p['rw_w2'][j, 1]], 1),
            jnp.concatenate([p['rw_a2'][j, 0], zeros], 1), jnp.concatenate([zeros, p['rw_a2'][j, 1]], 1)], 0)
        pw.update({'w_in': p['cd_w_in'][j].astype(BF16), 'hy_conv': p['hy_conv'][j],
                   'rw_mu': p['rw_mu'][j], 'rw_mu_x': p['rw_mu_x'][j],
                   'l1': jnp.concatenate([p['rw_w1'][j, 0], p['rw_w1'][j, 1],
                                          p['rw_a1'][j, 0], p['rw_a1'][j, 1]], 1).astype(BF16),
                   'l2': l2.astype(BF16),
                   'l0': jnp.stack([p['rw_w0'][j].reshape(-1), p['rw_a0'][j].reshape(-1)]),
                   'rw_kk': row(p['rw_kk'][j]), 'rw_ka': row(p['rw_ka'][j]), 'rw_rk': row(p['rw_rk'][j]),
                   'rw_gn_g': row(p['rw_gn_g'][j]), 'rw_gn_b': row(p['rw_gn_b'][j]),
                   'hy_w1': p['hy_w1'][j], 'hy_b1': p['hy_b1'][j], 'hy_w2': p['hy_w2'][j],
                   'hy_b2': p['hy_b2'][j], 'hy_w3': p['hy_w3'][j], 'hy_freq': p['hy_freq'][j],
                   'hy_bias': p['hy_bias'][j]})
    return pw


def _to_colmajor(x, bsz, rows):
    return jnp.transpose(x.reshape(bsz, rows, GRID_W, D_MODEL), (0, 2, 1, 3)).reshape(-1, D_MODEL)


def _from_colmajor(x, bsz, rows):
    return jnp.transpose(x.reshape(bsz, GRID_W, rows, D_MODEL), (0, 2, 1, 3)).reshape(-1, D_MODEL)


def _trunk(x, mods, init_lru, init_rwkv, rows, weights):
    bsz, seq, _ = x.shape
    x = x.reshape(bsz * seq, D_MODEL)
    new_lru, new_rwkv = [], []
    for l in range(DEPTH):
        j = l // 2
        pw = weights[l]
        if l % 2 == 0:
            line = seq if rows is None else GRID_W
            x, st = _even_layer(x, mods[l], init_lru[:, j], bsz, seq, line, pw)
            new_lru.append(st)
        elif rows is None:
            x, st = _odd_layer(x, mods[l], init_rwkv[:, j], bsz, seq, seq, pw)
            new_rwkv.append(st)
        else:
            x = _to_colmajor(x, bsz, rows)
            x, st = _odd_layer(x, mods[l], init_rwkv[:, j], bsz, seq, rows, pw)
            x = _from_colmajor(x, bsz, rows)
            new_rwkv.append(st)
    return x.reshape(bsz, seq, D_MODEL), jnp.stack(new_lru, 1), jnp.stack(new_rwkv, 1)


def kernel(x_prompt, x_sample, state_lru, state_rwkv, c, c_ctx, w_mod, b_mod, ln1_g, ln1_b, ln2_g, ln2_b, mlp_w1, mlp_w2, w_out, ab_w_in, sc_conv, lru_conv, lru_conv_b, lru_wa, lru_ba, lru_wi, lru_bi, lru_lambda, cd_w_in, hy_conv, hy_w1, hy_b1, hy_w2, hy_b2, hy_w3, hy_freq, hy_bias, rw_mu, rw_mu_x, rw_w0, rw_w1, rw_w2, rw_a0, rw_a1, rw_a2, rw_kk, rw_ka, rw_rk, rw_gn_g, rw_gn_b):
    p = dict(ln1_g=ln1_g, ln1_b=ln1_b, ln2_g=ln2_g, ln2_b=ln2_b,
             mlp_w1=mlp_w1, mlp_w2=mlp_w2, w_out=w_out, ab_w_in=ab_w_in, sc_conv=sc_conv,
             lru_conv=lru_conv, lru_conv_b=lru_conv_b, lru_wa=lru_wa, lru_ba=lru_ba, lru_wi=lru_wi,
             lru_bi=lru_bi, lru_lambda=lru_lambda, cd_w_in=cd_w_in, hy_conv=hy_conv, hy_w1=hy_w1,
             hy_b1=hy_b1, hy_w2=hy_w2, hy_b2=hy_b2, hy_w3=hy_w3, hy_freq=hy_freq, hy_bias=hy_bias,
             rw_mu=rw_mu, rw_mu_x=rw_mu_x, rw_w0=rw_w0, rw_w1=rw_w1, rw_w2=rw_w2, rw_a0=rw_a0,
             rw_a1=rw_a1, rw_a2=rw_a2, rw_kk=rw_kk, rw_ka=rw_ka, rw_rk=rw_rk, rw_gn_g=rw_gn_g,
             rw_gn_b=rw_gn_b)
    weights = [_layer_weights(p, l) for l in range(DEPTH)]
    nb, dec = x_prompt.shape[0], x_sample.shape[0]
    rpad = -(1 + dec) % 8
    cvec = jnp.concatenate([c_ctx[None, :], c, jnp.zeros((rpad, D_MODEL), F32)], 0)
    mods = _mods(cvec, w_mod, b_mod)
    mods = jnp.transpose(mods, (0, 2, 1, 3))
    zero_lru = jnp.zeros((nb, (DEPTH + 1) // 2, 2, W_HALF), F32)
    zero_rwkv = jnp.zeros((nb, DEPTH // 2, 2, RW_HEADS, RW_HEAD, RW_HEAD), F32)
    y_prompt, new_lru, new_rwkv = _trunk(x_prompt, mods[:, 0:1], zero_lru, zero_rwkv, None, weights)
    rows = x_sample.shape[1] // GRID_W
    y_sample, _, _ = _trunk(x_sample, mods[:, 1:1 + dec], state_lru, state_rwkv, rows, weights)
    return (y_prompt, y_sample, new_lru, new_rwkv)
```

```python
import functools
import math

import jax
import jax.numpy as jnp
import numpy as np
from jax import lax
from jax.experimental import pallas as pl
from jax.experimental.pallas import tpu as pltpu

F32 = jnp.float32
BF16 = jnp.bfloat16
HIGHEST = lax.Precision.HIGHEST

D_MODEL = 1024
DEPTH = 4
GRID_W = 64
W_HALF = D_MODEL // 2
RG_C = 8.0
HY_BANDS = 16
HY_TARGET = 1e-2
HY_SHORT_PCT = 0.3
HY_LONG_PCT = 1.5
RW_HEAD = 64
RW_HEADS = W_HALF // RW_HEAD
D_FF = 4 * D_MODEL
DN_ALPHA = (2 * DEPTH) ** 0.25
LN_EPS = 1e-5
GN_EPS = 64e-5

LANES = 128
TM = 256
TM_TAIL = 512
VMEM_LIMIT = 56 * 1024 * 1024
DFT_P = 128
DFT_ROWS = 16
MID_BATCH = 8
FF_CHUNK = 1024
LRU_BATCH = 4
MXU_TILE = 256
SCAN_LANES = MXU_TILE
SCAN_BATCH = 8
SCAN_CHAINS = 8


def _cparams(sem):
    return pltpu.CompilerParams(dimension_semantics=sem, vmem_limit_bytes=VMEM_LIMIT)


def _shift_rows(x, off, line):
    if off == 0:
        return x
    n, width = x.shape
    rolled = pltpu.roll(x, (-off) % n, 0)
    pos = lax.broadcasted_iota(jnp.int32, (n, LANES), 0) & (line - 1)
    valid = (pos + off >= 0) if off < 0 else (pos + off < line)
    return jnp.concatenate([jnp.where(valid, rolled[:, j:j + LANES], 0.0) for j in range(0, width, LANES)], 1)


def _dwconv_rows(x, w, pad_left, line):
    out = None
    for k in range(w.shape[0]):
        term = _shift_rows(x, k - pad_left, line) * w[k:k + 1]
        out = term if out is None else out + term
    return out


def _tshift_rows(x, line):
    return 0.5 * (_shift_rows(x, -1, line) + _shift_rows(x, 1, line))


def _layer_norm(v, g, b):
    mu = jnp.mean(v, -1, keepdims=True)
    vc = v - mu
    var = jnp.mean(vc * vc, -1, keepdims=True)
    return vc * lax.rsqrt(var + LN_EPS) * g + b


def _dot_bf16(a, b):
    return jnp.dot(a.astype(BF16), b, preferred_element_type=F32)


def _group_sum(x, ones):
    hi = x.astype(BF16)
    lo = (x - hi.astype(F32)).astype(BF16)
    blk = ones.shape[0]
    parts = [jnp.dot(hi[:, j:j + blk], ones, preferred_element_type=F32)
             + jnp.dot(lo[:, j:j + blk], ones, preferred_element_type=F32) for j in range(0, x.shape[1], blk)]
    return jnp.concatenate(parts, 1)


def _softplus(x):
    return jnp.maximum(x, 0.0) + jnp.log(1.0 + jnp.exp(-jnp.abs(x)))


def _sigmoid(x):
    return 1.0 / (1.0 + jnp.exp(-x))


def _mods_kernel(c_ref, w_ref, b_ref, o_ref):
    c = c_ref[...]
    s = c * _sigmoid(c)
    o_ref[0, 0] = jnp.dot(s, w_ref[0], precision=HIGHEST, preferred_element_type=F32) + b_ref[0, 0]


def _mods(cvec, w_mod, b_mod):
    r = cvec.shape[0]
    return pl.pallas_call(
        _mods_kernel,
        grid=(DEPTH, 6),
        in_specs=[pl.BlockSpec((r, D_MODEL), lambda l, n: (0, 0)),
                  pl.BlockSpec((1, D_MODEL, D_MODEL), lambda l, n: (l, 0, n)),
                  pl.BlockSpec((1, 1, 1, D_MODEL), lambda l, n: (l, n, 0, 0))],
        out_specs=pl.BlockSpec((1, 1, r, D_MODEL), lambda l, n: (l, n, 0, 0)),
        out_shape=jax.ShapeDtypeStruct((DEPTH, 6, r, D_MODEL), F32),
        compiler_params=_cparams(("parallel", "parallel")),
        name="mods",
    )(cvec, w_mod, b_mod.reshape(DEPTH, 6, 1, D_MODEL))


def _ab_in_kernel(x_ref, mod_ref, w_ref, scw_ref, lcw_ref, lcb_ref, wg_ref, bg_ref, nsp_ref,
                  ya_ref, gg_ref, a_ref, u_ref, *, line):
    x = x_ref[...]
    m = mod_ref[0]
    h = x * (1.0 + m[1:2]) + m[0:1]
    proj = _dot_bf16(h, w_ref[...])
    w = W_HALF
    s_b, s_c, s_v = proj[:, 0:w], proj[:, w:2 * w], proj[:, 2 * w:3 * w]
    g_lru, x_lru = proj[:, 3 * w:4 * w], proj[:, 4 * w:5 * w]
    ya_ref[...] = s_b * _dwconv_rows(s_c * s_v, scw_ref[...], 1, line)
    gg_ref[...] = 0.5 * g_lru * (1.0 + jnp.tanh(
        math.sqrt(2.0 / math.pi) * (g_lru + 0.044715 * (g_lru * g_lru * g_lru))))
    xc = _dwconv_rows(x_lru, lcw_ref[...], 2, line) + lcb_ref[...]
    gates = _dot_bf16(xc, wg_ref[...]) + bg_ref[...]
    for d in range(2):
        r = _sigmoid(gates[:, (2 * d) * w:(2 * d + 1) * w])
        i = _sigmoid(gates[:, (2 * d + 1) * w:(2 * d + 2) * w])
        a = jnp.exp(-RG_C * r * nsp_ref[d:d + 1])
        a_ref[d] = a
        u_ref[d] = jnp.sqrt(1.0 - a * a) * (i * xc)


def _ab_in(x, mods, w_in, scw, lcw, lcb, wg, bg, nsp, line):
    n = x.shape[0]
    tpb = n // mods.shape[0] // TM
    tok = lambda i: (i, 0)
    const2 = lambda i: (0, 0)
    half = jax.ShapeDtypeStruct((n, W_HALF), F32)
    both = jax.ShapeDtypeStruct((2, n, W_HALF), F32)
    return pl.pallas_call(
        functools.partial(_ab_in_kernel, line=line),
        grid=(n // TM,),
        in_specs=[pl.BlockSpec((TM, D_MODEL), tok),
                  pl.BlockSpec((1, 6, D_MODEL), lambda i: (i // tpb, 0, 0)),
                  pl.BlockSpec(w_in.shape, const2),
                  pl.BlockSpec(scw.shape, const2),
                  pl.BlockSpec(lcw.shape, const2),
                  pl.BlockSpec(lcb.shape, const2),
                  pl.BlockSpec(wg.shape, const2),
                  pl.BlockSpec(bg.shape, const2),
                  pl.BlockSpec(nsp.shape, const2)],
        out_specs=[pl.BlockSpec((TM, W_HALF), tok), pl.BlockSpec((TM, W_HALF), tok),
                   pl.BlockSpec((2, TM, W_HALF), lambda i: (0, i, 0)),
                   pl.BlockSpec((2, TM, W_HALF), lambda i: (0, i, 0))],
        out_shape=[half, half, both, both],
        compiler_params=_cparams(("parallel",)),
        name="ab_in",
    )(x, mods, w_in, scw, lcw, lcb, wg, bg, nsp)


def _lru_scan_kernel(af_ref, uf_ref, ab_ref, ub_ref, h0_ref, hf_ref, hb_ref, fin_ref, carry, *, tc, nc):
    c = pl.program_id(1)

    nb = af_ref.shape[1]

    @pl.when(c == 0)
    def _():
        carry[...] = h0_ref[...]

    def step(s, hs):
        tb = tc - 1 - s
        out = []
        for g in range(nb):
            hf, hb = hs[2 * g], hs[2 * g + 1]
            hf = af_ref[0, g, pl.ds(s, 1), :] * hf + uf_ref[0, g, pl.ds(s, 1), :]
            hb = ab_ref[0, g, pl.ds(tb, 1), :] * hb + ub_ref[0, g, pl.ds(tb, 1), :]
            hf_ref[g, pl.ds(s, 1), :] = hf
            hb_ref[g, pl.ds(tb, 1), :] = hb
            out += [hf, hb]
        return tuple(out)

    init = tuple(carry[g, d:d + 1] for g in range(nb) for d in range(2))
    hs = lax.fori_loop(0, tc, step, init, unroll=8)
    for g in range(nb):
        for d in range(2):
            carry[g, d:d + 1] = hs[2 * g + d]

    @pl.when(c == nc - 1)
    def _():
        fin_ref[...] = carry[...]


def _lru_scan(a, u, h0, bsz, seq):
    tc = min(seq, 512)
    nc = seq // tc
    nb = min(bsz, LRU_BATCH)
    a = a.reshape(2, bsz, seq, W_HALF)
    u = u.reshape(2, bsz, seq, W_HALF)
    fwd = lambda b, c: (0, b, c, 0)
    bwd = lambda b, c: (1, b, nc - 1 - c, 0)
    blk = (1, nb, tc, W_HALF)
    seq_shape = jax.ShapeDtypeStruct((bsz, seq, W_HALF), F32)
    return pl.pallas_call(
        functools.partial(_lru_scan_kernel, tc=tc, nc=nc),
        grid=(bsz // nb, nc),
        in_specs=[pl.BlockSpec(blk, fwd), pl.BlockSpec(blk, fwd),
                  pl.BlockSpec(blk, bwd), pl.BlockSpec(blk, bwd),
                  pl.BlockSpec((nb, 2, W_HALF), lambda b, c: (b, 0, 0))],
        out_specs=[pl.BlockSpec((nb, tc, W_HALF), lambda b, c: (b, c, 0)),
                   pl.BlockSpec((nb, tc, W_HALF), lambda b, c: (b, nc - 1 - c, 0)),
                   pl.BlockSpec((nb, 2, W_HALF), lambda b, c: (b, 0, 0))],
        out_shape=[seq_shape, seq_shape, jax.ShapeDtypeStruct((bsz, 2, W_HALF), F32)],
        scratch_shapes=[pltpu.VMEM((nb, 2, W_HALF), F32)],
        compiler_params=_cparams(("parallel", "arbitrary")),
        name="lru_scan",
    )(a, u, a, u, h0)


def _sublayers_tail(x, m, y, tail_refs):
    g1_ref, b1_ref, w1_ref, w2_ref, g2_ref, b2_ref = tail_refs
    x = _layer_norm(DN_ALPHA * x + m[2:3] * y, g1_ref[...], b1_ref[...])
    h = (x * (1.0 + m[4:5]) + m[3:4]).astype(BF16)
    y = None
    for c in range(D_FF // FF_CHUNK):
        t = jnp.dot(h, w1_ref[:, c * FF_CHUNK:(c + 1) * FF_CHUNK], preferred_element_type=F32)
        t = jnp.maximum(t, 0.0)
        part = _dot_bf16(t * t, w2_ref[c * FF_CHUNK:(c + 1) * FF_CHUNK, :])
        y = part if y is None else y + part
    return _layer_norm(DN_ALPHA * x + m[5:6] * y, g2_ref[...], b2_ref[...])


def _tail_specs(pw, n, mods):
    assert n % TM_TAIL == 0 and (mods.shape[0] == 1 or (n // mods.shape[0]) % TM_TAIL == 0), (n, mods.shape)
    const2 = lambda i: (0, 0)
    vec = pl.BlockSpec((1, D_MODEL), const2)
    args = [pw['ln1_g'], pw['ln1_b'], pw['mlp_w1'], pw['mlp_w2'], pw['ln2_g'], pw['ln2_b']]
    specs = [vec, vec,
             pl.BlockSpec(pw['mlp_w1'].shape, const2, pipeline_mode=pl.Buffered(1)),
             pl.BlockSpec(pw['mlp_w2'].shape, const2, pipeline_mode=pl.Buffered(1)),
             vec, vec]
    return args, specs


def _ab_out_kernel(x_ref, mod_ref, ya_ref, gg_ref, hf_ref, hb_ref, w_ref, *rest):
    *tail_refs, o_ref = rest
    yb = gg_ref[...] * (hf_ref[...] + hb_ref[...])
    y = _dot_bf16(ya_ref[...], w_ref[0:W_HALF, :]) + _dot_bf16(yb, w_ref[W_HALF:, :])
    o_ref[...] = _sublayers_tail(x_ref[...], mod_ref[0], y, tail_refs)


def _ab_out(x, mods, ya, gg, hf, hb, pw):
    n = x.shape[0]
    tpb = n // mods.shape[0] // TM_TAIL
    tok = lambda i: (i, 0)
    half = pl.BlockSpec((TM_TAIL, W_HALF), tok)
    tail_args, tail_specs = _tail_specs(pw, n, mods)
    return pl.pallas_call(
        _ab_out_kernel,
        grid=(n // TM_TAIL,),
        in_specs=[pl.BlockSpec((TM_TAIL, D_MODEL), tok),
                  pl.BlockSpec((1, 6, D_MODEL), lambda i: (i // tpb, 0, 0)),
                  half, half, half, half,
                  pl.BlockSpec(pw['w_out'].shape, lambda i: (0, 0))] + tail_specs,
        out_specs=pl.BlockSpec((TM_TAIL, D_MODEL), tok),
        out_shape=jax.ShapeDtypeStruct((n, D_MODEL), F32),
        compiler_params=_cparams(("parallel",)),
        name="ab_out_mlp",
    )(x, mods, ya, gg, hf, hb, pw['w_out'], *tail_args)


def _cd_in_kernel(x_ref, mod_ref, w_ref, hyc_ref, mu_ref, mux_ref, l1_ref, l2_ref, l0_ref,
                  kkp_ref, ka_ref, rk_ref, ones_ref,
                  hv_ref, hx1_ref, hx2_ref, r_ref, v_ref, kk_ref, w_out_ref, kd_ref, b_ref,
                  bonus_ref, sg_ref, *, line):
    x = x_ref[...]
    m = mod_ref[0]
    h = x * (1.0 + m[1:2]) + m[0:1]
    proj = _dot_bf16(h, w_ref[...])
    w = W_HALF
    u = _dwconv_rows(proj[:, 0:3 * w], hyc_ref[...], 1, line)
    hv_ref[...] = u[:, 0:w]
    hx1_ref[...] = u[:, w:2 * w]
    hx2_ref[...] = u[:, 2 * w:3 * w]

    mixed = []
    for n in range(4):
        t = proj[:, (3 + n) * w:(4 + n) * w]
        mixed.append(t + (_tshift_rows(t, line) - t) * mu_ref[n:n + 1])
    r, k, v, g = mixed
    dh = _tshift_rows(h, line) - h
    xw = h + dh * mux_ref[0:1]
    xa = h + dh * mux_ref[1:2]
    tw = jnp.tanh(_dot_bf16(xw, l1_ref[:, 0:128]))
    ta = _dot_bf16(xa, l1_ref[:, 128:256])
    zw = _dot_bf16(tw, l2_ref[0:128, :]) + l0_ref[0:1]
    za = _dot_bf16(ta, l2_ref[128:256, :]) + l0_ref[1:2]

    ones = ones_ref[...]
    kk = k * kkp_ref[...]
    kk = kk * lax.rsqrt(_group_sum(kk * kk, ones) + 1e-12)
    r_ref[...] = r
    v_ref[...] = v
    kk_ref[...] = kk
    kd_sum = None
    for d in range(2):
        w_raw = -_softplus(-zw[:, d * w:(d + 1) * w]) - 0.5
        w_out_ref[d] = jnp.exp(-jnp.exp(w_raw))
        a = _sigmoid(za[:, d * w:(d + 1) * w])
        kd = k * (1.0 + (a - 1.0) * ka_ref[...])
        kd_ref[d] = kd
        b_ref[d] = kk * a
        kd_sum = kd if kd_sum is None else kd_sum + kd
    bonus_ref[...] = _group_sum(r * kd_sum * rk_ref[...], ones) * v
    sg_ref[...] = _sigmoid(g)


def _cd_in(x, mods, w_in, hyc, mu, mux, l1, l2, l0, kkp, ka, rk, ones, line):
    n = x.shape[0]
    tpb = n // mods.shape[0] // TM
    tok = lambda i: (i, 0)
    const2 = lambda i: (0, 0)
    half = jax.ShapeDtypeStruct((n, W_HALF), F32)
    both = jax.ShapeDtypeStruct((2, n, W_HALF), F32)
    half_spec = pl.BlockSpec((TM, W_HALF), tok)
    both_spec = pl.BlockSpec((2, TM, W_HALF), lambda i: (0, i, 0))
    consts = [w_in, hyc, mu, mux, l1, l2, l0, kkp, ka, rk, ones]
    return pl.pallas_call(
        functools.partial(_cd_in_kernel, line=line),
        grid=(n // TM,),
        in_specs=[pl.BlockSpec((TM, D_MODEL), tok),
                  pl.BlockSpec((1, 6, D_MODEL), lambda i: (i // tpb, 0, 0))]
                 + [pl.BlockSpec(c.shape, const2) for c in consts],
        out_specs=[half_spec] * 6 + [both_spec] * 3 + [half_spec] * 2,
        out_shape=[half] * 6 + [both] * 3 + [half] * 2,
        compiler_params=_cparams(("parallel",)),
        name="cd_in",
    )(x, mods, *consts)


def _rwkv_scan_kernel(rf_ref, vf_ref, kkf_ref, wf_ref, kdf_ref, bf_ref,
                      rb_ref, vb_ref, kkb_ref, wb_ref, kdb_ref, bb_ref,
                      s0_ref, ones_ref, eye_ref,
                      of_ref, ob_ref, fin_ref, s_scr, sb_scr, o_scr, *, tc, nc, gb):
    c = pl.program_id(1)
    nsub = tc // RW_HEAD
    grp = 8
    per_sub = RW_HEAD // grp

    @pl.when(c == 0)
    def _():
        s_scr[...] = s0_ref[...]
        sb_scr[...] = s0_ref[...].astype(BF16)
        o_scr[...] = jnp.zeros_like(o_scr)

    ones = ones_ref[...]
    lane_pos = lax.broadcasted_iota(jnp.int32, (RW_HEAD, SCAN_LANES), 1) & (RW_HEAD - 1)
    chains = []
    for g in range(gb):
        chains.append((g, 0, rf_ref, vf_ref, kkf_ref, wf_ref, kdf_ref, bf_ref))
        chains.append((g, 1, rb_ref, vb_ref, kkb_ref, wb_ref, kdb_ref, bb_ref))

    hd = RW_HEAD

    tiles = [[(ch, pl.ds(j * SCAN_LANES, SCAN_LANES)) for ch in chains[c0:c0 + SCAN_CHAINS]]
             for c0 in range(0, len(chains), SCAN_CHAINS) for j in range(W_HALF // SCAN_LANES)]

    def put_outputs(streams, res_rows, step, base):
        for si, ((g, d, *_), ls) in enumerate(streams):
            pos = base + step if d == 0 else RW_HEAD - 1 - (base + step)
            o_scr[g, d, :, ls] = jnp.where(lane_pos == pos, res_rows[si * hd:(si + 1) * hd], o_scr[g, d, :, ls])

    def group(i, carry):
        sub = i // per_sub
        base = (i % per_sub) * grp
        row_f = pl.ds(pl.multiple_of(i * grp, grp), grp)
        row_b = pl.ds(pl.multiple_of(tc - grp - i * grp, grp), grp)

        def row(ref, g, d, ls, s, per_dir=False):
            q = s if d == 0 else grp - 1 - s
            tile = ref[0, g, row_b if d else row_f, ls] if per_dir else ref[g, row_b if d else row_f, ls]
            return tile[q:q + 1]

        def output_rows(streams, s):
            return [sb_scr[g, d, :, ls] * row(r_ref, g, d, ls, s).astype(BF16)
                    for ((g, d, r_ref, *_), ls) in streams]

        for s in range(grp):
            for streams in tiles:
                nst = len(streams)
                lhs = []
                for ((g, d, r_ref, v_ref, kk_ref, w_ref, kd_ref, b_ref), ls) in streams:
                    lhs.append(sb_scr[g, d, :, ls] * row(kk_ref, g, d, ls, s).astype(BF16))
                    lhs.append(eye_ref[:, ls] * row(v_ref, g, d, ls, s).astype(BF16))
                if s > 0:
                    lhs += output_rows(streams, s - 1)
                res = jnp.dot(jnp.concatenate(lhs, 0), ones, preferred_element_type=F32)
                for si, ((g, d, r_ref, v_ref, kk_ref, w_ref, kd_ref, b_ref), ls) in enumerate(streams):
                    sa = res[2 * si * hd:(2 * si + 1) * hd]
                    vcol = res[(2 * si + 1) * hd:(2 * si + 2) * hd]
                    st = (s_scr[g, d, :, ls] * row(w_ref, g, d, ls, s, True)
                          - sa * row(b_ref, g, d, ls, s, True) + vcol * row(kd_ref, g, d, ls, s, True))
                    s_scr[g, d, :, ls] = st
                    sb_scr[g, d, :, ls] = st.astype(BF16)
                if s > 0:
                    put_outputs(streams, res[2 * nst * hd:], s - 1, base)
        for streams in tiles:
            res = jnp.dot(jnp.concatenate(output_rows(streams, grp - 1), 0), ones, preferred_element_type=F32)
            put_outputs(streams, res, grp - 1, base)

        @pl.when(i % per_sub == per_sub - 1)
        def _():
            for g in range(gb):
                for d, (o_ref, blk) in enumerate(((of_ref, sub), (ob_ref, nsub - 1 - sub))):
                    ot = o_scr[g, d].T
                    for h in range(RW_HEADS):
                        o_ref[g, blk, :, h * hd:(h + 1) * hd] = ot[h * hd:(h + 1) * hd, :]

        return carry

    lax.fori_loop(0, tc // grp, group, 0)

    @pl.when(c == nc - 1)
    def _():
        fin_ref[...] = s_scr[...]


def _rwkv_scan(r, v, kk, w, kd, b, s0, ones2, eye, bsz, seq):
    tc = 64
    nc = seq // tc
    gb = min(bsz, SCAN_BATCH)
    r, v, kk = (t.reshape(bsz, seq, W_HALF) for t in (r, v, kk))
    w, kd, b = (t.reshape(2, bsz, seq, W_HALF) for t in (w, kd, b))
    shared = (gb, tc, W_HALF)
    perdir = (1, gb, tc, W_HALF)
    sf = lambda i, c: (i, c, 0)
    sb = lambda i, c: (i, nc - 1 - c, 0)
    pf = lambda i, c: (0, i, c, 0)
    pb = lambda i, c: (1, i, nc - 1 - c, 0)
    nsub = tc // RW_HEAD
    o_shape = jax.ShapeDtypeStruct((bsz, seq // RW_HEAD, RW_HEAD, W_HALF), F32)
    o_blk = (gb, nsub, RW_HEAD, W_HALF)
    st_blk = (gb, 2, RW_HEAD, W_HALF)
    return pl.pallas_call(
        functools.partial(_rwkv_scan_kernel, tc=tc, nc=nc, gb=gb),
        grid=(bsz // gb, nc),
        in_specs=[pl.BlockSpec(shared, sf)] * 3 + [pl.BlockSpec(perdir, pf)] * 3
                 + [pl.BlockSpec(shared, sb)] * 3 + [pl.BlockSpec(perdir, pb)] * 3
                 + [pl.BlockSpec(st_blk, lambda i, c: (i, 0, 0, 0)),
                    pl.BlockSpec(ones2.shape, lambda i, c: (0, 0)),
                    pl.BlockSpec(eye.shape, lambda i, c: (0, 0))],
        out_specs=[pl.BlockSpec(o_blk, lambda i, c: (i, c, 0, 0)),
                   pl.BlockSpec(o_blk, lambda i, c: (i, nc - 1 - c, 0, 0)),
                   pl.BlockSpec(st_blk, lambda i, c: (i, 0, 0, 0))],
        out_shape=[o_shape, o_shape, jax.ShapeDtypeStruct((bsz, 2, RW_HEAD, W_HALF), F32)],
        scratch_shapes=[pltpu.VMEM(st_blk, F32), pltpu.VMEM(st_blk, BF16), pltpu.VMEM(st_blk, F32)],
        compiler_params=_cparams(("parallel", "arbitrary")),
        name="rwkv_scan",
    )(r, v, kk, w, kd, b, r, v, kk, w, kd, b, s0, ones2, eye)


def _cd_out_kernel(x_ref, mod_ref, yc_ref, of_ref, ob_ref, bonus_ref, sg_ref, gng_ref, gnb_ref,
                   ones_ref, w_ref, *rest):
    *tail_refs, o_ref = rest
    ones = ones_ref[...]
    o = of_ref[...] + ob_ref[...]
    oc = o - _group_sum(o, ones) * (1.0 / RW_HEAD)
    on = oc * lax.rsqrt(_group_sum(oc * oc, ones) * (1.0 / RW_HEAD) + GN_EPS)
    yd = (on * gng_ref[...] + gnb_ref[...] + bonus_ref[...]) * sg_ref[...]
    y = _dot_bf16(yc_ref[...], w_ref[0:W_HALF, :]) + _dot_bf16(yd, w_ref[W_HALF:, :])
    o_ref[...] = _sublayers_tail(x_ref[...], mod_ref[0], y, tail_refs)


def _cd_out(x, mods, yc, of, ob, bonus, sg, ones, pw):
    n = x.shape[0]
    tpb = n // mods.shape[0] // TM_TAIL
    tok = lambda i: (i, 0)
    const2 = lambda i: (0, 0)
    half = pl.BlockSpec((TM_TAIL, W_HALF), tok)
    vec = pl.BlockSpec((1, W_HALF), const2)
    tail_args, tail_specs = _tail_specs(pw, n, mods)
    return pl.pallas_call(
        _cd_out_kernel,
        grid=(n // TM_TAIL,),
        in_specs=[pl.BlockSpec((TM_TAIL, D_MODEL), tok),
                  pl.BlockSpec((1, 6, D_MODEL), lambda i: (i // tpb, 0, 0)),
                  half, half, half, half, half, vec, vec,
                  pl.BlockSpec(ones.shape, const2),
                  pl.BlockSpec(pw['w_out'].shape, const2)] + tail_specs,
        out_specs=pl.BlockSpec((TM_TAIL, D_MODEL), tok),
        out_shape=jax.ShapeDtypeStruct((n, D_MODEL), F32),
        compiler_params=_cparams(("parallel",)),
        name="cd_out_mlp",
    )(x, mods, yc, of, ob, bonus, sg, pw['rw_gn_g'], pw['rw_gn_b'], ones, pw['w_out'], *tail_args)


def _hyena_consts(seq):
    tn = np.linspace(0.0, 1.0, seq, dtype=np.float32)
    tr = np.arange(seq, dtype=np.float32)
    bands = np.linspace(1e-4, HY_BANDS - 1, HY_BANDS, dtype=np.float32)
    ang = np.float32(2.0 * math.pi / seq) * tr[:, None] * bands[None, :]
    feats = np.concatenate([tn[:, None], np.cos(ang), -np.sin(ang)], -1).astype(np.float32)
    feats = np.pad(feats, ((0, 0), (0, 40 - feats.shape[1])))
    deltas = np.abs(np.linspace(math.log(HY_TARGET) / HY_LONG_PCT, math.log(HY_TARGET) / HY_SHORT_PCT,
                                W_HALF, dtype=np.float32))
    decay = np.exp(-tn[:, None] * deltas[None, :]).astype(np.float32)
    lag = np.concatenate([np.arange(seq), [0], np.arange(seq - 1, 0, -1)])
    return feats[lag], decay[lag]


def _filt_kernel(f_ref, dec_ref, w1_ref, b1_ref, w2_ref, b2_ref, w3_ref, fr_ref, o_ref, ssq_ref, *, half):
    i = pl.program_id(0)
    fr = fr_ref[...]
    hid = jnp.sin(fr * (jnp.dot(f_ref[...], w1_ref[...], precision=HIGHEST,
                                preferred_element_type=F32) + b1_ref[...]))
    hid = jnp.sin(fr * (jnp.dot(hid, w2_ref[...], precision=HIGHEST,
                                preferred_element_type=F32) + b2_ref[...]))
    raw = jnp.dot(hid, w3_ref[...], precision=HIGHEST, preferred_element_type=F32)
    dec = dec_ref[...]
    rows = raw.shape[0]
    backward = i >= half
    middle = (lax.broadcasted_iota(jnp.int32, (rows, W_HALF), 0) == 0) & (i == half)
    parts = []
    for order in range(2):
        fwd = raw[:, (2 * order) * W_HALF:(2 * order + 1) * W_HALF]
        bwd = raw[:, (2 * order + 1) * W_HALF:(2 * order + 2) * W_HALF]
        f = jnp.where(middle, 0.0, jnp.where(backward, bwd, fwd) * dec)
        o_ref[order] = f
        parts.append(jnp.sum(f * f, 0, keepdims=True))
    ssq = jnp.concatenate(parts, 0)

    @pl.when(i == 0)
    def _():
        ssq_ref[...] = ssq

    @pl.when(i > 0)
    def _():
        ssq_ref[...] += ssq


def _hyena_filters(seq, w1, b1, w2, b2, w3, freq):
    feats, decay = _hyena_consts(seq)
    w1 = jnp.pad(w1, ((0, 40 - w1.shape[0]), (0, 0)))
    rows = 256
    const2 = lambda i: (0, 0)
    tok = lambda i: (i, 0)
    vec = lambda a: a.reshape(1, -1)
    args = [jnp.asarray(feats), jnp.asarray(decay), w1, vec(b1), w2, vec(b2), w3, vec(freq)]
    return pl.pallas_call(
        functools.partial(_filt_kernel, half=seq // rows),
        grid=(2 * seq // rows,),
        in_specs=[pl.BlockSpec((rows, 40), tok), pl.BlockSpec((rows, W_HALF), tok)]
                 + [pl.BlockSpec(a.shape, const2) for a in args[2:]],
        out_specs=[pl.BlockSpec((2, rows, W_HALF), lambda i: (0, i, 0)), pl.BlockSpec((2, W_HALF), const2)],
        out_shape=[jax.ShapeDtypeStruct((2, 2 * seq, W_HALF), F32),
                   jax.ShapeDtypeStruct((2, W_HALF), F32)],
        compiler_params=_cparams(("arbitrary",)),
        name="hyena_filters",
    )(*args)


def _dft_consts_two_stage(seq):
    n = 2 * seq
    p = DFT_P
    q = n // p
    a = np.arange(q)[None, :]
    k1 = np.arange(q)[:, None]
    ang1 = 2.0 * np.pi * a * k1 / q
    f1 = np.empty((2 * q, q))
    f1[0::2] = np.cos(ang1)
    f1[1::2] = -np.sin(ang1)
    f3 = np.empty((q, 2 * q))
    f3[:, 0::2] = np.cos(ang1).T / n
    f3[:, 1::2] = -np.sin(ang1).T / n
    qq = np.arange(p)[None, None, :]
    k2 = np.arange(p)[None, :, None]
    kk1 = np.arange(q)[:, None, None]
    ang = 2.0 * np.pi * (qq * k2 / p + qq * kk1 / n)
    gr, gi = np.cos(ang), -np.sin(ang)
    g = np.concatenate([np.concatenate([gr, -gi], 2), np.concatenate([gi, gr], 2)], 1)
    grt, git = np.swapaxes(gr, 1, 2), np.swapaxes(gi, 1, 2)
    ginv = np.concatenate([np.concatenate([grt, git], 2), np.concatenate([-git, grt], 2)], 1)
    half = q // 2
    return (f1.astype(np.float32), f1[:, :half].astype(np.float32), f3[:half].astype(np.float32),
            g.astype(np.float32), ginv.astype(np.float32))


def _dft_consts_one_stage(seq):
    n = 2 * seq
    t = np.arange(n)[None, :]
    f = np.arange(n)[:, None]
    ang = 2.0 * np.pi * t * f / n
    fwd = np.concatenate([np.cos(ang), -np.sin(ang)], 0)
    inv = np.concatenate([np.cos(ang), -np.sin(ang)], 1)[:seq] / n
    return fwd.astype(np.float32), fwd[:, :seq].astype(np.float32), inv.astype(np.float32)


def _expand_rows(f, rows=DFT_ROWS):
    return np.kron(f, np.eye(rows, dtype=f.dtype))


def _stage_kernel(f_ref, x_ref, o_ref, *, exact):
    k, rows, ch = x_ref.shape[1:]
    x = x_ref[0].reshape(k * rows, ch)
    if exact:
        res = jnp.dot(f_ref[...], x, precision=HIGHEST, preferred_element_type=F32)
    else:
        res = _dot_bf16(f_ref[...], x.astype(BF16))
    o_ref[0] = res.reshape(o_ref.shape[1:]).astype(o_ref.dtype)


def _stage(f, x, exact=False):
    bsz, k, p, ch = x.shape
    rr = f.shape[0]
    rows = DFT_ROWS // 2 if exact else DFT_ROWS
    fx = jnp.asarray(_expand_rows(f, rows), dtype=F32 if exact else BF16)
    return pl.pallas_call(
        functools.partial(_stage_kernel, exact=exact),
        grid=(bsz, p // rows),
        in_specs=[pl.BlockSpec(fx.shape, lambda b, j: (0, 0)),
                  pl.BlockSpec((1, k, rows, ch), lambda b, j: (b, 0, j, 0))],
        out_specs=pl.BlockSpec((1, rr, rows, ch), lambda b, j: (b, 0, j, 0)),
        out_shape=jax.ShapeDtypeStruct((bsz, rr, p, ch), F32 if exact else BF16),
        compiler_params=_cparams(("parallel", "parallel")),
        name="dft_stage",
    )(fx, x)


def _spectrum_kernel(a_ref, g_ref, ssq_ref, o_ref):
    x = jnp.dot(g_ref[0], a_ref[0, 0], precision=HIGHEST, preferred_element_type=F32)
    o_ref[0, 0] = x * lax.rsqrt(ssq_ref[0] + 1e-6)


def _spectrum(a, g, ssq):
    nb, q, k, ch = a.shape
    rr = g.shape[1]
    return pl.pallas_call(
        _spectrum_kernel,
        grid=(nb, q),
        in_specs=[pl.BlockSpec((1, 1, k, ch), lambda b, j: (b, j, 0, 0)),
                  pl.BlockSpec((1, rr, k), lambda b, j: (j, 0, 0)),
                  pl.BlockSpec((1, 1, ch), lambda b, j: (b, 0, 0))],
        out_specs=pl.BlockSpec((1, 1, rr, ch), lambda b, j: (b, j, 0, 0)),
        out_shape=jax.ShapeDtypeStruct((nb, q, rr, ch), F32),
        compiler_params=_cparams(("parallel", "parallel")),
        name="hyena_spectrum",
    )(a, g, ssq)


def _spectral_conv(a, g_ref, h_ref, gi_ref):
    x = _dot_bf16(g_ref[0], a.astype(BF16))
    half = x.shape[0] // 2
    xr, xi = x[:half], x[half:]
    hr, hi = h_ref[0, 0, :half], h_ref[0, 0, half:]
    y = jnp.concatenate([xr * hr - xi * hi, xr * hi + xi * hr], 0)
    return _dot_bf16(gi_ref[0], y.astype(BF16))


def _mid_kernel(a_ref, g_ref, h_ref, gi_ref, o_ref):
    for b in range(a_ref.shape[0]):
        o_ref[b, 0] = _spectral_conv(a_ref[b, 0], g_ref, h_ref, gi_ref).astype(o_ref.dtype)


def _mid_gate_kernel(a_ref, g_ref, h_ref, gi_ref, hx_ref, bias_ref, o_ref):
    for b in range(a_ref.shape[0]):
        a = a_ref[b, 0]
        o_ref[b, 0] = hx_ref[b, 0] * (_spectral_conv(a, g_ref, h_ref, gi_ref) + bias_ref[...] * a)


def _mid(a, g, h, gi, order, gate=None, out_dtype=F32):
    bsz, q, k, ch = a.shape
    rr = g.shape[1]
    ro = gi.shape[1]
    nb = min(bsz, MID_BATCH)
    act = lambda j, b: (b, j, 0, 0)
    in_specs = [pl.BlockSpec((nb, 1, k, ch), act),
                pl.BlockSpec((1, rr, k), lambda j, b: (j, 0, 0)),
                pl.BlockSpec((1, 1, rr, ch), lambda j, b: (order, j, 0, 0)),
                pl.BlockSpec((1, ro, rr), lambda j, b: (j, 0, 0))]
    args = [a, g, h, gi]
    kern = _mid_kernel
    if gate is not None:
        in_specs += [pl.BlockSpec((nb, 1, ro, ch), act), pl.BlockSpec((1, ch), lambda j, b: (0, 0))]
        args += list(gate)
        kern = _mid_gate_kernel
    return pl.pallas_call(
        kern,
        grid=(q, bsz // nb),
        in_specs=in_specs,
        out_specs=pl.BlockSpec((nb, 1, ro, ch), act),
        out_shape=jax.ShapeDtypeStruct((bsz, q, ro, ch), out_dtype),
        compiler_params=_cparams(("parallel", "parallel")),
        name="hyena_mid",
    )(*args)


def _last_stage_kernel(f3_ref, bt_ref, hx_ref, z_ref, bias_ref, o_ref):
    k, rows, ch = bt_ref.shape[1:]
    conv = _dot_bf16(f3_ref[...], bt_ref[0].reshape(k * rows, ch))
    flat = conv.shape
    out = hx_ref[0].reshape(flat) * (conv + bias_ref[...] * z_ref[0].reshape(flat))
    o_ref[0] = out.reshape(o_ref.shape[1:])


def _last_stage(f3, bt, hx, z, bias):
    bsz, k, p, ch = bt.shape
    rr = f3.shape[0]
    fx = jnp.asarray(_expand_rows(f3), dtype=BF16)
    act = pl.BlockSpec((1, rr, DFT_ROWS, ch), lambda b, j: (b, 0, j, 0))
    return pl.pallas_call(
        _last_stage_kernel,
        grid=(bsz, p // DFT_ROWS),
        in_specs=[pl.BlockSpec(fx.shape, lambda b, j: (0, 0)),
                  pl.BlockSpec((1, k, DFT_ROWS, ch), lambda b, j: (b, 0, j, 0)),
                  act, act,
                  pl.BlockSpec((1, ch), lambda b, j: (0, 0))],
        out_specs=act,
        out_shape=jax.ShapeDtypeStruct((bsz, rr, p, ch), F32),
        compiler_params=_cparams(("parallel", "parallel")),
        name="dft_last_stage",
    )(fx, bt, hx, z, bias)


def _hyena(hv, hx1, hx2, circ, ssq, bias, bsz, seq):
    ch = W_HALF
    ssq = ssq.reshape(2, 1, ch)
    if seq <= 512:
        fwd_full, fwd_half, inv = _dft_consts_one_stage(seq)
        spec = _spectrum(circ[:, None], jnp.asarray(fwd_full)[None], ssq)
        fwd_half = jnp.asarray(fwd_half, dtype=BF16)[None]
        inv = jnp.asarray(inv, dtype=BF16)[None]
        shape4 = (bsz, 1, seq, ch)
        z = _mid(hv.reshape(shape4), fwd_half, spec, inv, 0, gate=(hx1.reshape(shape4), bias[0:1]))
        y = _mid(z, fwd_half, spec, inv, 1, gate=(hx2.reshape(shape4), bias[1:2]))
        return y.reshape(bsz * seq, ch)
    f1_full, f1_half, f3, g, ginv = _dft_consts_two_stage(seq)
    p = DFT_P
    q = 2 * seq // p
    spec = _spectrum(_stage(f1_full, circ.reshape(2, q, p, ch), exact=True).reshape(2, q, 2 * p, ch),
                     jnp.asarray(g), ssq)
    g = jnp.asarray(g, dtype=BF16)
    ginv = jnp.asarray(ginv, dtype=BF16)
    nat = (bsz, q // 2, p, ch)
    out = hv.reshape(nat)
    for order, hx in enumerate((hx1, hx2)):
        a = _stage(f1_half, out).reshape(bsz, q, 2 * p, ch)
        bt = _mid(a, g, spec, ginv, order, out_dtype=BF16).reshape(bsz, 2 * q, p, ch)
        out = _last_stage(f3, bt, hx.reshape(nat), out, bias[order:order + 1])
    return out.reshape(bsz * seq, ch)


def _block_diag(w):
    hh, blk, _ = w.shape
    eye = jnp.eye(hh, dtype=w.dtype)
    return jnp.einsum('hij,hg->higj', w, eye).reshape(hh * blk, hh * blk)


def _group_ones(width):
    idx = np.arange(width) // RW_HEAD
    return jnp.asarray((idx[:, None] == idx[None, :]).astype(np.float32), dtype=BF16)


def _even_layer(x, mods, h0, bsz, seq, line, pw):
    ya, gg, a, u = _ab_in(x, mods, pw['w_in'], pw['sc_conv'], pw['lru_conv'], pw['lru_conv_b'],
                          pw['wg'], pw['bg'], pw['nsp'], line)
    hf, hb, fin = _lru_scan(a, u, h0, bsz, seq)
    x = _ab_out(x, mods, ya, gg, hf.reshape(-1, W_HALF), hb.reshape(-1, W_HALF), pw)
    return x, fin


def _odd_layer(x, mods, s0, bsz, seq, line, pw):
    ones = _group_ones(MXU_TILE)
    (hv, hx1, hx2, r, v, kk, w, kd, b, bonus, sg) = _cd_in(
        x, mods, pw['w_in'], pw['hy_conv'], pw['rw_mu'], pw['rw_mu_x'], pw['l1'], pw['l2'], pw['l0'],
        pw['rw_kk'], pw['rw_ka'], pw['rw_rk'], ones, line)
    filt, ssq = _hyena_filters(seq, pw['hy_w1'], pw['hy_b1'], pw['hy_w2'], pw['hy_b2'], pw['hy_w3'],
                               pw['hy_freq'])
    yc = _hyena(hv, hx1, hx2, filt, ssq, pw['hy_bias'], bsz, seq)

    s0 = jnp.transpose(s0, (0, 1, 3, 2, 4)).reshape(bsz, 2, RW_HEAD, W_HALF)
    lane = np.arange(W_HALF)
    eye = jnp.asarray(((lane[None, :] % RW_HEAD) == np.arange(RW_HEAD)[:, None]).astype(np.float32),
                      dtype=BF16)
    of, ob, fin = _rwkv_scan(r, v, kk, w, kd, b, s0, _group_ones(SCAN_LANES), eye, bsz, seq)
    x = _cd_out(x, mods, yc, of.reshape(-1, W_HALF), ob.reshape(-1, W_HALF), bonus, sg, ones, pw)
    fin = jnp.transpose(fin.reshape(bsz, 2, RW_HEAD, RW_HEADS, RW_HEAD), (0, 1, 3, 2, 4))
    return x, fin


def _layer_weights(p, l):
    j = l // 2
    row = lambda a: a.reshape(1, -1)
    pw = {'w_out': p['w_out'][l].astype(BF16), 'ln1_g': row(p['ln1_g'][l]), 'ln1_b': row(p['ln1_b'][l]),
          'ln2_g': row(p['ln2_g'][l]), 'ln2_b': row(p['ln2_b'][l]),
          'mlp_w1': p['mlp_w1'][l].astype(BF16), 'mlp_w2': p['mlp_w2'][l].astype(BF16)}
    if l % 2 == 0:
        gates = [_block_diag(p[name][j, d]) for d in range(2) for name in ('lru_wa', 'lru_wi')]
        biases = [p[name][j, d] for d in range(2) for name in ('lru_ba', 'lru_bi')]
        pw.update({'w_in': p['ab_w_in'][j].astype(BF16), 'sc_conv': p['sc_conv'][j],
                   'lru_conv': p['lru_conv'][j], 'lru_conv_b': row(p['lru_conv_b'][j]),
                   'wg': jnp.concatenate(gates, 1).astype(BF16), 'bg': row(jnp.concatenate(biases)),
                   'nsp': jax.nn.softplus(-p['lru_lambda'][j])})
    else:
        zeros = jnp.zeros((64, W_HALF), F32)
        l2 = jnp.concatenate([
            jnp.concatenate([p['rw_w2'][j, 0], zeros], 1), jnp.concatenate([zeros, p['rw_w2'][j, 1]], 1),
            jnp.concatenate([p['rw_a2'][j, 0], zeros], 1), jnp.concatenate([zeros, p['rw_a2'][j, 1]], 1)], 0)
        pw.update({'w_in': p['cd_w_in'][j].astype(BF16), 'hy_conv': p['hy_conv'][j],
                   'rw_mu': p['rw_mu'][j], 'rw_mu_x': p['rw_mu_x'][j],
                   'l1': jnp.concatenate([p['rw_w1'][j, 0], p['rw_w1'][j, 1],
                                          p['rw_a1'][j, 0], p['rw_a1'][j, 1]], 1).astype(BF16),
                   'l2': l2.astype(BF16),
                   'l0': jnp.stack([p['rw_w0'][j].reshape(-1), p['rw_a0'][j].reshape(-1)]),
                   'rw_kk': row(p['rw_kk'][j]), 'rw_ka': row(p['rw_ka'][j]), 'rw_rk': row(p['rw_rk'][j]),
                   'rw_gn_g': row(p['rw_gn_g'][j]), 'rw_gn_b': row(p['rw_gn_b'][j]),
                   'hy_w1': p['hy_w1'][j], 'hy_b1': p['hy_b1'][j], 'hy_w2': p['hy_w2'][j],
                   'hy_b2': p['hy_b2'][j], 'hy_w3': p['hy_w3'][j], 'hy_freq': p['hy_freq'][j],
                   'hy_bias': p['hy_bias'][j]})
    return pw


def _to_colmajor(x, bsz, rows):
    return jnp.transpose(x.reshape(bsz, rows, GRID_W, D_MODEL), (0, 2, 1, 3)).reshape(-1, D_MODEL)


def _from_colmajor(x, bsz, rows):
    return jnp.transpose(x.reshape(bsz, GRID_W, rows, D_MODEL), (0, 2, 1, 3)).reshape(-1, D_MODEL)


def _trunk(x, mods, init_lru, init_rwkv, rows, weights):
    bsz, seq, _ = x.shape
    x = x.reshape(bsz * seq, D_MODEL)
    new_lru, new_rwkv = [], []
    for l in range(DEPTH):
        j = l // 2
        pw = weights[l]
        if l % 2 == 0:
            line = seq if rows is None else GRID_W
            x, st = _even_layer(x, mods[l], init_lru[:, j], bsz, seq, line, pw)
            new_lru.append(st)
        elif rows is None:
            x, st = _odd_layer(x, mods[l], init_rwkv[:, j], bsz, seq, seq, pw)
            new_rwkv.append(st)
        else:
            x = _to_colmajor(x, bsz, rows)
            x, st = _odd_layer(x, mods[l], init_rwkv[:, j], bsz, seq, rows, pw)
            x = _from_colmajor(x, bsz, rows)
            new_rwkv.append(st)
    return x.reshape(bsz, seq, D_MODEL), jnp.stack(new_lru, 1), jnp.stack(new_rwkv, 1)


def kernel(x_prompt, x_sample, state_lru, state_rwkv, c, c_ctx, w_mod, b_mod, ln1_g, ln1_b, ln2_g, ln2_b, mlp_w1, mlp_w2, w_out, ab_w_in, sc_conv, lru_conv, lru_conv_b, lru_wa, lru_ba, lru_wi, lru_bi, lru_lambda, cd_w_in, hy_conv, hy_w1, hy_b1, hy_w2, hy_b2, hy_w3, hy_freq, hy_bias, rw_mu, rw_mu_x, rw_w0, rw_w1, rw_w2, rw_a0, rw_a1, rw_a2, rw_kk, rw_ka, rw_rk, rw_gn_g, rw_gn_b):
    p = dict(ln1_g=ln1_g, ln1_b=ln1_b, ln2_g=ln2_g, ln2_b=ln2_b,
             mlp_w1=mlp_w1, mlp_w2=mlp_w2, w_out=w_out, ab_w_in=ab_w_in, sc_conv=sc_conv,
             lru_conv=lru_conv, lru_conv_b=lru_conv_b, lru_wa=lru_wa, lru_ba=lru_ba, lru_wi=lru_wi,
             lru_bi=lru_bi, lru_lambda=lru_lambda, cd_w_in=cd_w_in, hy_conv=hy_conv, hy_w1=hy_w1,
             hy_b1=hy_b1, hy_w2=hy_w2, hy_b2=hy_b2, hy_w3=hy_w3, hy_freq=hy_freq, hy_bias=hy_bias,
             rw_mu=rw_mu, rw_mu_x=rw_mu_x, rw_w0=rw_w0, rw_w1=rw_w1, rw_w2=rw_w2, rw_a0=rw_a0,
             rw_a1=rw_a1, rw_a2=rw_a2, rw_kk=rw_kk, rw_ka=rw_ka, rw_rk=rw_rk, rw_gn_g=rw_gn_g,
             rw_gn_b=rw_gn_b)
    weights = [_layer_weights(p, l) for l in range(DEPTH)]
    nb, dec = x_prompt.shape[0], x_sample.shape[0]
    rpad = -(1 + dec) % 8
    cvec = jnp.concatenate([c_ctx[None, :], c, jnp.zeros((rpad, D_MODEL), F32)], 0)
    mods = _mods(cvec, w_mod, b_mod)
    mods = jnp.transpose(mods, (0, 2, 1, 3))
    zero_lru = jnp.zeros((nb, (DEPTH + 1) // 2, 2, W_HALF), F32)
    zero_rwkv = jnp.zeros((nb, DEPTH // 2, 2, RW_HEADS, RW_HEAD, RW_HEAD), F32)
    y_prompt, new_lru, new_rwkv = _trunk(x_prompt, mods[:, 0:1], zero_lru, zero_rwkv, None, weights)
    rows = x_sample.shape[1] // GRID_W
    y_sample, _, _ = _trunk(x_sample, mods[:, 1:1 + dec], state_lru, state_rwkv, rows, weights)
    return (y_prompt, y_sample, new_lru, new_rwkv)
```

```python
import functools
import math

import jax
import jax.numpy as jnp
import numpy as np
from jax import lax
from jax.experimental import pallas as pl
from jax.experimental.pallas import tpu as pltpu

F32 = jnp.float32
BF16 = jnp.bfloat16
HIGHEST = lax.Precision.HIGHEST

D_MODEL = 1024
DEPTH = 4
GRID_W = 64
W_HALF = D_MODEL // 2
RG_C = 8.0
HY_BANDS = 16
HY_TARGET = 1e-2
HY_SHORT_PCT = 0.3
HY_LONG_PCT = 1.5
RW_HEAD = 64
RW_HEADS = W_HALF // RW_HEAD
D_FF = 4 * D_MODEL
DN_ALPHA = (2 * DEPTH) ** 0.25
LN_EPS = 1e-5
GN_EPS = 64e-5

LANES = 128
TM = 256
TM_TAIL = 512
VMEM_LIMIT = 56 * 1024 * 1024
DFT_P = 128
DFT_ROWS = 16
MID_BATCH = 8
FF_CHUNK = 1024
LRU_BATCH = 4
MXU_TILE = 256
SCAN_LANES = MXU_TILE
SCAN_BATCH = 8
SCAN_CHAINS = 8


def _cparams(sem):
    return pltpu.CompilerParams(dimension_semantics=sem, vmem_limit_bytes=VMEM_LIMIT)


def _shift_rows(x, off, line):
    if off == 0:
        return x
    n, width = x.shape
    rolled = pltpu.roll(x, (-off) % n, 0)
    pos = lax.broadcasted_iota(jnp.int32, (n, LANES), 0) & (line - 1)
    valid = (pos + off >= 0) if off < 0 else (pos + off < line)
    return jnp.concatenate([jnp.where(valid, rolled[:, j:j + LANES], 0.0) for j in range(0, width, LANES)], 1)


def _dwconv_rows(x, w, pad_left, line):
    out = None
    for k in range(w.shape[0]):
        term = _shift_rows(x, k - pad_left, line) * w[k:k + 1]
        out = term if out is None else out + term
    return out


def _tshift_rows(x, line):
    return 0.5 * (_shift_rows(x, -1, line) + _shift_rows(x, 1, line))


def _layer_norm(v, g, b):
    mu = jnp.mean(v, -1, keepdims=True)
    vc = v - mu
    var = jnp.mean(vc * vc, -1, keepdims=True)
    return vc * lax.rsqrt(var + LN_EPS) * g + b


def _stream_spec(tile, n, grid_rows):
    if grid_rows is None:
        return pl.BlockSpec((tile, D_MODEL), lambda i: (i, 0)), (n, D_MODEL)
    cols = tile // grid_rows
    per_seq = GRID_W // cols
    spec = pl.BlockSpec((1, grid_rows, cols * D_MODEL), lambda i: (i // per_seq, 0, i % per_seq))
    return spec, (n // (grid_rows * GRID_W), grid_rows, GRID_W * D_MODEL)


def _read_stream(x_ref, grid_rows):
    if grid_rows is None:
        return x_ref[...]
    xb = x_ref[0]
    return jnp.concatenate([xb[:, k:k + D_MODEL] for k in range(0, xb.shape[1], D_MODEL)], 0)


def _write_stream(o_ref, y, grid_rows):
    if grid_rows is None:
        o_ref[...] = y
    else:
        for c in range(y.shape[0] // grid_rows):
            o_ref[0, :, c * D_MODEL:(c + 1) * D_MODEL] = y[c * grid_rows:(c + 1) * grid_rows]


def _dot_bf16(a, b):
    return jnp.dot(a.astype(BF16), b, preferred_element_type=F32)


def _group_sum(x, ones):
    hi = x.astype(BF16)
    lo = (x - hi.astype(F32)).astype(BF16)
    blk = ones.shape[0]
    parts = [jnp.dot(hi[:, j:j + blk], ones, preferred_element_type=F32)
             + jnp.dot(lo[:, j:j + blk], ones, preferred_element_type=F32) for j in range(0, x.shape[1], blk)]
    return jnp.concatenate(parts, 1)


def _softplus(x):
    return jnp.maximum(x, 0.0) + jnp.log(1.0 + jnp.exp(-jnp.abs(x)))


def _sigmoid(x):
    return 1.0 / (1.0 + jnp.exp(-x))


def _mods_kernel(c_ref, w_ref, b_ref, o_ref):
    c = c_ref[...]
    s = c * _sigmoid(c)
    o_ref[0, 0] = jnp.dot(s, w_ref[0], precision=HIGHEST, preferred_element_type=F32) + b_ref[0, 0]


def _mods(cvec, w_mod, b_mod):
    r = cvec.shape[0]
    return pl.pallas_call(
        _mods_kernel,
        grid=(DEPTH, 6),
        in_specs=[pl.BlockSpec((r, D_MODEL), lambda l, n: (0, 0)),
                  pl.BlockSpec((1, D_MODEL, D_MODEL), lambda l, n: (l, 0, n)),
                  pl.BlockSpec((1, 1, 1, D_MODEL), lambda l, n: (l, n, 0, 0))],
        out_specs=pl.BlockSpec((1, 1, r, D_MODEL), lambda l, n: (l, n, 0, 0)),
        out_shape=jax.ShapeDtypeStruct((DEPTH, 6, r, D_MODEL), F32),
        compiler_params=_cparams(("parallel", "parallel")),
        name="mods",
    )(cvec, w_mod, b_mod.reshape(DEPTH, 6, 1, D_MODEL))


def _ab_in_kernel(x_ref, mod_ref, w_ref, scw_ref, lcw_ref, lcb_ref, wg_ref, bg_ref, nsp_ref,
                  ya_ref, gg_ref, a_ref, u_ref, *, line):
    x = x_ref[...]
    m = mod_ref[0]
    h = x * (1.0 + m[1:2]) + m[0:1]
    proj = _dot_bf16(h, w_ref[...])
    w = W_HALF
    s_b, s_c, s_v = proj[:, 0:w], proj[:, w:2 * w], proj[:, 2 * w:3 * w]
    g_lru, x_lru = proj[:, 3 * w:4 * w], proj[:, 4 * w:5 * w]
    ya_ref[...] = s_b * _dwconv_rows(s_c * s_v, scw_ref[...], 1, line)
    gg_ref[...] = 0.5 * g_lru * (1.0 + jnp.tanh(
        math.sqrt(2.0 / math.pi) * (g_lru + 0.044715 * (g_lru * g_lru * g_lru))))
    xc = _dwconv_rows(x_lru, lcw_ref[...], 2, line) + lcb_ref[...]
    gates = _dot_bf16(xc, wg_ref[...]) + bg_ref[...]
    for d in range(2):
        r = _sigmoid(gates[:, (2 * d) * w:(2 * d + 1) * w])
        i = _sigmoid(gates[:, (2 * d + 1) * w:(2 * d + 2) * w])
        a = jnp.exp(-RG_C * r * nsp_ref[d:d + 1])
        a_ref[d] = a
        u_ref[d] = jnp.sqrt(1.0 - a * a) * (i * xc)


def _ab_in(x, mods, w_in, scw, lcw, lcb, wg, bg, nsp, line):
    n = x.shape[0]
    tpb = n // mods.shape[0] // TM
    tok = lambda i: (i, 0)
    const2 = lambda i: (0, 0)
    half = jax.ShapeDtypeStruct((n, W_HALF), F32)
    both = jax.ShapeDtypeStruct((2, n, W_HALF), F32)
    return pl.pallas_call(
        functools.partial(_ab_in_kernel, line=line),
        grid=(n // TM,),
        in_specs=[pl.BlockSpec((TM, D_MODEL), tok),
                  pl.BlockSpec((1, 6, D_MODEL), lambda i: (i // tpb, 0, 0)),
                  pl.BlockSpec(w_in.shape, const2),
                  pl.BlockSpec(scw.shape, const2),
                  pl.BlockSpec(lcw.shape, const2),
                  pl.BlockSpec(lcb.shape, const2),
                  pl.BlockSpec(wg.shape, const2),
                  pl.BlockSpec(bg.shape, const2),
                  pl.BlockSpec(nsp.shape, const2)],
        out_specs=[pl.BlockSpec((TM, W_HALF), tok), pl.BlockSpec((TM, W_HALF), tok),
                   pl.BlockSpec((2, TM, W_HALF), lambda i: (0, i, 0)),
                   pl.BlockSpec((2, TM, W_HALF), lambda i: (0, i, 0))],
        out_shape=[half, half, both, both],
        compiler_params=_cparams(("parallel",)),
        name="ab_in",
    )(x, mods, w_in, scw, lcw, lcb, wg, bg, nsp)


def _lru_scan_kernel(af_ref, uf_ref, ab_ref, ub_ref, h0_ref, hf_ref, hb_ref, fin_ref, carry, *, tc, nc):
    c = pl.program_id(1)

    nb = af_ref.shape[1]

    @pl.when(c == 0)
    def _():
        carry[...] = h0_ref[...]

    def step(s, hs):
        tb = tc - 1 - s
        out = []
        for g in range(nb):
            hf, hb = hs[2 * g], hs[2 * g + 1]
            hf = af_ref[0, g, pl.ds(s, 1), :] * hf + uf_ref[0, g, pl.ds(s, 1), :]
            hb = ab_ref[0, g, pl.ds(tb, 1), :] * hb + ub_ref[0, g, pl.ds(tb, 1), :]
            hf_ref[g, pl.ds(s, 1), :] = hf
            hb_ref[g, pl.ds(tb, 1), :] = hb
            out += [hf, hb]
        return tuple(out)

    init = tuple(carry[g, d:d + 1] for g in range(nb) for d in range(2))
    hs = lax.fori_loop(0, tc, step, init, unroll=8)
    for g in range(nb):
        for d in range(2):
            carry[g, d:d + 1] = hs[2 * g + d]

    @pl.when(c == nc - 1)
    def _():
        fin_ref[...] = carry[...]


def _lru_scan(a, u, h0, bsz, seq):
    tc = min(seq, 512)
    nc = seq // tc
    nb = min(bsz, LRU_BATCH)
    a = a.reshape(2, bsz, seq, W_HALF)
    u = u.reshape(2, bsz, seq, W_HALF)
    fwd = lambda b, c: (0, b, c, 0)
    bwd = lambda b, c: (1, b, nc - 1 - c, 0)
    blk = (1, nb, tc, W_HALF)
    seq_shape = jax.ShapeDtypeStruct((bsz, seq, W_HALF), F32)
    return pl.pallas_call(
        functools.partial(_lru_scan_kernel, tc=tc, nc=nc),
        grid=(bsz // nb, nc),
        in_specs=[pl.BlockSpec(blk, fwd), pl.BlockSpec(blk, fwd),
                  pl.BlockSpec(blk, bwd), pl.BlockSpec(blk, bwd),
                  pl.BlockSpec((nb, 2, W_HALF), lambda b, c: (b, 0, 0))],
        out_specs=[pl.BlockSpec((nb, tc, W_HALF), lambda b, c: (b, c, 0)),
                   pl.BlockSpec((nb, tc, W_HALF), lambda b, c: (b, nc - 1 - c, 0)),
                   pl.BlockSpec((nb, 2, W_HALF), lambda b, c: (b, 0, 0))],
        out_shape=[seq_shape, seq_shape, jax.ShapeDtypeStruct((bsz, 2, W_HALF), F32)],
        scratch_shapes=[pltpu.VMEM((nb, 2, W_HALF), F32)],
        compiler_params=_cparams(("parallel", "arbitrary")),
        name="lru_scan",
    )(a, u, a, u, h0)


def _sublayers_tail(x, m, y, tail_refs):
    g1_ref, b1_ref, w1_ref, w2_ref, g2_ref, b2_ref = tail_refs
    x = _layer_norm(DN_ALPHA * x + m[2:3] * y, g1_ref[...], b1_ref[...])
    h = (x * (1.0 + m[4:5]) + m[3:4]).astype(BF16)
    y = None
    for c in range(D_FF // FF_CHUNK):
        t = jnp.dot(h, w1_ref[:, c * FF_CHUNK:(c + 1) * FF_CHUNK], preferred_element_type=F32)
        t = jnp.maximum(t, 0.0)
        part = _dot_bf16(t * t, w2_ref[c * FF_CHUNK:(c + 1) * FF_CHUNK, :])
        y = part if y is None else y + part
    return _layer_norm(DN_ALPHA * x + m[5:6] * y, g2_ref[...], b2_ref[...])


def _tail_specs(pw, n, mods):
    assert n % TM_TAIL == 0 and (mods.shape[0] == 1 or (n // mods.shape[0]) % TM_TAIL == 0), (n, mods.shape)
    const2 = lambda i: (0, 0)
    vec = pl.BlockSpec((1, D_MODEL), const2)
    args = [pw['ln1_g'], pw['ln1_b'], pw['mlp_w1'], pw['mlp_w2'], pw['ln2_g'], pw['ln2_b']]
    specs = [vec, vec,
             pl.BlockSpec(pw['mlp_w1'].shape, const2, pipeline_mode=pl.Buffered(1)),
             pl.BlockSpec(pw['mlp_w2'].shape, const2, pipeline_mode=pl.Buffered(1)),
             vec, vec]
    return args, specs


def _ab_out_kernel(x_ref, mod_ref, ya_ref, gg_ref, hf_ref, hb_ref, w_ref, *rest):
    *tail_refs, o_ref = rest
    yb = gg_ref[...] * (hf_ref[...] + hb_ref[...])
    y = _dot_bf16(ya_ref[...], w_ref[0:W_HALF, :]) + _dot_bf16(yb, w_ref[W_HALF:, :])
    o_ref[...] = _sublayers_tail(x_ref[...], mod_ref[0], y, tail_refs)


def _ab_out(x, mods, ya, gg, hf, hb, pw):
    n = x.shape[0]
    tpb = n // mods.shape[0] // TM_TAIL
    tok = lambda i: (i, 0)
    half = pl.BlockSpec((TM_TAIL, W_HALF), tok)
    tail_args, tail_specs = _tail_specs(pw, n, mods)
    return pl.pallas_call(
        _ab_out_kernel,
        grid=(n // TM_TAIL,),
        in_specs=[pl.BlockSpec((TM_TAIL, D_MODEL), tok),
                  pl.BlockSpec((1, 6, D_MODEL), lambda i: (i // tpb, 0, 0)),
                  half, half, half, half,
                  pl.BlockSpec(pw['w_out'].shape, lambda i: (0, 0))] + tail_specs,
        out_specs=pl.BlockSpec((TM_TAIL, D_MODEL), tok),
        out_shape=jax.ShapeDtypeStruct((n, D_MODEL), F32),
        compiler_params=_cparams(("parallel",)),
        name="ab_out_mlp",
    )(x, mods, ya, gg, hf, hb, pw['w_out'], *tail_args)


def _cd_in_kernel(x_ref, mod_ref, w_ref, hyc_ref, mu_ref, mux_ref, l1_ref, l2_ref, l0_ref,
                  kkp_ref, ka_ref, rk_ref, ones_ref,
                  hv_ref, hx1_ref, hx2_ref, r_ref, v_ref, kk_ref, w_out_ref, kd_ref, b_ref,
                  bonus_ref, sg_ref, *, line, grid_rows):
    x = _read_stream(x_ref, grid_rows)
    m = mod_ref[0]
    h = x * (1.0 + m[1:2]) + m[0:1]
    proj = _dot_bf16(h, w_ref[...])
    w = W_HALF
    u = _dwconv_rows(proj[:, 0:3 * w], hyc_ref[...], 1, line)
    hv_ref[...] = u[:, 0:w]
    hx1_ref[...] = u[:, w:2 * w]
    hx2_ref[...] = u[:, 2 * w:3 * w]

    mixed = []
    for n in range(4):
        t = proj[:, (3 + n) * w:(4 + n) * w]
        mixed.append(t + (_tshift_rows(t, line) - t) * mu_ref[n:n + 1])
    r, k, v, g = mixed
    dh = _tshift_rows(h, line) - h
    xw = h + dh * mux_ref[0:1]
    xa = h + dh * mux_ref[1:2]
    tw = jnp.tanh(_dot_bf16(xw, l1_ref[:, 0:128]))
    ta = _dot_bf16(xa, l1_ref[:, 128:256])
    zw = _dot_bf16(tw, l2_ref[0:128, :]) + l0_ref[0:1]
    za = _dot_bf16(ta, l2_ref[128:256, :]) + l0_ref[1:2]

    ones = ones_ref[...]
    kk = k * kkp_ref[...]
    kk = kk * lax.rsqrt(_group_sum(kk * kk, ones) + 1e-12)
    r_ref[...] = r
    v_ref[...] = v
    kk_ref[...] = kk
    kd_sum = None
    for d in range(2):
        w_raw = -_softplus(-zw[:, d * w:(d + 1) * w]) - 0.5
        w_out_ref[d] = jnp.exp(-jnp.exp(w_raw))
        a = _sigmoid(za[:, d * w:(d + 1) * w])
        kd = k * (1.0 + (a - 1.0) * ka_ref[...])
        kd_ref[d] = kd
        b_ref[d] = kk * a
        kd_sum = kd if kd_sum is None else kd_sum + kd
    bonus_ref[...] = _group_sum(r * kd_sum * rk_ref[...], ones) * v
    sg_ref[...] = _sigmoid(g)


def _cd_in(x, mods, w_in, hyc, mu, mux, l1, l2, l0, kkp, ka, rk, ones, line, grid_rows=None):
    n = x.shape[0]
    x_spec, x_view = _stream_spec(TM, n, grid_rows)
    tpb = n // mods.shape[0] // TM
    tok = lambda i: (i, 0)
    const2 = lambda i: (0, 0)
    half = jax.ShapeDtypeStruct((n, W_HALF), F32)
    both = jax.ShapeDtypeStruct((2, n, W_HALF), F32)
    half_spec = pl.BlockSpec((TM, W_HALF), tok)
    both_spec = pl.BlockSpec((2, TM, W_HALF), lambda i: (0, i, 0))
    consts = [w_in, hyc, mu, mux, l1, l2, l0, kkp, ka, rk, ones]
    return pl.pallas_call(
        functools.partial(_cd_in_kernel, line=line, grid_rows=grid_rows),
        grid=(n // TM,),
        in_specs=[x_spec,
                  pl.BlockSpec((1, 6, D_MODEL), lambda i: (i // tpb, 0, 0))]
                 + [pl.BlockSpec(c.shape, const2) for c in consts],
        out_specs=[half_spec] * 6 + [both_spec] * 3 + [half_spec] * 2,
        out_shape=[half] * 6 + [both] * 3 + [half] * 2,
        compiler_params=_cparams(("parallel",)),
        name="cd_in",
    )(x.reshape(x_view), mods, *consts)


def _rwkv_scan_kernel(rf_ref, vf_ref, kkf_ref, wf_ref, kdf_ref, bf_ref,
                      rb_ref, vb_ref, kkb_ref, wb_ref, kdb_ref, bb_ref,
                      s0_ref, ones_ref, eye_ref,
                      of_ref, ob_ref, fin_ref, s_scr, sb_scr, o_scr, *, tc, nc, gb):
    c = pl.program_id(1)
    nsub = tc // RW_HEAD
    grp = 8
    per_sub = RW_HEAD // grp

    @pl.when(c == 0)
    def _():
        s_scr[...] = s0_ref[...]
        sb_scr[...] = s0_ref[...].astype(BF16)
        o_scr[...] = jnp.zeros_like(o_scr)

    ones = ones_ref[...]
    lane_pos = lax.broadcasted_iota(jnp.int32, (RW_HEAD, SCAN_LANES), 1) & (RW_HEAD - 1)
    chains = []
    for g in range(gb):
        chains.append((g, 0, rf_ref, vf_ref, kkf_ref, wf_ref, kdf_ref, bf_ref))
        chains.append((g, 1, rb_ref, vb_ref, kkb_ref, wb_ref, kdb_ref, bb_ref))

    hd = RW_HEAD

    tiles = [[(ch, pl.ds(j * SCAN_LANES, SCAN_LANES)) for ch in chains[c0:c0 + SCAN_CHAINS]]
             for c0 in range(0, len(chains), SCAN_CHAINS) for j in range(W_HALF // SCAN_LANES)]

    def put_outputs(streams, res_rows, step, base):
        for si, ((g, d, *_), ls) in enumerate(streams):
            pos = base + step if d == 0 else RW_HEAD - 1 - (base + step)
            o_scr[g, d, :, ls] = jnp.where(lane_pos == pos, res_rows[si * hd:(si + 1) * hd], o_scr[g, d, :, ls])

    def group(i, carry):
        sub = i // per_sub
        base = (i % per_sub) * grp
        row_f = pl.ds(pl.multiple_of(i * grp, grp), grp)
        row_b = pl.ds(pl.multiple_of(tc - grp - i * grp, grp), grp)

        def row(ref, g, d, ls, s, per_dir=False):
            q = s if d == 0 else grp - 1 - s
            tile = ref[0, g, row_b if d else row_f, ls] if per_dir else ref[g, row_b if d else row_f, ls]
            return tile[q:q + 1]

        def output_rows(streams, s):
            return [sb_scr[g, d, :, ls] * row(r_ref, g, d, ls, s).astype(BF16)
                    for ((g, d, r_ref, *_), ls) in streams]

        for s in range(grp):
            for streams in tiles:
                nst = len(streams)
                lhs = []
                for ((g, d, r_ref, v_ref, kk_ref, w_ref, kd_ref, b_ref), ls) in streams:
                    lhs.append(sb_scr[g, d, :, ls] * row(kk_ref, g, d, ls, s).astype(BF16))
                    lhs.append(eye_ref[:, ls] * row(v_ref, g, d, ls, s).astype(BF16))
                if s > 0:
                    lhs += output_rows(streams, s - 1)
                res = jnp.dot(jnp.concatenate(lhs, 0), ones, preferred_element_type=F32)
                for si, ((g, d, r_ref, v_ref, kk_ref, w_ref, kd_ref, b_ref), ls) in enumerate(streams):
                    sa = res[2 * si * hd:(2 * si + 1) * hd]
                    vcol = res[(2 * si + 1) * hd:(2 * si + 2) * hd]
                    st = (s_scr[g, d, :, ls] * row(w_ref, g, d, ls, s, True)
                          - sa * row(b_ref, g, d, ls, s, True) + vcol * row(kd_ref, g, d, ls, s, True))
                    s_scr[g, d, :, ls] = st
                    sb_scr[g, d, :, ls] = st.astype(BF16)
                if s > 0:
                    put_outputs(streams, res[2 * nst * hd:], s - 1, base)
        for streams in tiles:
            res = jnp.dot(jnp.concatenate(output_rows(streams, grp - 1), 0), ones, preferred_element_type=F32)
            put_outputs(streams, res, grp - 1, base)

        @pl.when(i % per_sub == per_sub - 1)
        def _():
            for g in range(gb):
                for d, (o_ref, blk) in enumerate(((of_ref, sub), (ob_ref, nsub - 1 - sub))):
                    ot = o_scr[g, d].T
                    for h in range(RW_HEADS):
                        o_ref[g, blk, :, h * hd:(h + 1) * hd] = ot[h * hd:(h + 1) * hd, :]

        return carry

    lax.fori_loop(0, tc // grp, group, 0)

    @pl.when(c == nc - 1)
    def _():
        fin_ref[...] = s_scr[...]


def _rwkv_scan(r, v, kk, w, kd, b, s0, ones2, eye, bsz, seq):
    tc = 64
    nc = seq // tc
    gb = min(bsz, SCAN_BATCH)
    r, v, kk = (t.reshape(bsz, seq, W_HALF) for t in (r, v, kk))
    w, kd, b = (t.reshape(2, bsz, seq, W_HALF) for t in (w, kd, b))
    shared = (gb, tc, W_HALF)
    perdir = (1, gb, tc, W_HALF)
    sf = lambda i, c: (i, c, 0)
    sb = lambda i, c: (i, nc - 1 - c, 0)
    pf = lambda i, c: (0, i, c, 0)
    pb = lambda i, c: (1, i, nc - 1 - c, 0)
    nsub = tc // RW_HEAD
    o_shape = jax.ShapeDtypeStruct((bsz, seq // RW_HEAD, RW_HEAD, W_HALF), F32)
    o_blk = (gb, nsub, RW_HEAD, W_HALF)
    st_blk = (gb, 2, RW_HEAD, W_HALF)
    return pl.pallas_call(
        functools.partial(_rwkv_scan_kernel, tc=tc, nc=nc, gb=gb),
        grid=(bsz // gb, nc),
        in_specs=[pl.BlockSpec(shared, sf)] * 3 + [pl.BlockSpec(perdir, pf)] * 3
                 + [pl.BlockSpec(shared, sb)] * 3 + [pl.BlockSpec(perdir, pb)] * 3
                 + [pl.BlockSpec(st_blk, lambda i, c: (i, 0, 0, 0)),
                    pl.BlockSpec(ones2.shape, lambda i, c: (0, 0)),
                    pl.BlockSpec(eye.shape, lambda i, c: (0, 0))],
        out_specs=[pl.BlockSpec(o_blk, lambda i, c: (i, c, 0, 0)),
                   pl.BlockSpec(o_blk, lambda i, c: (i, nc - 1 - c, 0, 0)),
                   pl.BlockSpec(st_blk, lambda i, c: (i, 0, 0, 0))],
        out_shape=[o_shape, o_shape, jax.ShapeDtypeStruct((bsz, 2, RW_HEAD, W_HALF), F32)],
        scratch_shapes=[pltpu.VMEM(st_blk, F32), pltpu.VMEM(st_blk, BF16), pltpu.VMEM(st_blk, F32)],
        compiler_params=_cparams(("parallel", "arbitrary")),
        name="rwkv_scan",
    )(r, v, kk, w, kd, b, r, v, kk, w, kd, b, s0, ones2, eye)


def _cd_out_kernel(x_ref, mod_ref, yc_ref, of_ref, ob_ref, bonus_ref, sg_ref, gng_ref, gnb_ref,
                   ones_ref, w_ref, *rest, grid_rows):
    *tail_refs, o_ref = rest
    ones = ones_ref[...]
    o = of_ref[...] + ob_ref[...]
    oc = o - _group_sum(o, ones) * (1.0 / RW_HEAD)
    on = oc * lax.rsqrt(_group_sum(oc * oc, ones) * (1.0 / RW_HEAD) + GN_EPS)
    yd = (on * gng_ref[...] + gnb_ref[...] + bonus_ref[...]) * sg_ref[...]
    y = _dot_bf16(yc_ref[...], w_ref[0:W_HALF, :]) + _dot_bf16(yd, w_ref[W_HALF:, :])
    _write_stream(o_ref, _sublayers_tail(_read_stream(x_ref, grid_rows), mod_ref[0], y, tail_refs), grid_rows)


def _cd_out(x, mods, yc, of, ob, bonus, sg, ones, pw, grid_rows=None):
    n = x.shape[0]
    tpb = n // mods.shape[0] // TM_TAIL
    tok = lambda i: (i, 0)
    const2 = lambda i: (0, 0)
    half = pl.BlockSpec((TM_TAIL, W_HALF), tok)
    vec = pl.BlockSpec((1, W_HALF), const2)
    tail_args, tail_specs = _tail_specs(pw, n, mods)
    x_spec, x_view = _stream_spec(TM_TAIL, n, grid_rows)
    return pl.pallas_call(
        functools.partial(_cd_out_kernel, grid_rows=grid_rows),
        grid=(n // TM_TAIL,),
        in_specs=[x_spec,
                  pl.BlockSpec((1, 6, D_MODEL), lambda i: (i // tpb, 0, 0)),
                  half, half, half, half, half, vec, vec,
                  pl.BlockSpec(ones.shape, const2),
                  pl.BlockSpec(pw['w_out'].shape, const2)] + tail_specs,
        out_specs=x_spec,
        out_shape=jax.ShapeDtypeStruct(x_view, F32),
        compiler_params=_cparams(("parallel",)),
        name="cd_out_mlp",
    )(x.reshape(x_view), mods, yc, of, ob, bonus, sg, pw['rw_gn_g'], pw['rw_gn_b'], ones, pw['w_out'],
      *tail_args).reshape(n, D_MODEL)


def _hyena_consts(seq):
    tn = np.linspace(0.0, 1.0, seq, dtype=np.float32)
    tr = np.arange(seq, dtype=np.float32)
    bands = np.linspace(1e-4, HY_BANDS - 1, HY_BANDS, dtype=np.float32)
    ang = np.float32(2.0 * math.pi / seq) * tr[:, None] * bands[None, :]
    feats = np.concatenate([tn[:, None], np.cos(ang), -np.sin(ang)], -1).astype(np.float32)
    feats = np.pad(feats, ((0, 0), (0, 40 - feats.shape[1])))
    deltas = np.abs(np.linspace(math.log(HY_TARGET) / HY_LONG_PCT, math.log(HY_TARGET) / HY_SHORT_PCT,
                                W_HALF, dtype=np.float32))
    decay = np.exp(-tn[:, None] * deltas[None, :]).astype(np.float32)
    lag = np.concatenate([np.arange(seq), [0], np.arange(seq - 1, 0, -1)])
    return feats[lag], decay[lag]


def _filt_kernel(f_ref, dec_ref, w1_ref, b1_ref, w2_ref, b2_ref, w3_ref, fr_ref, o_ref, ssq_ref, *, half):
    i = pl.program_id(0)
    fr = fr_ref[...]
    hid = jnp.sin(fr * (jnp.dot(f_ref[...], w1_ref[...], precision=HIGHEST,
                                preferred_element_type=F32) + b1_ref[...]))
    hid = jnp.sin(fr * (jnp.dot(hid, w2_ref[...], precision=HIGHEST,
                                preferred_element_type=F32) + b2_ref[...]))
    raw = jnp.dot(hid, w3_ref[...], precision=HIGHEST, preferred_element_type=F32)
    dec = dec_ref[...]
    rows = raw.shape[0]
    backward = i >= half
    middle = (lax.broadcasted_iota(jnp.int32, (rows, W_HALF), 0) == 0) & (i == half)
    parts = []
    for order in range(2):
        fwd = raw[:, (2 * order) * W_HALF:(2 * order + 1) * W_HALF]
        bwd = raw[:, (2 * order + 1) * W_HALF:(2 * order + 2) * W_HALF]
        f = jnp.where(middle, 0.0, jnp.where(backward, bwd, fwd) * dec)
        o_ref[order] = f
        parts.append(jnp.sum(f * f, 0, keepdims=True))
    ssq = jnp.concatenate(parts, 0)

    @pl.when(i == 0)
    def _():
        ssq_ref[...] = ssq

    @pl.when(i > 0)
    def _():
        ssq_ref[...] += ssq


def _hyena_filters(seq, w1, b1, w2, b2, w3, freq):
    feats, decay = _hyena_consts(seq)
    w1 = jnp.pad(w1, ((0, 40 - w1.shape[0]), (0, 0)))
    rows = 256
    const2 = lambda i: (0, 0)
    tok = lambda i: (i, 0)
    vec = lambda a: a.reshape(1, -1)
    args = [jnp.asarray(feats), jnp.asarray(decay), w1, vec(b1), w2, vec(b2), w3, vec(freq)]
    return pl.pallas_call(
        functools.partial(_filt_kernel, half=seq // rows),
        grid=(2 * seq // rows,),
        in_specs=[pl.BlockSpec((rows, 40), tok), pl.BlockSpec((rows, W_HALF), tok)]
                 + [pl.BlockSpec(a.shape, const2) for a in args[2:]],
        out_specs=[pl.BlockSpec((2, rows, W_HALF), lambda i: (0, i, 0)), pl.BlockSpec((2, W_HALF), const2)],
        out_shape=[jax.ShapeDtypeStruct((2, 2 * seq, W_HALF), F32),
                   jax.ShapeDtypeStruct((2, W_HALF), F32)],
        compiler_params=_cparams(("arbitrary",)),
        name="hyena_filters",
    )(*args)


def _dft_consts_two_stage(seq):
    n = 2 * seq
    p = DFT_P
    q = n // p
    a = np.arange(q)[None, :]
    k1 = np.arange(q)[:, None]
    ang1 = 2.0 * np.pi * a * k1 / q
    f1 = np.empty((2 * q, q))
    f1[0::2] = np.cos(ang1)
    f1[1::2] = -np.sin(ang1)
    f3 = np.empty((q, 2 * q))
    f3[:, 0::2] = np.cos(ang1).T / n
    f3[:, 1::2] = -np.sin(ang1).T / n
    qq = np.arange(p)[None, None, :]
    k2 = np.arange(p)[None, :, None]
    kk1 = np.arange(q)[:, None, None]
    ang = 2.0 * np.pi * (qq * k2 / p + qq * kk1 / n)
    gr, gi = np.cos(ang), -np.sin(ang)
    g = np.concatenate([np.concatenate([gr, -gi], 2), np.concatenate([gi, gr], 2)], 1)
    grt, git = np.swapaxes(gr, 1, 2), np.swapaxes(gi, 1, 2)
    ginv = np.concatenate([np.concatenate([grt, git], 2), np.concatenate([-git, grt], 2)], 1)
    half = q // 2
    return (f1.astype(np.float32), f1[:, :half].astype(np.float32), f3[:half].astype(np.float32),
            g.astype(np.float32), ginv.astype(np.float32))


def _dft_consts_one_stage(seq):
    n = 2 * seq
    t = np.arange(n)[None, :]
    f = np.arange(n)[:, None]
    ang = 2.0 * np.pi * t * f / n
    fwd = np.concatenate([np.cos(ang), -np.sin(ang)], 0)
    inv = np.concatenate([np.cos(ang), -np.sin(ang)], 1)[:seq] / n
    return fwd.astype(np.float32), fwd[:, :seq].astype(np.float32), inv.astype(np.float32)


def _expand_rows(f, rows=DFT_ROWS):
    return np.kron(f, np.eye(rows, dtype=f.dtype))


def _stage_kernel(f_ref, x_ref, o_ref, *, exact):
    k, rows, ch = x_ref.shape[1:]
    x = x_ref[0].reshape(k * rows, ch)
    if exact:
        res = jnp.dot(f_ref[...], x, precision=HIGHEST, preferred_element_type=F32)
    else:
        res = _dot_bf16(f_ref[...], x.astype(BF16))
    o_ref[0] = res.reshape(o_ref.shape[1:]).astype(o_ref.dtype)


def _stage(f, x, exact=False):
    bsz, k, p, ch = x.shape
    rr = f.shape[0]
    rows = DFT_ROWS // 2 if exact else DFT_ROWS
    fx = jnp.asarray(_expand_rows(f, rows), dtype=F32 if exact else BF16)
    return pl.pallas_call(
        functools.partial(_stage_kernel, exact=exact),
        grid=(bsz, p // rows),
        in_specs=[pl.BlockSpec(fx.shape, lambda b, j: (0, 0)),
                  pl.BlockSpec((1, k, rows, ch), lambda b, j: (b, 0, j, 0))],
        out_specs=pl.BlockSpec((1, rr, rows, ch), lambda b, j: (b, 0, j, 0)),
        out_shape=jax.ShapeDtypeStruct((bsz, rr, p, ch), F32 if exact else BF16),
        compiler_params=_cparams(("parallel", "parallel")),
        name="dft_stage",
    )(fx, x)


def _spectrum_kernel(a_ref, g_ref, ssq_ref, o_ref):
    x = jnp.dot(g_ref[0], a_ref[0, 0], precision=HIGHEST, preferred_element_type=F32)
    o_ref[0, 0] = x * lax.rsqrt(ssq_ref[0] + 1e-6)


def _spectrum(a, g, ssq):
    nb, q, k, ch = a.shape
    rr = g.shape[1]
    return pl.pallas_call(
        _spectrum_kernel,
        grid=(nb, q),
        in_specs=[pl.BlockSpec((1, 1, k, ch), lambda b, j: (b, j, 0, 0)),
                  pl.BlockSpec((1, rr, k), lambda b, j: (j, 0, 0)),
                  pl.BlockSpec((1, 1, ch), lambda b, j: (b, 0, 0))],
        out_specs=pl.BlockSpec((1, 1, rr, ch), lambda b, j: (b, j, 0, 0)),
        out_shape=jax.ShapeDtypeStruct((nb, q, rr, ch), F32),
        compiler_params=_cparams(("parallel", "parallel")),
        name="hyena_spectrum",
    )(a, g, ssq)


def _spectral_conv(a, g_ref, h_ref, gi_ref):
    x = _dot_bf16(g_ref[0], a.astype(BF16))
    half = x.shape[0] // 2
    xr, xi = x[:half], x[half:]
    hr, hi = h_ref[0, 0, :half], h_ref[0, 0, half:]
    y = jnp.concatenate([xr * hr - xi * hi, xr * hi + xi * hr], 0)
    return _dot_bf16(gi_ref[0], y.astype(BF16))


def _mid_kernel(a_ref, g_ref, h_ref, gi_ref, o_ref):
    for b in range(a_ref.shape[0]):
        o_ref[b, 0] = _spectral_conv(a_ref[b, 0], g_ref, h_ref, gi_ref).astype(o_ref.dtype)


def _mid_gate_kernel(a_ref, g_ref, h_ref, gi_ref, hx_ref, bias_ref, o_ref):
    for b in range(a_ref.shape[0]):
        a = a_ref[b, 0]
        o_ref[b, 0] = hx_ref[b, 0] * (_spectral_conv(a, g_ref, h_ref, gi_ref) + bias_ref[...] * a)


def _mid(a, g, h, gi, order, gate=None, out_dtype=F32):
    bsz, q, k, ch = a.shape
    rr = g.shape[1]
    ro = gi.shape[1]
    nb = min(bsz, MID_BATCH)
    act = lambda j, b: (b, j, 0, 0)
    in_specs = [pl.BlockSpec((nb, 1, k, ch), act),
                pl.BlockSpec((1, rr, k), lambda j, b: (j, 0, 0)),
                pl.BlockSpec((1, 1, rr, ch), lambda j, b: (order, j, 0, 0)),
                pl.BlockSpec((1, ro, rr), lambda j, b: (j, 0, 0))]
    args = [a, g, h, gi]
    kern = _mid_kernel
    if gate is not None:
        in_specs += [pl.BlockSpec((nb, 1, ro, ch), act), pl.BlockSpec((1, ch), lambda j, b: (0, 0))]
        args += list(gate)
        kern = _mid_gate_kernel
    return pl.pallas_call(
        kern,
        grid=(q, bsz // nb),
        in_specs=in_specs,
        out_specs=pl.BlockSpec((nb, 1, ro, ch), act),
        out_shape=jax.ShapeDtypeStruct((bsz, q, ro, ch), out_dtype),
        compiler_params=_cparams(("parallel", "parallel")),
        name="hyena_mid",
    )(*args)


def _last_stage_kernel(f3_ref, bt_ref, hx_ref, z_ref, bias_ref, o_ref):
    k, rows, ch = bt_ref.shape[1:]
    conv = _dot_bf16(f3_ref[...], bt_ref[0].reshape(k * rows, ch))
    flat = conv.shape
    out = hx_ref[0].reshape(flat) * (conv + bias_ref[...] * z_ref[0].reshape(flat))
    o_ref[0] = out.reshape(o_ref.shape[1:])


def _last_stage(f3, bt, hx, z, bias):
    bsz, k, p, ch = bt.shape
    rr = f3.shape[0]
    fx = jnp.asarray(_expand_rows(f3), dtype=BF16)
    act = pl.BlockSpec((1, rr, DFT_ROWS, ch), lambda b, j: (b, 0, j, 0))
    return pl.pallas_call(
        _last_stage_kernel,
        grid=(bsz, p // DFT_ROWS),
        in_specs=[pl.BlockSpec(fx.shape, lambda b, j: (0, 0)),
                  pl.BlockSpec((1, k, DFT_ROWS, ch), lambda b, j: (b, 0, j, 0)),
                  act, act,
                  pl.BlockSpec((1, ch), lambda b, j: (0, 0))],
        out_specs=act,
        out_shape=jax.ShapeDtypeStruct((bsz, rr, p, ch), F32),
        compiler_params=_cparams(("parallel", "parallel")),
        name="dft_last_stage",
    )(fx, bt, hx, z, bias)


def _hyena(hv, hx1, hx2, circ, ssq, bias, bsz, seq):
    ch = W_HALF
    ssq = ssq.reshape(2, 1, ch)
    if seq <= 512:
        fwd_full, fwd_half, inv = _dft_consts_one_stage(seq)
        spec = _spectrum(circ[:, None], jnp.asarray(fwd_full)[None], ssq)
        fwd_half = jnp.asarray(fwd_half, dtype=BF16)[None]
        inv = jnp.asarray(inv, dtype=BF16)[None]
        shape4 = (bsz, 1, seq, ch)
        z = _mid(hv.reshape(shape4), fwd_half, spec, inv, 0, gate=(hx1.reshape(shape4), bias[0:1]))
        y = _mid(z, fwd_half, spec, inv, 1, gate=(hx2.reshape(shape4), bias[1:2]))
        return y.reshape(bsz * seq, ch)
    f1_full, f1_half, f3, g, ginv = _dft_consts_two_stage(seq)
    p = DFT_P
    q = 2 * seq // p
    spec = _spectrum(_stage(f1_full, circ.reshape(2, q, p, ch), exact=True).reshape(2, q, 2 * p, ch),
                     jnp.asarray(g), ssq)
    g = jnp.asarray(g, dtype=BF16)
    ginv = jnp.asarray(ginv, dtype=BF16)
    nat = (bsz, q // 2, p, ch)
    out = hv.reshape(nat)
    for order, hx in enumerate((hx1, hx2)):
        a = _stage(f1_half, out).reshape(bsz, q, 2 * p, ch)
        bt = _mid(a, g, spec, ginv, order, out_dtype=BF16).reshape(bsz, 2 * q, p, ch)
        out = _last_stage(f3, bt, hx.reshape(nat), out, bias[order:order + 1])
    return out.reshape(bsz * seq, ch)


def _block_diag(w):
    hh, blk, _ = w.shape
    eye = jnp.eye(hh, dtype=w.dtype)
    return jnp.einsum('hij,hg->higj', w, eye).reshape(hh * blk, hh * blk)


def _group_ones(width):
    idx = np.arange(width) // RW_HEAD
    return jnp.asarray((idx[:, None] == idx[None, :]).astype(np.float32), dtype=BF16)


def _even_layer(x, mods, h0, bsz, seq, line, pw):
    ya, gg, a, u = _ab_in(x, mods, pw['w_in'], pw['sc_conv'], pw['lru_conv'], pw['lru_conv_b'],
                          pw['wg'], pw['bg'], pw['nsp'], line)
    hf, hb, fin = _lru_scan(a, u, h0, bsz, seq)
    x = _ab_out(x, mods, ya, gg, hf.reshape(-1, W_HALF), hb.reshape(-1, W_HALF), pw)
    return x, fin


def _odd_layer(x, mods, s0, bsz, seq, line, pw, grid_rows=None):
    ones = _group_ones(MXU_TILE)
    (hv, hx1, hx2, r, v, kk, w, kd, b, bonus, sg) = _cd_in(
        x, mods, pw['w_in'], pw['hy_conv'], pw['rw_mu'], pw['rw_mu_x'], pw['l1'], pw['l2'], pw['l0'],
        pw['rw_kk'], pw['rw_ka'], pw['rw_rk'], ones, line, grid_rows)
    filt, ssq = _hyena_filters(seq, pw['hy_w1'], pw['hy_b1'], pw['hy_w2'], pw['hy_b2'], pw['hy_w3'],
                               pw['hy_freq'])
    yc = _hyena(hv, hx1, hx2, filt, ssq, pw['hy_bias'], bsz, seq)

    s0 = jnp.transpose(s0, (0, 1, 3, 2, 4)).reshape(bsz, 2, RW_HEAD, W_HALF)
    lane = np.arange(W_HALF)
    eye = jnp.asarray(((lane[None, :] % RW_HEAD) == np.arange(RW_HEAD)[:, None]).astype(np.float32),
                      dtype=BF16)
    of, ob, fin = _rwkv_scan(r, v, kk, w, kd, b, s0, _group_ones(SCAN_LANES), eye, bsz, seq)
    x = _cd_out(x, mods, yc, of.reshape(-1, W_HALF), ob.reshape(-1, W_HALF), bonus, sg, ones, pw, grid_rows)
    fin = jnp.transpose(fin.reshape(bsz, 2, RW_HEAD, RW_HEADS, RW_HEAD), (0, 1, 3, 2, 4))
    return x, fin


def _layer_weights(p, l):
    j = l // 2
    row = lambda a: a.reshape(1, -1)
    pw = {'w_out': p['w_out'][l].astype(BF16), 'ln1_g': row(p['ln1_g'][l]), 'ln1_b': row(p['ln1_b'][l]),
          'ln2_g': row(p['ln2_g'][l]), 'ln2_b': row(p['ln2_b'][l]),
          'mlp_w1': p['mlp_w1'][l].astype(BF16), 'mlp_w2': p['mlp_w2'][l].astype(BF16)}
    if l % 2 == 0:
        gates = [_block_diag(p[name][j, d]) for d in range(2) for name in ('lru_wa', 'lru_wi')]
        biases = [p[name][j, d] for d in range(2) for name in ('lru_ba', 'lru_bi')]
        pw.update({'w_in': p['ab_w_in'][j].astype(BF16), 'sc_conv': p['sc_conv'][j],
                   'lru_conv': p['lru_conv'][j], 'lru_conv_b': row(p['lru_conv_b'][j]),
                   'wg': jnp.concatenate(gates, 1).astype(BF16), 'bg': row(jnp.concatenate(biases)),
                   'nsp': jax.nn.softplus(-p['lru_lambda'][j])})
    else:
        zeros = jnp.zeros((64, W_HALF), F32)
        l2 = jnp.concatenate([
            jnp.concatenate([p['rw_w2'][j, 0], zeros], 1), jnp.concatenate([zeros, p['rw_w2'][j, 1]], 1),
            jnp.concatenate([p['rw_a2'][j, 0], zeros], 1), jnp.concatenate([zeros, p['rw_a2'][j, 1]], 1)], 0)
        pw.update({'w_in': p['cd_w_in'][j].astype(BF16), 'hy_conv': p['hy_conv'][j],
                   'rw_mu': p['rw_mu'][j], 'rw_mu_x': p['rw_mu_x'][j],
                   'l1': jnp.concatenate([p['rw_w1'][j, 0], p['rw_w1'][j, 1],
                                          p['rw_a1'][j, 0], p['rw_a1'][j, 1]], 1).astype(BF16),
                   'l2': l2.astype(BF16),
                   'l0': jnp.stack([p['rw_w0'][j].reshape(-1), p['rw_a0'][j].reshape(-1)]),
                   'rw_kk': row(p['rw_kk'][j]), 'rw_ka': row(p['rw_ka'][j]), 'rw_rk': row(p['rw_rk'][j]),
                   'rw_gn_g': row(p['rw_gn_g'][j]), 'rw_gn_b': row(p['rw_gn_b'][j]),
                   'hy_w1': p['hy_w1'][j], 'hy_b1': p['hy_b1'][j], 'hy_w2': p['hy_w2'][j],
                   'hy_b2': p['hy_b2'][j], 'hy_w3': p['hy_w3'][j], 'hy_freq': p['hy_freq'][j],
                   'hy_bias': p['hy_bias'][j]})
    return pw


def _trunk(x, mods, init_lru, init_rwkv, rows, weights):
    bsz, seq, _ = x.shape
    x = x.reshape(bsz * seq, D_MODEL)
    new_lru, new_rwkv = [], []
    for l in range(DEPTH):
        j = l // 2
        pw = weights[l]
        if l % 2 == 0:
            line = seq if rows is None else GRID_W
            x, st = _even_layer(x, mods[l], init_lru[:, j], bsz, seq, line, pw)
            new_lru.append(st)
        elif rows is None:
            x, st = _odd_layer(x, mods[l], init_rwkv[:, j], bsz, seq, seq, pw)
            new_rwkv.append(st)
        else:
            x, st = _odd_layer(x, mods[l], init_rwkv[:, j], bsz, seq, rows, pw, grid_rows=rows)
            new_rwkv.append(st)
    return x.reshape(bsz, seq, D_MODEL), jnp.stack(new_lru, 1), jnp.stack(new_rwkv, 1)


def kernel(x_prompt, x_sample, state_lru, state_rwkv, c, c_ctx, w_mod, b_mod, ln1_g, ln1_b, ln2_g, ln2_b, mlp_w1, mlp_w2, w_out, ab_w_in, sc_conv, lru_conv, lru_conv_b, lru_wa, lru_ba, lru_wi, lru_bi, lru_lambda, cd_w_in, hy_conv, hy_w1, hy_b1, hy_w2, hy_b2, hy_w3, hy_freq, hy_bias, rw_mu, rw_mu_x, rw_w0, rw_w1, rw_w2, rw_a0, rw_a1, rw_a2, rw_kk, rw_ka, rw_rk, rw_gn_g, rw_gn_b):
    p = dict(ln1_g=ln1_g, ln1_b=ln1_b, ln2_g=ln2_g, ln2_b=ln2_b,
             mlp_w1=mlp_w1, mlp_w2=mlp_w2, w_out=w_out, ab_w_in=ab_w_in, sc_conv=sc_conv,
             lru_conv=lru_conv, lru_conv_b=lru_conv_b, lru_wa=lru_wa, lru_ba=lru_ba, lru_wi=lru_wi,
             lru_bi=lru_bi, lru_lambda=lru_lambda, cd_w_in=cd_w_in, hy_conv=hy_conv, hy_w1=hy_w1,
             hy_b1=hy_b1, hy_w2=hy_w2, hy_b2=hy_b2, hy_w3=hy_w3, hy_freq=hy_freq, hy_bias=hy_bias,
             rw_mu=rw_mu, rw_mu_x=rw_mu_x, rw_w0=rw_w0, rw_w1=rw_w1, rw_w2=rw_w2, rw_a0=rw_a0,
             rw_a1=rw_a1, rw_a2=rw_a2, rw_kk=rw_kk, rw_ka=rw_ka, rw_rk=rw_rk, rw_gn_g=rw_gn_g,
             rw_gn_b=rw_gn_b)
    weights = [_layer_weights(p, l) for l in range(DEPTH)]
    nb, dec = x_prompt.shape[0], x_sample.shape[0]
    rpad = -(1 + dec) % 8
    cvec = jnp.concatenate([c_ctx[None, :], c, jnp.zeros((rpad, D_MODEL), F32)], 0)
    mods = _mods(cvec, w_mod, b_mod)
    mods = jnp.transpose(mods, (0, 2, 1, 3))
    zero_lru = jnp.zeros((nb, (DEPTH + 1) // 2, 2, W_HALF), F32)
    zero_rwkv = jnp.zeros((nb, DEPTH // 2, 2, RW_HEADS, RW_HEAD, RW_HEAD), F32)
    y_prompt, new_lru, new_rwkv = _trunk(x_prompt, mods[:, 0:1], zero_lru, zero_rwkv, None, weights)
    rows = x_sample.shape[1] // GRID_W
    y_sample, _, _ = _trunk(x_sample, mods[:, 1:1 + dec], state_lru, state_rwkv, rows, weights)
    return (y_prompt, y_sample, new_lru, new_rwkv)
```
